```python
import jax
import jax.numpy as jnp
from jax import lax
import numpy as np

D_MODEL = 1024
BATCH = 4
SEQ = 4096
DEPTH = 2

NORM_EPS = 1e-6
ROPE_BASE = 10000.0

RW_HEADS = 4
RW_HEAD_DIM = 64
RW_DIM = RW_HEADS * RW_HEAD_DIM
RW_DECAY_RANK = 32
RW_A_RANK = 32
RW_V_RANK = 32
RW_GATE_RANK = 64
RW_GN_EPS = 64e-5
RW_IN = 3 * RW_DIM + RW_DECAY_RANK + RW_A_RANK + RW_GATE_RANK

RET_HEADS = 4
RET_HEAD_DIM = 64
RET_DIM = RET_HEADS * RET_HEAD_DIM
RET_CHUNK = 128
RET_IN = 4 * RET_DIM

MLA_HEADS = 8
MLA_NOPE = 64
MLA_ROPE = 32
MLA_V = 64
MLA_Q_RANK = 384
MLA_KV_RANK = 256
MLA_DIM = MLA_HEADS * MLA_V
MLA_IN = MLA_Q_RANK + MLA_KV_RANK + MLA_ROPE
ATTN_BLOCK = 128

IN_DIM = RW_IN + RET_IN + MLA_IN
MIX_DIM = RW_DIM + RET_DIM + MLA_DIM
D_FF = -(-(8 * D_MODEL) // (3 * 256)) * 256

kernel_name = 'hymba_rwkv7_retnet_mla_block'


def rms_norm(x, g):
    xf = x.astype(jnp.float32)
    y = xf * lax.rsqrt(jnp.mean(xf * xf, axis=-1, keepdims=True) + NORM_EPS)
    return (y * g.astype(jnp.float32)).astype(x.dtype)


def rope(x, positions):
    d = x.shape[-1]
    inv = ROPE_BASE ** (-jnp.arange(0, d, 2, dtype=jnp.float32) / d)
    ang = positions.astype(jnp.float32)[..., None] * inv
    ang = ang.reshape(ang.shape[:2] + (1,) * (x.ndim - 3) + ang.shape[-1:])
    cos, sin = jnp.cos(ang), jnp.sin(ang)
    xf = x.astype(jnp.float32)
    x1, x2 = xf[..., : d // 2], xf[..., d // 2:]
    return jnp.concatenate([x1 * cos - x2 * sin, x2 * cos + x1 * sin], axis=-1).astype(x.dtype)


def token_shift(p):
    return jnp.pad(p, ((0, 0), (1, 0), (0, 0)))[:, :-1]


def rwkv7_mixer(p, mu, w0, w2, a0, a2, g2, k_k, k_a, r_k, gn_w, gn_b, v_first, v_res):
    B, S, _ = p.shape
    p = p.astype(jnp.float32)
    ps = p + (token_shift(p) - p) * mu
    o1, o2, o3 = RW_DIM, 2 * RW_DIM, 3 * RW_DIM
    o4, o5 = o3 + RW_DECAY_RANK, o3 + RW_DECAY_RANK + RW_A_RANK
    r, k, v = ps[..., :o1], ps[..., o1:o2], ps[..., o2:o3]
    wd, ad, gd = ps[..., o3:o4], ps[..., o4:o5], ps[..., o5:]
    w = -jax.nn.softplus(-(w0 + jnp.tanh(wd) @ w2)) - 0.5
    decay = jnp.exp(-jnp.exp(w))
    a = jax.nn.sigmoid(a0 + ad @ a2)
    g = jax.nn.sigmoid(gd) @ g2
    if v_res is None:
        v_first = v
    else:
        v0, v1, v2 = v_res
        v = v + (v_first - v) * jax.nn.sigmoid(v0 + (v @ v1) @ v2)
    heads = lambda t: t.reshape(B, S, RW_HEADS, RW_HEAD_DIM)
    kk = heads(k * k_k)
    kk = kk * lax.rsqrt(jnp.maximum(jnp.sum(kk * kk, axis=-1, keepdims=True), 1e-24))
    k = k * (1.0 + (a - 1.0) * k_a)
    r_h, k_h, v_h, w_h, a_h = heads(r), heads(k), heads(v), heads(decay), heads(a)

    def step(state, inp):
        rt, kt, vt, wt, kkt, at = inp
        sa = jnp.einsum('bhvk,bhk->bhv', state, -kkt)
        state = (state * wt[:, :, None, :] + sa[..., None] * (kkt * at)[:, :, None, :]
                 + vt[..., None] * kt[:, :, None, :])
        return state, jnp.einsum('bhvk,bhk->bhv', state, rt)

    xs = tuple(jnp.moveaxis(t, 1, 0) for t in (r_h, k_h, v_h, w_h, kk, a_h))
    s0 = jnp.zeros((B, RW_HEADS, RW_HEAD_DIM, RW_HEAD_DIM), jnp.float32)
    _, y = lax.scan(step, s0, xs)
    y = jnp.moveaxis(y, 0, 1)
    mean = jnp.mean(y, axis=-1, keepdims=True)
    var = jnp.mean(jnp.square(y - mean), axis=-1, keepdims=True)
    y = ((y - mean) * lax.rsqrt(var + RW_GN_EPS)).reshape(B, S, RW_DIM) * gn_w + gn_b
    bonus = jnp.sum(r_h * k_h * r_k, axis=-1, keepdims=True) * v_h
    y = (y + bonus.reshape(B, S, RW_DIM)) * g
    return y, v_first


def retention_mixer(p, positions):
    B, S, _ = p.shape
    p = p.astype(jnp.float32)
    q, k, v, g = jnp.split(p, 4, axis=-1)
    heads = lambda t: t.reshape(B, S, RET_HEADS, RET_HEAD_DIM)
    q = rope(heads(q), positions)
    k = rope(heads(k), positions) * (RET_HEAD_DIM ** -0.5)
    v = heads(v)
    log_gamma = jnp.log(1.0 - 2.0 ** (-5.0 - jnp.arange(RET_HEADS, dtype=jnp.float32)))
    n_chunks = S // RET_CHUNK
    chunk = lambda t: t.reshape(B, n_chunks, RET_CHUNK, RET_HEADS, RET_HEAD_DIM).transpose(0, 3, 1, 2, 4)
    qc, kc, vc = chunk(q), chunk(k), chunk(v)
    idx = jnp.arange(RET_CHUNK, dtype=jnp.float32)
    diff = idx[:, None] - idx[None, :]
    dmask = jnp.where(diff >= 0, jnp.exp(log_gamma[:, None, None] * jnp.maximum(diff, 0.0)), 0.0)
    inner = jnp.einsum('bhnqd,bhnkd->bhnqk', qc, kc) * dmask[:, None]
    inner = jnp.einsum('bhnqk,bhnkd->bhnqd', inner, vc)
    q_dec = jnp.exp(log_gamma[:, None] * (idx + 1.0))
    k_dec = jnp.exp(log_gamma[:, None] * (RET_CHUNK - 1.0 - idx))
    chunk_dec = jnp.exp(log_gamma * RET_CHUNK)
    kv_chunk = jnp.einsum('bhnkd,bhnke->nbhde', kc * k_dec[:, None, :, None], vc)

    def step(R, kv_i):
        return R * chunk_dec[None, :, None, None] + kv_i, R

    R0 = jnp.zeros((B, RET_HEADS, RET_HEAD_DIM, RET_HEAD_DIM), jnp.float32)
    _, R_prev = lax.scan(step, R0, kv_chunk)
    cross = jnp.einsum('bhnqd,nbhde->bhnqe', qc * q_dec[:, None, :, None], R_prev)
    y = (inner + cross).transpose(0, 2, 3, 1, 4).reshape(B, S, RET_HEADS, RET_HEAD_DIM)
    y = y * lax.rsqrt(jnp.mean(y * y, axis=-1, keepdims=True) + NORM_EPS)
    return jax.nn.silu(g) * y.reshape(B, S, RET_DIM)


def mla_mixer(p, positions, q_norm, kv_norm, w_q_up, w_kv_up):
    B, S, _ = p.shape
    q_lat = p[..., :MLA_Q_RANK]
    kv_lat = p[..., MLA_Q_RANK:MLA_Q_RANK + MLA_KV_RANK]
    k_rope = rope(p[..., MLA_Q_RANK + MLA_KV_RANK:], positions)
    q = (rms_norm(q_lat, q_norm) @ w_q_up).reshape(B, S, MLA_HEADS, MLA_NOPE + MLA_ROPE)
    q_nope, q_rope = q[..., :MLA_NOPE], rope(q[..., MLA_NOPE:], positions)
    kv = (rms_norm(kv_lat, kv_norm) @ w_kv_up).reshape(B, S, MLA_HEADS, MLA_NOPE + MLA_V)
    k_nope, v = kv[..., :MLA_NOPE], kv[..., MLA_NOPE:]
    scale = (MLA_NOPE + MLA_ROPE) ** -0.5
    n_blocks = S // ATTN_BLOCK
    blk = lambda t: t.reshape(B, n_blocks, ATTN_BLOCK, MLA_HEADS, t.shape[-1]).transpose(1, 0, 3, 2, 4)
    qn_b, qr_b = blk(q_nope), blk(q_rope)
    key_idx = jnp.arange(S)

    def attend(args):
        qn, qr, i = args
        s = (jnp.einsum('bhqd,bkhd->bhqk', qn, k_nope)
             + jnp.einsum('bhqr,bkr->bhqk', qr, k_rope)).astype(jnp.float32) * scale
        q_idx = i * ATTN_BLOCK + jnp.arange(ATTN_BLOCK)
        s = jnp.where(key_idx[None, :] <= q_idx[:, None], s, -jnp.inf)
        pr = jax.nn.softmax(s, axis=-1).astype(v.dtype)
        return jnp.einsum('bhqk,bkhd->bqhd', pr, v)

    o = lax.map(attend, (qn_b, qr_b, jnp.arange(n_blocks)))
    return o.transpose(1, 0, 2, 3, 4).reshape(B, S, MLA_DIM)


def setup_inputs(seed: int = 0) -> dict:
    key = jax.random.key(seed)
    keys = jax.random.split(key, 32)
    nrm = lambda i, shape, scale: scale * jax.random.normal(keys[i], shape, jnp.float32)
    L = DEPTH
    ch = jnp.linspace(0.0, 1.0, RW_DIM, dtype=jnp.float32)
    return {
        'x': nrm(0, (BATCH, SEQ, D_MODEL), 1.0),
        'positions': jnp.broadcast_to(jnp.arange(SEQ, dtype=jnp.int32), (BATCH, SEQ)),
        'attn_norm': 1.0 + nrm(1, (L, D_MODEL), 0.05),
        'w_in': nrm(2, (L, D_MODEL, IN_DIM), D_MODEL ** -0.5),
        'w_out': nrm(3, (L, MIX_DIM, D_MODEL), MIX_DIM ** -0.5),
        'rw_mu': jax.random.uniform(keys[4], (L, RW_IN), jnp.float32),
        'rw_w0': -6.0 + 5.0 * ch ** 0.85 + nrm(5, (L, RW_DIM), 0.1),
        'rw_w2': nrm(6, (L, RW_DECAY_RANK, RW_DIM), 0.5 * RW_DECAY_RANK ** -0.5),
        'rw_a0': nrm(7, (L, RW_DIM), 0.1),
        'rw_a2': nrm(8, (L, RW_A_RANK, RW_DIM), RW_A_RANK ** -0.5),
        'rw_g2': nrm(9, (L, RW_GATE_RANK, RW_DIM), RW_GATE_RANK ** -0.5),
        'rw_k_k': 0.85 + nrm(10, (L, RW_DIM), 0.05),
        'rw_k_a': 1.0 + nrm(11, (L, RW_DIM), 0.05),
        'rw_r_k': nrm(12, (L, RW_HEADS, RW_HEAD_DIM), 0.1),
        'rw_gn_w': 1.0 + nrm(13, (L, RW_DIM), 0.05),
        'rw_gn_b': nrm(14, (L, RW_DIM), 0.01),
        'rw_v0': 1.0 + nrm(15, (L - 1, RW_DIM), 0.1),
        'rw_v1': nrm(16, (L - 1, RW_DIM, RW_V_RANK), RW_DIM ** -0.5),
        'rw_v2': nrm(17, (L - 1, RW_V_RANK, RW_DIM), 0.5 * RW_V_RANK ** -0.5),
        'mla_q_norm': 1.0 + nrm(18, (L, MLA_Q_RANK), 0.05),
        'mla_kv_norm': 1.0 + nrm(19, (L, MLA_KV_RANK), 0.05),
        'mla_w_q_up': nrm(20, (L, MLA_Q_RANK, MLA_HEADS * (MLA_NOPE + MLA_ROPE)), MLA_Q_RANK ** -0.5),
        'mla_w_kv_up': nrm(21, (L, MLA_KV_RANK, MLA_HEADS * (MLA_NOPE + MLA_V)), MLA_KV_RANK ** -0.5),
        'ffn_norm': 1.0 + nrm(22, (L, D_MODEL), 0.05),
        'w_gate_up': nrm(23, (L, D_MODEL, 2 * D_FF), D_MODEL ** -0.5),
        'w_down': nrm(24, (L, D_FF, D_MODEL), D_FF ** -0.5),
        'final_norm': 1.0 + nrm(25, (D_MODEL,), 0.05),
    }


def reference(x, positions, attn_norm, w_in, w_out, rw_mu, rw_w0, rw_w2, rw_a0, rw_a2, rw_g2,
              rw_k_k, rw_k_a, rw_r_k, rw_gn_w, rw_gn_b, rw_v0, rw_v1, rw_v2, mla_q_norm, mla_kv_norm,
              mla_w_q_up, mla_w_kv_up, ffn_norm, w_gate_up, w_down, final_norm):
    h = x
    v_first = None
    for l in range(DEPTH):
        hn = rms_norm(h, attn_norm[l])
        proj = hn @ w_in[l]
        p_a = proj[..., :RW_IN]
        p_b = proj[..., RW_IN:RW_IN + RET_IN]
        p_c = proj[..., RW_IN + RET_IN:]
        v_res = None if l == 0 else (rw_v0[l - 1], rw_v1[l - 1], rw_v2[l - 1])
        y_a, v_first = rwkv7_mixer(p_a, rw_mu[l], rw_w0[l], rw_w2[l], rw_a0[l], rw_a2[l], rw_g2[l],
                                   rw_k_k[l], rw_k_a[l], rw_r_k[l], rw_gn_w[l], rw_gn_b[l], v_first, v_res)
        y_b = retention_mixer(p_b, positions)
        y_c = mla_mixer(p_c, positions, mla_q_norm[l], mla_kv_norm[l], mla_w_q_up[l], mla_w_kv_up[l])
        mix = jnp.concatenate([y_a.astype(h.dtype), y_b.astype(h.dtype), y_c.astype(h.dtype)], axis=-1)
        h = h + mix @ w_out[l]
        hn = rms_norm(h, ffn_norm[l])
        gu = hn @ w_gate_up[l]
        h = h + (jax.nn.silu(gu[..., :D_FF]) * gu[..., D_FF:]) @ w_down[l]
    return rms_norm(h, final_norm)
```

```python
import functools
import math

import jax
import jax.numpy as jnp
import numpy as np
from jax import lax
from jax.experimental import pallas as pl
from jax.experimental.pallas import tpu as pltpu

F32 = jnp.float32
BF16 = jnp.bfloat16

NORM_EPS = 1e-6
ROPE_BASE = 10000.0

RW_HEADS = 4
RW_HEAD_DIM = 64
RW_DIM = RW_HEADS * RW_HEAD_DIM
RW_DECAY_RANK = 32
RW_A_RANK = 32
RW_V_RANK = 32
RW_GATE_RANK = 64
RW_GN_EPS = 64e-5
RW_IN = 3 * RW_DIM + RW_DECAY_RANK + RW_A_RANK + RW_GATE_RANK
RW_LOWRANK = RW_DECAY_RANK + RW_A_RANK + RW_GATE_RANK
RW_CHUNK = 64

RET_HEADS = 4
RET_HEAD_DIM = 64
RET_DIM = RET_HEADS * RET_HEAD_DIM
RET_CHUNK = 128
RET_IN = 4 * RET_DIM

MLA_HEADS = 8
MLA_NOPE = 64
MLA_ROPE = 32
MLA_V = 64
MLA_Q_RANK = 384
MLA_KV_RANK = 256
MLA_DIM = MLA_HEADS * MLA_V
MLA_IN = MLA_Q_RANK + MLA_KV_RANK + MLA_ROPE
MLA_HEAD_PAD = 128
MLA_C_PAD = 768

V7X_VMEM_LIMIT_BYTES = 56 * 1024 * 1024


def _cparams(semantics):
    return pltpu.CompilerParams(dimension_semantics=semantics, vmem_limit_bytes=V7X_VMEM_LIMIT_BYTES)


def _dot(a, b):
    return jnp.dot(a, b, preferred_element_type=F32)


def _dot_nt(a, b):
    return lax.dot_general(a, b, (((1,), (1,)), ((), ())), preferred_element_type=F32)


def _dot_tn(a, b):
    return lax.dot_general(a, b, (((0,), (0,)), ((), ())), preferred_element_type=F32)


def _split_dot(x, m_bf16, passes):
    acc = None
    rem = x
    for _ in range(passes):
        piece = rem.astype(BF16)
        part = _dot(piece, m_bf16)
        acc = part if acc is None else acc + part
        rem = rem - piece.astype(F32)
    return acc


def _rms(x, g):
    return x * lax.rsqrt(jnp.mean(x * x, axis=-1, keepdims=True) + NORM_EPS) * g


def _inproj_kernel(x_ref, g_ref, wa_ref, wb_ref, wc_ref, pa_ref, pb_ref, pc_ref):
    hn = _rms(x_ref[...], g_ref[...]).astype(BF16)
    pa_ref[...] = _dot(hn, wa_ref[...])
    pb_ref[...] = _dot(hn, wb_ref[...])
    pc_ref[...] = _dot(hn, wc_ref[...])


def _inproj(x, g, wa, wb, wc, tm):
    t, d = x.shape
    row = lambda n: pl.BlockSpec((tm, n), lambda i: (i, 0))
    full = lambda a: pl.BlockSpec(a.shape, lambda i: (0, 0))
    return pl.pallas_call(
        _inproj_kernel,
        grid=(t // tm,),
        in_specs=[row(d), full(g), full(wa), full(wb), full(wc)],
        out_specs=[row(wa.shape[1]), row(wb.shape[1]), row(wc.shape[1])],
        out_shape=[jax.ShapeDtypeStruct((t, w.shape[1]), F32) for w in (wa, wb, wc)],
        compiler_params=_cparams(("parallel",)),
        name="inproj",
    )(x, g, wa, wb, wc)


_VEC_ROWS = 16
(_V_W0, _V_A0, _V_KK, _V_KA, _V_RK, _V_GNW, _V_GNB, _V_V0) = range(8)


def _rwkv_kernel(*refs, has_vres, nb, tb):
    if has_vres:
        (p_ref, vf_ref, mu_ref, vec_ref, w2_ref, a2_ref, g2_ref, v1_ref, v2_ref, bd_ref, tri_ref,
         o_ref, carry_ref, s_ref, r_s, k_s, v_s, lw_s, kk_s, ka_s, g_s, bon_s, y_s) = refs
    else:
        (p_ref, mu_ref, vec_ref, w2_ref, a2_ref, g2_ref, bd_ref, tri_ref,
         o_ref, vfo_ref, carry_ref, s_ref, r_s, k_s, v_s, lw_s, kk_s, ka_s, g_s, bon_s, y_s) = refs

    L = RW_CHUNK
    C = RW_DIM
    tstep = pl.program_id(1)

    @pl.when(tstep == 0)
    def _():
        carry_ref[...] = jnp.zeros_like(carry_ref)
        s_ref[...] = jnp.zeros_like(s_ref)

    vec = vec_ref[...]
    row = lambda i: vec[i:i + 1, :]
    bd = bd_ref[...]
    mu = mu_ref[...]

    def bdsum(x):
        return _split_dot(x, bd, 2)

    def prep(b, _):
        p = p_ref[b]
        shifted = pltpu.roll(p, 1, 0)
        first = lax.broadcasted_iota(jnp.int32, p.shape, 0) == 0
        p_prev = jnp.where(first, carry_ref[b], shifted)
        carry_ref[b] = p[tb - 1:tb, :]
        ps = p + (p_prev - p) * mu
        r = ps[:, 0:C]
        k = ps[:, C:2 * C]
        v = ps[:, 2 * C:3 * C]
        lr = ps[:, 3 * C:3 * C + RW_LOWRANK]
        z = row(_V_W0) + _dot(jnp.tanh(lr).astype(BF16), w2_ref[...])
        lw = -math.exp(-0.5) * jax.nn.sigmoid(z)
        a = jax.nn.sigmoid(row(_V_A0) + _dot(lr.astype(BF16), a2_ref[...]))
        g = _dot(jax.nn.sigmoid(lr).astype(BF16), g2_ref[...])
        if has_vres:
            lat = _dot(v.astype(BF16), v1_ref[...]).astype(BF16)
            v = v + (vf_ref[b] - v) * jax.nn.sigmoid(row(_V_V0) + _dot(lat, v2_ref[...]))
        else:
            vfo_ref[b] = v
        kk = k * row(_V_KK)
        kk = kk * lax.rsqrt(jnp.maximum(bdsum(kk * kk), 1e-24))
        k2 = k * (1.0 + (a - 1.0) * row(_V_KA))
        r_s[b] = r
        k_s[b] = k2
        v_s[b] = v
        lw_s[b] = lw
        kk_s[b] = kk
        ka_s[b] = kk * a
        g_s[b] = g
        bon_s[b] = bdsum(r * k2 * row(_V_RK)) * v
        return 0

    lax.fori_loop(0, nb, prep, 0)

    hl = RW_HEADS * L
    r_i = lax.broadcasted_iota(jnp.int32, (hl, C), 0)
    c_i = lax.broadcasted_iota(jnp.int32, (hl, C), 1)
    headmask = (r_i // L) == (c_i // RW_HEAD_DIM)
    t_i = lax.broadcasted_iota(jnp.int32, (L, hl), 0)
    s_i = lax.broadcasted_iota(jnp.int32, (L, hl), 1) % L
    strict = t_i > s_i
    incl = t_i >= s_i
    tri = tri_ref[...]

    def rep4(x):
        return jnp.concatenate([x] * RW_HEADS, axis=0)

    def masked4(x):
        return jnp.where(headmask, rep4(x), 0.0).astype(BF16)

    def chunk(j, _):
        sl = pl.ds(pl.multiple_of(j * L, L), L)
        for b in range(nb):
            r = r_s[b, sl, :]
            k2 = k_s[b, sl, :]
            v = v_s[b, sl, :]
            lw = lw_s[b, sl, :]
            kk = kk_s[b, sl, :]
            ka = ka_s[b, sl, :]
            c = _split_dot_left(tri, lw)
            c_last = c[L - 1:L, :]
            sig = 0.5 * c_last
            e_pos = jnp.exp(c - sig)
            e_neg = jnp.exp(sig - c)
            e_prev = jnp.exp(c - lw - sig)
            rt = r * e_pos
            at = -(kk * e_prev)
            kt = k2 * e_neg
            bt = ka * e_neg
            x = jnp.concatenate([at, rt], axis=0).astype(BF16)
            kb = jnp.concatenate([masked4(kt), masked4(bt)], axis=0)
            ab = _dot_nt(x, kb)
            a_ak = jnp.where(strict, ab[0:L, 0:hl], 0.0)
            n_w = jnp.where(strict, ab[0:L, hl:2 * hl], 0.0)
            a_rk = jnp.where(incl, ab[L:2 * L, 0:hl], 0.0)
            a_rb = jnp.where(incl, ab[L:2 * L, hl:2 * hl], 0.0)
            av = _dot(jnp.concatenate([a_ak, a_rk], axis=0).astype(BF16), masked4(v))
            s_old = s_ref[b]
            xs = _dot_nt(x, (s_old * jnp.exp(sig)).astype(BF16))
            u = av[0:L] + xs[0:L]
            for i in range(6):
                u = u + _dot(n_w.astype(BF16), masked4(u))
                if i < 5:
                    n_w = _dot(n_w.astype(BF16), masked4(n_w))
            y_s[b, sl, :] = av[L:2 * L] + xs[L:2 * L] + _dot(a_rb.astype(BF16), masked4(u))
            scale = jnp.exp(c_last - sig)
            kbs = jnp.concatenate([kt * scale, bt * scale], axis=0).astype(BF16)
            vu = jnp.concatenate([v, u], axis=0).astype(BF16)
            upd = _dot_tn(vu, kbs)
            s_ref[b] = s_old * jnp.exp(c_last) + jnp.where(headmask, upd, 0.0)
        return 0

    lax.fori_loop(0, tb // L, chunk, 0)

    def post(b, _):
        y = y_s[b]
        mean = bdsum(y) * (1.0 / RW_HEAD_DIM)
        d = y - mean
        var = bdsum(d * d) * (1.0 / RW_HEAD_DIM)
        yn = d * lax.rsqrt(var + RW_GN_EPS) * row(_V_GNW) + row(_V_GNB)
        o_ref[b] = ((yn + bon_s[b]) * g_s[b]).astype(o_ref.dtype)
        return 0

    lax.fori_loop(0, nb, post, 0)


def _split_dot_left(m_bf16, x):
    acc = None
    rem = x
    for _ in range(3):
        piece = rem.astype(BF16)
        part = _dot(m_bf16, piece)
        acc = part if acc is None else acc + part
        rem = rem - piece.astype(F32)
    return acc


def _block_ones(n, blk):
    i = np.arange(n)
    return (i[:, None] // blk == i[None, :] // blk).astype(np.float32)


def _rwkv(p_a, v_first, mu, vec, w2p, a2p, g2p, v1p, v2p, nb, tb):
    bsz, s, _ = p_a.shape
    has_vres = v_first is not None
    bd = jnp.asarray(_block_ones(RW_DIM, RW_HEAD_DIM), BF16)
    tri = jnp.asarray(np.tril(np.ones((RW_CHUNK, RW_CHUNK), np.float32)), BF16)
    blk = lambda n: pl.BlockSpec((nb, tb, n), lambda i, j: (i, j, 0))
    full = lambda a: pl.BlockSpec(a.shape, lambda i, j: (0,) * a.ndim)
    if has_vres:
        args = (p_a, v_first, mu, vec, w2p, a2p, g2p, v1p, v2p, bd, tri)
        in_specs = [blk(RW_IN), blk(RW_DIM)] + [full(a) for a in args[2:]]
        out_specs = blk(RW_DIM)
        out_shape = jax.ShapeDtypeStruct((bsz, s, RW_DIM), BF16)
    else:
        args = (p_a, mu, vec, w2p, a2p, g2p, bd, tri)
        in_specs = [blk(RW_IN)] + [full(a) for a in args[1:]]
        out_specs = [blk(RW_DIM), blk(RW_DIM)]
        out_shape = [jax.ShapeDtypeStruct((bsz, s, RW_DIM), BF16), jax.ShapeDtypeStruct((bsz, s, RW_DIM), F32)]
    big = pltpu.VMEM((nb, tb, RW_DIM), F32)
    scratch = [pltpu.VMEM((nb, 1, RW_IN), F32), pltpu.VMEM((nb, RW_DIM, RW_DIM), F32)] + [big] * 9
    return pl.pallas_call(
        functools.partial(_rwkv_kernel, has_vres=has_vres, nb=nb, tb=tb),
        grid=(bsz // nb, s // tb),
        in_specs=in_specs,
        out_specs=out_specs,
        out_shape=out_shape,
        scratch_shapes=scratch,
        compiler_params=_cparams(("parallel", "arbitrary")),
        name="rwkv7",
    )(*args)


def _ret_tables():
    h, d, c = RET_HEADS, RET_HEAD_DIM, RET_CHUNK
    log_gamma = np.log(1.0 - 2.0 ** (-5.0 - np.arange(h, dtype=np.float64)))
    idx = np.arange(c, dtype=np.float64)
    diff = idx[:, None] - idx[None, :]
    dmask = np.where(diff >= 0, np.exp(log_gamma[:, None, None] * np.maximum(diff, 0.0)), 0.0)
    lane = np.arange(RET_DIM)
    head_qk = (lane % (RET_DIM // 2)) // (d // 2)
    head_v = lane // d
    qdec = np.exp(log_gamma[head_qk][None, :] * (idx[:, None] + 1.0))
    kdec = np.exp(log_gamma[head_qk][None, :] * (c - 1.0 - idx[:, None])) * d ** -0.5
    hm_qk = (np.arange(h)[:, None, None] == head_qk[None, None, :]) * np.ones((1, c, 1))
    hm_v = (np.arange(h)[:, None, None] == head_v[None, None, :]) * np.ones((1, c, 1))
    block = head_qk[:, None] == head_v[None, :]
    rdec = np.where(block, np.exp(log_gamma[head_qk] * c)[:, None], 0.0)
    f = lambda a: jnp.asarray(a, F32)
    return (f(dmask.reshape(h * c, c) * d ** -0.5), f(qdec), f(kdec), f(hm_qk.reshape(h * c, RET_DIM)),
            f(hm_v.reshape(h * c, RET_DIM)), f(block), f(rdec))


def _ret_kernel(p_ref, cos_ref, sin_ref, dmask_ref, qdec_ref, kdec_ref, hmqk_ref, hmv_ref, block_ref, rdec_ref,
                bd_ref, o_ref, r_ref):
    c, dm = RET_CHUNK, RET_DIM
    half = dm // 2

    @pl.when(pl.program_id(1) == 0)
    def _():
        r_ref[...] = jnp.zeros_like(r_ref)

    p = p_ref[0]
    cos = cos_ref[0]
    sin = sin_ref[0]

    def rope(x):
        x1, x2 = x[:, :half], x[:, half:]
        return jnp.concatenate([x1 * cos - x2 * sin, x2 * cos + x1 * sin], axis=1)

    q = rope(p[:, 0:dm])
    k = rope(p[:, dm:2 * dm])
    v = p[:, 2 * dm:3 * dm]
    g = p[:, 3 * dm:4 * dm]
    vb = v.astype(BF16)
    q_bd = (jnp.concatenate([q] * RET_HEADS, axis=0) * hmqk_ref[...]).astype(BF16)
    scores = _dot_nt(q_bd, k.astype(BF16)) * dmask_ref[...]
    o = _dot(scores.astype(BF16), vb) * hmv_ref[...]
    inner = o[0:c] + o[c:2 * c] + o[2 * c:3 * c] + o[3 * c:4 * c]
    r_old = r_ref[...]
    cross = _dot((q * qdec_ref[...]).astype(BF16), r_old.astype(BF16))
    r_ref[...] = r_old * rdec_ref[...] + _dot_tn((k * kdec_ref[...]).astype(BF16), vb) * block_ref[...]
    y = inner + cross
    ms = _split_dot(y * y, bd_ref[...], 2) * (1.0 / RET_HEAD_DIM)
    y = y * lax.rsqrt(ms + NORM_EPS)
    o_ref[0] = (g * jax.nn.sigmoid(g) * y).astype(o_ref.dtype)


def _retention(p_b, cos, sin):
    bsz, s, _ = p_b.shape
    tabs = _ret_tables()
    bd = jnp.asarray(_block_ones(RET_DIM, RET_HEAD_DIM), BF16)
    blk = lambda n: pl.BlockSpec((1, RET_CHUNK, n), lambda i, j: (i, j, 0))
    full = lambda a: pl.BlockSpec(a.shape, lambda i, j: (0,) * a.ndim)
    consts = tabs + (bd,)
    return pl.pallas_call(
        _ret_kernel,
        grid=(bsz, s // RET_CHUNK),
        in_specs=[blk(RET_IN), blk(RET_DIM // 2), blk(RET_DIM // 2)] + [full(a) for a in consts],
        out_specs=blk(RET_DIM),
        out_shape=jax.ShapeDtypeStruct((bsz, s, RET_DIM), BF16),
        scratch_shapes=[pltpu.VMEM((RET_DIM, RET_DIM), F32)],
        compiler_params=_cparams(("parallel", "arbitrary")),
        name="retention",
    )(p_b, cos, sin, *consts)


def _mla_proj_kernel(pc_ref, cq_ref, sq_ref, ck_ref, qn_ref, kn_ref, wq_ref, wqr_ref, wk_ref, wv_ref, place_ref,
                     q_ref, k_ref, v_ref):
    pc = pc_ref[...]
    nq = _rms(pc[:, 0:MLA_Q_RANK], qn_ref[...]).astype(BF16)
    nkv = _rms(pc[:, MLA_Q_RANK:MLA_Q_RANK + MLA_KV_RANK], kn_ref[...]).astype(BF16)
    kr = (pc[:, MLA_Q_RANK + MLA_KV_RANK:MLA_C_PAD] * ck_ref[...]).astype(BF16)
    qa = _dot(nq, wq_ref[...])
    qb = _dot(nq, wqr_ref[...])
    cq = cq_ref[...]
    sq = sq_ref[...]
    scale = (MLA_NOPE + MLA_ROPE) ** -0.5
    for h in range(MLA_HEADS):
        hs = slice(h * MLA_HEAD_PAD, (h + 1) * MLA_HEAD_PAD)
        q_ref[:, hs] = ((qa[:, hs] * cq + qb[:, hs] * sq) * scale).astype(q_ref.dtype)
    k_ref[...] = (_dot(nkv, wk_ref[...]) + _dot(kr, place_ref[...])).astype(k_ref.dtype)
    v_ref[...] = _dot(nkv, wv_ref[...]).astype(v_ref.dtype)


def _mla_proj(p_c, cq, sq, ck, qn, kn, wq, wqr, wk, wv, place, tm):
    t = p_c.shape[0]
    row = lambda n: pl.BlockSpec((tm, n), lambda i: (i, 0))
    full = lambda a: pl.BlockSpec(a.shape, lambda i: (0, 0))
    consts = (qn, kn, wq, wqr, wk, wv, place)
    hp = MLA_HEADS * MLA_HEAD_PAD
    return pl.pallas_call(
        _mla_proj_kernel,
        grid=(t // tm,),
        in_specs=[row(MLA_C_PAD), row(MLA_HEAD_PAD), row(MLA_HEAD_PAD), row(MLA_HEAD_PAD)] + [full(a) for a in consts],
        out_specs=[row(hp), row(hp), row(MLA_DIM)],
        out_shape=[jax.ShapeDtypeStruct((t, hp), BF16), jax.ShapeDtypeStruct((t, hp), BF16),
                   jax.ShapeDtypeStruct((t, MLA_DIM), BF16)],
        compiler_params=_cparams(("parallel",)),
        name="mla_proj",
    )(p_c, cq, sq, ck, *consts)


_NEG = -1e30


def _flash_kernel(q_ref, k_ref, v_ref, o_ref, m_ref, l_ref, acc_ref, *, tq, tk):
    qi = pl.program_id(1)
    ki = pl.program_id(2)
    nk = pl.num_programs(2)

    @pl.when(ki == 0)
    def _():
        m_ref[...] = jnp.full_like(m_ref, _NEG)
        l_ref[...] = jnp.zeros_like(l_ref)
        acc_ref[...] = jnp.zeros_like(acc_ref)

    def step(masked):
        if masked:
            rows = lax.broadcasted_iota(jnp.int32, (tq, tk), 0)
            cols = lax.broadcasted_iota(jnp.int32, (tq, tk), 1)
            keep = cols <= rows
        lane = lax.broadcasted_iota(jnp.int32, (tq, 2 * MLA_V), 1)
        low = lane < MLA_V
        for pair in range(MLA_HEADS // 2):
            ps = slice(pair * 2 * MLA_V, (pair + 1) * 2 * MLA_V)
            v_pair = v_ref[0, :, ps]
            pv = []
            alphas = []
            for sub in range(2):
                h = 2 * pair + sub
                hs = slice(h * MLA_HEAD_PAD, (h + 1) * MLA_HEAD_PAD)
                s = _dot_nt(q_ref[0, :, hs], k_ref[0, :, hs])
                if masked:
                    s = jnp.where(keep, s, _NEG)
                m_old = m_ref[h]
                m_new = jnp.maximum(m_old, jnp.max(s, axis=1, keepdims=True))
                alpha = jnp.exp(m_old - m_new)
                p = jnp.exp(s - m_new[:, 0:1])
                l_ref[h] = alpha * l_ref[h] + jnp.sum(p, axis=1, keepdims=True)
                m_ref[h] = m_new
                pv.append(_dot(p.astype(BF16), v_pair))
                alphas.append(alpha)
            acc_ref[:, ps] = acc_ref[:, ps] * jnp.where(low, alphas[0], alphas[1]) + jnp.where(low, pv[0], pv[1])

    @pl.when(ki < qi)
    def _():
        step(False)

    @pl.when(ki == qi)
    def _():
        step(True)

    @pl.when(ki == nk - 1)
    def _():
        lane = lax.broadcasted_iota(jnp.int32, (tq, 2 * MLA_V), 1)
        low = lane < MLA_V
        for pair in range(MLA_HEADS // 2):
            ps = slice(pair * 2 * MLA_V, (pair + 1) * 2 * MLA_V)
            l_pair = jnp.where(low, l_ref[2 * pair], l_ref[2 * pair + 1])
            o_ref[0, :, ps] = (acc_ref[:, ps] / l_pair).astype(o_ref.dtype)


def _flash(q, k, v, tq):
    bsz, s, hp = q.shape
    tk = tq
    nq = s // tq
    qspec = pl.BlockSpec((1, tq, hp), lambda b, i, j: (b, i, 0))
    kspec = pl.BlockSpec((1, tk, hp), lambda b, i, j: (b, jnp.minimum(i, j), 0))
    vspec = pl.BlockSpec((1, tk, MLA_DIM), lambda b, i, j: (b, jnp.minimum(i, j), 0))
    ospec = pl.BlockSpec((1, tq, MLA_DIM), lambda b, i, j: (b, i, 0))
    return pl.pallas_call(
        functools.partial(_flash_kernel, tq=tq, tk=tk),
        grid=(bsz, nq, nq),
        in_specs=[qspec, kspec, vspec],
        out_specs=ospec,
        out_shape=jax.ShapeDtypeStruct((bsz, s, MLA_DIM), BF16),
        scratch_shapes=[pltpu.VMEM((MLA_HEADS, tq, 128), F32), pltpu.VMEM((MLA_HEADS, tq, 128), F32),
                        pltpu.VMEM((tq, MLA_DIM), F32)],
        compiler_params=_cparams(("parallel", "parallel", "arbitrary")),
        name="mla_flash",
    )(q, k, v)


def _ffn_kernel(h_ref, ya_ref, yb_ref, yc_ref, woa_ref, wob_ref, woc_ref, fg_ref, wg_ref, wu_ref, wd_ref, *rest,
                final):
    if final:
        fin_ref, o_ref, h1_ref, hn_ref, acc_ref = rest
    else:
        o_ref, h1_ref, hn_ref, acc_ref = rest
    f = pl.program_id(1)

    @pl.when(f == 0)
    def _():
        h1 = (h_ref[...] + _dot(ya_ref[...], woa_ref[...]) + _dot(yb_ref[...], wob_ref[...])
              + _dot(yc_ref[...], woc_ref[...]))
        h1_ref[...] = h1
        hn_ref[...] = _rms(h1, fg_ref[...]).astype(BF16)
        acc_ref[...] = jnp.zeros_like(acc_ref)

    hn = hn_ref[...]
    gate = _dot(hn, wg_ref[...])
    up = _dot(hn, wu_ref[...])
    act = (gate * jax.nn.sigmoid(gate) * up).astype(BF16)
    acc_ref[...] += _dot(act, wd_ref[...])

    @pl.when(f == pl.num_programs(1) - 1)
    def _():
        out = h1_ref[...] + acc_ref[...]
        if final:
            out = _rms(out, fin_ref[...])
        o_ref[...] = out


def _ffn(h, ya, yb, yc, woa, wob, woc, fg, wgu, wd, fin, tm, tf):
    t, d = h.shape
    dff = wd.shape[0]
    nf = dff // tf
    row = lambda n: pl.BlockSpec((tm, n), lambda i, f: (i, 0))
    full = lambda a: pl.BlockSpec(a.shape, lambda i, f: (0, 0))
    final = fin is not None
    in_specs = [row(d), row(ya.shape[1]), row(yb.shape[1]), row(yc.shape[1]), full(woa), full(wob), full(woc),
                full(fg),
                pl.BlockSpec((d, tf), lambda i, f: (0, f)),
                pl.BlockSpec((d, tf), lambda i, f: (0, f + nf)),
                pl.BlockSpec((tf, d), lambda i, f: (f, 0))]
    args = [h, ya, yb, yc, woa, wob, woc, fg, wgu, wgu, wd]
    if final:
        in_specs.append(full(fin))
        args.append(fin)
    return pl.pallas_call(
        functools.partial(_ffn_kernel, final=final),
        grid=(t // tm, nf),
        in_specs=in_specs,
        out_specs=row(d),
        out_shape=jax.ShapeDtypeStruct((t, d), F32),
        scratch_shapes=[pltpu.VMEM((tm, d), F32), pltpu.VMEM((tm, d), BF16), pltpu.VMEM((tm, d), F32)],
        compiler_params=_cparams(("parallel", "arbitrary")),
        name="outproj_ffn",
    )(*args)


def _rot_half_cols(w):
    n = w.shape[1]
    return jnp.concatenate([-w[:, n // 2:], w[:, :n // 2]], axis=1)


def _ret_perm():
    idx = []
    for half in range(2):
        for h in range(RET_HEADS):
            base = h * RET_HEAD_DIM + half * (RET_HEAD_DIM // 2)
            idx.extend(range(base, base + RET_HEAD_DIM // 2))
    return np.asarray(idx)


def _prep_in_weights(w_in):
    wa = w_in[:, :RW_IN]
    wb = w_in[:, RW_IN:RW_IN + RET_IN]
    wc = w_in[:, RW_IN + RET_IN:]
    perm = _ret_perm()
    wb = jnp.concatenate([wb[:, 0:RET_DIM][:, perm], wb[:, RET_DIM:2 * RET_DIM][:, perm], wb[:, 2 * RET_DIM:]], axis=1)
    kr = wc[:, MLA_Q_RANK + MLA_KV_RANK:]
    pad = jnp.zeros((w_in.shape[0], MLA_C_PAD - MLA_IN - MLA_ROPE), w_in.dtype)
    wc = jnp.concatenate([wc, _rot_half_cols(kr), pad], axis=1)
    return wa.astype(BF16), wb.astype(BF16), wc.astype(BF16)


def _prep_mla_weights(w_q_up, w_kv_up):
    qd = MLA_NOPE + MLA_ROPE
    zq = jnp.zeros((MLA_Q_RANK, MLA_HEAD_PAD - qd), w_q_up.dtype)
    zr = jnp.zeros((MLA_Q_RANK, MLA_NOPE), w_q_up.dtype)
    zk = jnp.zeros((MLA_KV_RANK, MLA_HEAD_PAD - MLA_NOPE), w_kv_up.dtype)
    wq, wqr, wk, wv = [], [], [], []
    for h in range(MLA_HEADS):
        qh = w_q_up[:, h * qd:(h + 1) * qd]
        wq += [qh, zq]
        wqr += [zr, _rot_half_cols(qh[:, MLA_NOPE:]), zq]
        kvh = w_kv_up[:, h * (MLA_NOPE + MLA_V):(h + 1) * (MLA_NOPE + MLA_V)]
        wk += [kvh[:, :MLA_NOPE], zk]
        wv.append(kvh[:, MLA_NOPE:])
    cat = lambda xs: jnp.concatenate(xs, axis=1).astype(BF16)
    place = np.zeros((MLA_HEAD_PAD, MLA_HEADS * MLA_HEAD_PAD), np.float32)
    for h in range(MLA_HEADS):
        for j in range(MLA_ROPE):
            place[j, h * MLA_HEAD_PAD + MLA_NOPE + j] = 1.0
            place[MLA_ROPE + j, h * MLA_HEAD_PAD + MLA_NOPE + j] = 1.0
    return cat(wq), cat(wqr), cat(wk), cat(wv), jnp.asarray(place, BF16)


def _pad_rows(w, start, total):
    out = jnp.zeros((total, w.shape[1]), w.dtype)
    return out.at[start:start + w.shape[0]].set(w)


def _rope_tables(positions):
    pos = positions.astype(F32)[..., None]
    inv_ret = ROPE_BASE ** (-jnp.arange(0, RET_HEAD_DIM, 2, dtype=F32) / RET_HEAD_DIM)
    ang = pos * inv_ret
    ret_cos = jnp.tile(jnp.cos(ang), (1, 1, RET_HEADS))
    ret_sin = jnp.tile(jnp.sin(ang), (1, 1, RET_HEADS))
    inv_mla = ROPE_BASE ** (-jnp.arange(0, MLA_ROPE, 2, dtype=F32) / MLA_ROPE)
    ang = pos * inv_mla
    cos, sin = jnp.cos(ang), jnp.sin(ang)
    shp = pos.shape[:2]
    ones = jnp.ones(shp + (MLA_NOPE,), F32)
    z = lambda n: jnp.zeros(shp + (n,), F32)
    cq = jnp.concatenate([ones, cos, cos, z(MLA_HEAD_PAD - MLA_NOPE - MLA_ROPE)], axis=-1)
    sq = jnp.concatenate([z(MLA_NOPE), sin, sin, z(MLA_HEAD_PAD - MLA_NOPE - MLA_ROPE)], axis=-1)
    ck = jnp.concatenate([cos, cos, sin, sin, z(MLA_HEAD_PAD - 2 * MLA_ROPE)], axis=-1)
    flat = lambda a: a.reshape(-1, a.shape[-1])
    return ret_cos, ret_sin, flat(cq), flat(sq), flat(ck)


def kernel(x, positions, attn_norm, w_in, w_out, rw_mu, rw_w0, rw_w2, rw_a0, rw_a2, rw_g2, rw_k_k, rw_k_a, rw_r_k, rw_gn_w, rw_gn_b, rw_v0, rw_v1, rw_v2, mla_q_norm, mla_kv_norm, mla_w_q_up, mla_w_kv_up, ffn_norm, w_gate_up, w_down, final_norm):
    bsz, s, d = x.shape
    depth = w_in.shape[0]
    t = bsz * s
    tm = min(512, t)
    tf = w_down.shape[1] // 2
    ret_cos, ret_sin, cq, sq, ck = _rope_tables(positions)
    h = x.reshape(t, d)
    v_first = None
    for l in range(depth):
        wa, wb, wc = _prep_in_weights(w_in[l])
        p_a, p_b, p_c = _inproj(h, attn_norm[l][None, :], wa, wb, wc, tm)

        vec = jnp.zeros((_VEC_ROWS, RW_DIM), F32)
        rows = [rw_w0[l], rw_a0[l], rw_k_k[l], rw_k_a[l], rw_r_k[l].reshape(-1), rw_gn_w[l], rw_gn_b[l]]
        if l > 0:
            rows.append(rw_v0[l - 1])
        vec = vec.at[:len(rows)].set(jnp.stack(rows))
        lo = 3 * RW_DIM
        w2p = _pad_rows(rw_w2[l], 0, RW_LOWRANK).astype(BF16)
        a2p = _pad_rows(rw_a2[l], RW_DECAY_RANK, RW_LOWRANK).astype(BF16)
        g2p = _pad_rows(rw_g2[l], RW_DECAY_RANK + RW_A_RANK, RW_LOWRANK).astype(BF16)
        del lo
        if l > 0:
            v1p = jnp.pad(rw_v1[l - 1], ((0, 0), (0, 128 - RW_V_RANK))).astype(BF16)
            v2p = _pad_rows(rw_v2[l - 1], 0, 128).astype(BF16)
        else:
            v1p = v2p = None
        nb = min(4, bsz)
        tb = min(256, s)
        res = _rwkv(p_a.reshape(bsz, s, RW_IN), v_first, rw_mu[l][None, :], vec, w2p, a2p, g2p, v1p, v2p, nb, tb)
        if l == 0:
            y_a, v_first = res
        else:
            y_a = res

        y_b = _retention(p_b.reshape(bsz, s, RET_IN), ret_cos, ret_sin)

        wq, wqr, wk, wv, place = _prep_mla_weights(mla_w_q_up[l], mla_w_kv_up[l])
        q, k, v = _mla_proj(p_c, cq, sq, ck, mla_q_norm[l][None, :], mla_kv_norm[l][None, :], wq, wqr, wk, wv, place, tm)
        hp = MLA_HEADS * MLA_HEAD_PAD
        y_c = _flash(q.reshape(bsz, s, hp), k.reshape(bsz, s, hp), v.reshape(bsz, s, MLA_DIM), min(512, s))

        wo = w_out[l].astype(BF16)
        fin = final_norm[None, :] if l == depth - 1 else None
        h = _ffn(h, y_a.reshape(t, RW_DIM), y_b.reshape(t, RET_DIM), y_c.reshape(t, MLA_DIM),
                 wo[:RW_DIM], wo[RW_DIM:RW_DIM + RET_DIM], wo[RW_DIM + RET_DIM:], ffn_norm[l][None, :],
                 w_gate_up[l].astype(BF16), w_down[l].astype(BF16), fin, tm, tf)
    return h.reshape(bsz, s, d)
```

```python
import functools
import math

import jax
import jax.numpy as jnp
import numpy as np
from jax import lax
from jax.experimental import pallas as pl
from jax.experimental.pallas import tpu as pltpu

F32 = jnp.float32
BF16 = jnp.bfloat16

NORM_EPS = 1e-6
ROPE_BASE = 10000.0
LANES = 128

RW_HEADS = 4
RW_HEAD_DIM = 64
RW_DIM = RW_HEADS * RW_HEAD_DIM
RW_DECAY_RANK = 32
RW_A_RANK = 32
RW_V_RANK = 32
RW_GATE_RANK = 64
RW_GN_EPS = 64e-5
RW_IN = 3 * RW_DIM + RW_DECAY_RANK + RW_A_RANK + RW_GATE_RANK
RW_LOWRANK = RW_DECAY_RANK + RW_A_RANK + RW_GATE_RANK
RW_CHUNK = 64

RET_HEADS = 4
RET_HEAD_DIM = 64
RET_DIM = RET_HEADS * RET_HEAD_DIM
RET_CHUNK = 128
RET_IN = 4 * RET_DIM
RET_FREQS = RET_HEAD_DIM // 2

MLA_HEADS = 8
MLA_NOPE = 64
MLA_ROPE = 32
MLA_V = 64
MLA_Q_RANK = 384
MLA_KV_RANK = 256
MLA_DIM = MLA_HEADS * MLA_V
MLA_IN = MLA_Q_RANK + MLA_KV_RANK + MLA_ROPE
MLA_HEAD_PAD = 128
MLA_C_PAD = 768
MLA_FREQS = MLA_ROPE // 2

TAB_RET_COS = 0
TAB_RET_SIN = RET_FREQS
TAB_MLA_COS = 2 * RET_FREQS
TAB_MLA_SIN = 2 * RET_FREQS + MLA_FREQS

V7X_VMEM_LIMIT_BYTES = 56 * 1024 * 1024


def _cparams(semantics):
    return pltpu.CompilerParams(dimension_semantics=semantics, vmem_limit_bytes=V7X_VMEM_LIMIT_BYTES)


def _dot(a, b):
    return jnp.dot(a, b, preferred_element_type=F32)


def _dot_nt(a, b):
    return lax.dot_general(a, b, (((1,), (1,)), ((), ())), preferred_element_type=F32)


def _dot_tn(a, b):
    return lax.dot_general(a, b, (((0,), (0,)), ((), ())), preferred_element_type=F32)


def _split_dot(x, m_bf16, passes):
    acc = None
    rem = x
    for _ in range(passes):
        piece = rem.astype(BF16)
        part = _dot(piece, m_bf16)
        acc = part if acc is None else acc + part
        rem = rem - piece.astype(F32)
    return acc


def _split_dot_left(m_bf16, x):
    acc = None
    rem = x
    for _ in range(3):
        piece = rem.astype(BF16)
        part = _dot(m_bf16, piece)
        acc = part if acc is None else acc + part
        rem = rem - piece.astype(F32)
    return acc


def _rms(x, g):
    return x * lax.rsqrt(jnp.mean(x * x, axis=-1, keepdims=True) + NORM_EPS) * g


def _layer_spec(shape, layer, ngrid):
    zeros = (0,) * len(shape)
    if ngrid == 1:
        return pl.BlockSpec((None,) + tuple(shape), lambda i: (layer,) + zeros)
    return pl.BlockSpec((None,) + tuple(shape), lambda i, j: (layer,) + zeros)


def _const_spec(a, ngrid):
    zeros = (0,) * a.ndim
    if ngrid == 1:
        return pl.BlockSpec(a.shape, lambda i: zeros)
    return pl.BlockSpec(a.shape, lambda i, j: zeros)


def _inproj_kernel(x_ref, g_ref, wa_ref, wb_ref, wc_ref, pa_ref, pb_ref, pc_ref):
    hn = _rms(x_ref[...], g_ref[...]).astype(BF16)
    pa_ref[...] = _dot(hn, wa_ref[...])
    pb_ref[...] = _dot(hn, wb_ref[...])
    pc_ref[...] = _dot(hn, wc_ref[...])


def _inproj(x, g, wa, wb, wc, layer, tm):
    t, d = x.shape
    row = lambda n: pl.BlockSpec((tm, n), lambda i: (i, 0))
    ws = (wa, wb, wc)
    return pl.pallas_call(
        _inproj_kernel,
        grid=(t // tm,),
        in_specs=[row(d), _layer_spec(g.shape[1:], layer, 1)] + [_layer_spec(w.shape[1:], layer, 1) for w in ws],
        out_specs=[row(w.shape[2]) for w in ws],
        out_shape=[jax.ShapeDtypeStruct((t, w.shape[2]), F32) for w in ws],
        compiler_params=_cparams(("parallel",)),
        name="inproj",
    )(x, g, wa, wb, wc)


_VEC_ROWS = 16
(_V_W0, _V_A0, _V_KK, _V_KA, _V_RK, _V_GNW, _V_GNB, _V_V0) = range(8)


def _rwkv_kernel(*refs, has_vres, nb, tb):
    if has_vres:
        (p_ref, vf_ref, mu_ref, vec_ref, w2_ref, a2_ref, g2_ref, v1_ref, v2_ref, bd_ref, tri_ref,
         o_ref, carry_ref, s_ref, r_s, k_s, v_s, lw_s, kk_s, ka_s, g_s, bon_s, y_s) = refs
    else:
        (p_ref, mu_ref, vec_ref, w2_ref, a2_ref, g2_ref, bd_ref, tri_ref,
         o_ref, vfo_ref, carry_ref, s_ref, r_s, k_s, v_s, lw_s, kk_s, ka_s, g_s, bon_s, y_s) = refs

    L = RW_CHUNK
    C = RW_DIM
    tstep = pl.program_id(1)

    @pl.when(tstep == 0)
    def _():
        carry_ref[...] = jnp.zeros_like(carry_ref)
        s_ref[...] = jnp.zeros_like(s_ref)

    vec = vec_ref[...]
    row = lambda i: vec[i:i + 1, :]
    bd = bd_ref[...]
    mu = mu_ref[...]

    def bdsum(x):
        return _split_dot(x, bd, 2)

    def prep(b, _):
        p = p_ref[b]
        shifted = pltpu.roll(p, 1, 0)
        first = lax.broadcasted_iota(jnp.int32, p.shape, 0) == 0
        p_prev = jnp.where(first, carry_ref[b], shifted)
        carry_ref[b] = p[tb - 1:tb, :]
        ps = p + (p_prev - p) * mu
        r = ps[:, 0:C]
        k = ps[:, C:2 * C]
        v = ps[:, 2 * C:3 * C]
        lr = ps[:, 3 * C:3 * C + RW_LOWRANK]
        z = row(_V_W0) + _dot(jnp.tanh(lr).astype(BF16), w2_ref[...])
        lw = -math.exp(-0.5) * jax.nn.sigmoid(z)
        a = jax.nn.sigmoid(row(_V_A0) + _dot(lr.astype(BF16), a2_ref[...]))
        g = _dot(jax.nn.sigmoid(lr).astype(BF16), g2_ref[...])
        if has_vres:
            lat = _dot(v.astype(BF16), v1_ref[...]).astype(BF16)
            v = v + (vf_ref[b] - v) * jax.nn.sigmoid(row(_V_V0) + _dot(lat, v2_ref[...]))
        else:
            vfo_ref[b] = v
        kk = k * row(_V_KK)
        kk = kk * lax.rsqrt(jnp.maximum(bdsum(kk * kk), 1e-24))
        k2 = k * (1.0 + (a - 1.0) * row(_V_KA))
        r_s[b] = r
        k_s[b] = k2
        v_s[b] = v
        lw_s[b] = lw
        kk_s[b] = kk
        ka_s[b] = kk * a
        g_s[b] = g
        bon_s[b] = bdsum(r * k2 * row(_V_RK)) * v
        return 0

    lax.fori_loop(0, nb, prep, 0)

    hl = RW_HEADS * L
    r_i = lax.broadcasted_iota(jnp.int32, (hl, C), 0)
    c_i = lax.broadcasted_iota(jnp.int32, (hl, C), 1)
    headmask = (r_i // L) == (c_i // RW_HEAD_DIM)
    t_i = lax.broadcasted_iota(jnp.int32, (L, hl), 0)
    s_i = lax.broadcasted_iota(jnp.int32, (L, hl), 1) % L
    strict = t_i > s_i
    incl = t_i >= s_i
    tri = tri_ref[...]

    def masked4(x):
        return jnp.where(headmask, jnp.concatenate([x] * RW_HEADS, axis=0), 0.0).astype(BF16)

    def chunk(j, _):
        sl = pl.ds(pl.multiple_of(j * L, L), L)
        bs = range(nb)
        r = [r_s[b, sl, :] for b in bs]
        k2 = [k_s[b, sl, :] for b in bs]
        v = [v_s[b, sl, :] for b in bs]
        lw = [lw_s[b, sl, :] for b in bs]
        kk = [kk_s[b, sl, :] for b in bs]
        ka = [ka_s[b, sl, :] for b in bs]
        c = [_split_dot_left(tri, lw[b]) for b in bs]
        c_last = [c[b][L - 1:L, :] for b in bs]
        sig = [0.5 * c_last[b] for b in bs]
        e_neg = [jnp.exp(sig[b] - c[b]) for b in bs]
        kt = [k2[b] * e_neg[b] for b in bs]
        bt = [ka[b] * e_neg[b] for b in bs]
        x = [jnp.concatenate([-(kk[b] * jnp.exp(c[b] - lw[b] - sig[b])), r[b] * jnp.exp(c[b] - sig[b])],
                             axis=0).astype(BF16) for b in bs]
        kb = [jnp.concatenate([masked4(kt[b]), masked4(bt[b])], axis=0) for b in bs]
        ab = [_dot_nt(x[b], kb[b]) for b in bs]
        s_old = [s_ref[b] for b in bs]
        xs = [_dot_nt(x[b], (s_old[b] * jnp.exp(sig[b])).astype(BF16)) for b in bs]
        a_k = [jnp.concatenate([jnp.where(strict, ab[b][0:L, 0:hl], 0.0), jnp.where(incl, ab[b][L:2 * L, 0:hl], 0.0)],
                               axis=0).astype(BF16) for b in bs]
        n_w = [jnp.where(strict, ab[b][0:L, hl:2 * hl], 0.0) for b in bs]
        a_rb = [jnp.where(incl, ab[b][L:2 * L, hl:2 * hl], 0.0).astype(BF16) for b in bs]
        av = [_dot(a_k[b], masked4(v[b])) for b in bs]
        u = [av[b][0:L] + xs[b][0:L] for b in bs]
        for i in range(6):
            n_b = [n_w[b].astype(BF16) for b in bs]
            u = [u[b] + _dot(n_b[b], masked4(u[b])) for b in bs]
            if i < 5:
                n_w = [_dot(n_b[b], masked4(n_w[b])) for b in bs]
        y = [av[b][L:2 * L] + xs[b][L:2 * L] + _dot(a_rb[b], masked4(u[b])) for b in bs]
        for b in bs:
            y_s[b, sl, :] = y[b]
        scale = [jnp.exp(c_last[b] - sig[b]) for b in bs]
        kbs = [jnp.concatenate([kt[b] * scale[b], bt[b] * scale[b]], axis=0).astype(BF16) for b in bs]
        vu = [jnp.concatenate([v[b], u[b]], axis=0).astype(BF16) for b in bs]
        upd = [_dot_tn(vu[b], kbs[b]) for b in bs]
        for b in bs:
            s_ref[b] = s_old[b] * jnp.exp(c_last[b]) + jnp.where(headmask, upd[b], 0.0)
        return 0

    lax.fori_loop(0, tb // L, chunk, 0)

    def post(b, _):
        y = y_s[b]
        mean = bdsum(y) * (1.0 / RW_HEAD_DIM)
        d = y - mean
        var = bdsum(d * d) * (1.0 / RW_HEAD_DIM)
        yn = d * lax.rsqrt(var + RW_GN_EPS) * row(_V_GNW) + row(_V_GNB)
        o_ref[b] = ((yn + bon_s[b]) * g_s[b]).astype(o_ref.dtype)
        return 0

    lax.fori_loop(0, nb, post, 0)


def _block_ones(n, blk):
    i = np.arange(n)
    return (i[:, None] // blk == i[None, :] // blk).astype(np.float32)


def _rwkv(p_a, v_first, mu, vec, w2p, a2p, g2p, v1p, v2p, layer, nb, tb):
    bsz, s, _ = p_a.shape
    has_vres = v_first is not None
    bd = jnp.asarray(_block_ones(RW_DIM, RW_HEAD_DIM), BF16)
    tri = jnp.asarray(np.tril(np.ones((RW_CHUNK, RW_CHUNK), np.float32)), BF16)
    blk = lambda n: pl.BlockSpec((nb, tb, n), lambda i, j: (i, j, 0))
    lay = lambda a, l: _layer_spec(a.shape[1:], l, 2)
    common = [lay(mu, layer), lay(vec, layer), lay(w2p, layer), lay(a2p, layer), lay(g2p, layer)]
    consts = [_const_spec(bd, 2), _const_spec(tri, 2)]
    if has_vres:
        args = (p_a, v_first, mu, vec, w2p, a2p, g2p, v1p, v2p, bd, tri)
        in_specs = [blk(RW_IN), blk(RW_DIM)] + common + [lay(v1p, layer - 1), lay(v2p, layer - 1)] + consts
        out_specs = blk(RW_DIM)
        out_shape = jax.ShapeDtypeStruct((bsz, s, RW_DIM), BF16)
    else:
        args = (p_a, mu, vec, w2p, a2p, g2p, bd, tri)
        in_specs = [blk(RW_IN)] + common + consts
        out_specs = [blk(RW_DIM), blk(RW_DIM)]
        out_shape = [jax.ShapeDtypeStruct((bsz, s, RW_DIM), BF16), jax.ShapeDtypeStruct((bsz, s, RW_DIM), F32)]
    big = pltpu.VMEM((nb, tb, RW_DIM), F32)
    scratch = [pltpu.VMEM((nb, 1, RW_IN), F32), pltpu.VMEM((nb, RW_DIM, RW_DIM), F32)] + [big] * 9
    return pl.pallas_call(
        functools.partial(_rwkv_kernel, has_vres=has_vres, nb=nb, tb=tb),
        grid=(bsz // nb, s // tb),
        in_specs=in_specs,
        out_specs=out_specs,
        out_shape=out_shape,
        scratch_shapes=scratch,
        compiler_params=_cparams(("parallel", "arbitrary")),
        name="rwkv7",
    )(*args)


def _ret_tables():
    h, d, c = RET_HEADS, RET_HEAD_DIM, RET_CHUNK
    log_gamma = np.log(1.0 - 2.0 ** (-5.0 - np.arange(h, dtype=np.float64)))
    idx = np.arange(c, dtype=np.float64)
    diff = idx[:, None] - idx[None, :]
    dmask = np.where(diff >= 0, np.exp(log_gamma[:, None, None] * np.maximum(diff, 0.0)), 0.0)
    lane = np.arange(RET_DIM)
    head_qk = (lane % (RET_DIM // 2)) // (d // 2)
    head_v = lane // d
    qdec = np.exp(log_gamma[head_qk][None, :] * (idx[:, None] + 1.0))
    kdec = np.exp(log_gamma[head_qk][None, :] * (c - 1.0 - idx[:, None])) * d ** -0.5
    hm_qk = (np.arange(h)[:, None, None] == head_qk[None, None, :]) * np.ones((1, c, 1))
    hm_v = (np.arange(h)[:, None, None] == head_v[None, None, :]) * np.ones((1, c, 1))
    block = head_qk[:, None] == head_v[None, :]
    rdec = np.where(block, np.exp(log_gamma[head_qk] * c)[:, None], 0.0)
    pcos = np.zeros((LANES, RET_DIM // 2), np.float32)
    psin = np.zeros((LANES, RET_DIM // 2), np.float32)
    for hh in range(h):
        for j in range(RET_FREQS):
            pcos[TAB_RET_COS + j, hh * RET_FREQS + j] = 1.0
            psin[TAB_RET_SIN + j, hh * RET_FREQS + j] = 1.0
    f = lambda a: jnp.asarray(a, F32)
    return (f(dmask.reshape(h * c, c) * d ** -0.5), f(qdec), f(kdec), f(hm_qk.reshape(h * c, RET_DIM)),
            f(hm_v.reshape(h * c, RET_DIM)), f(block), f(rdec),
            jnp.asarray(np.concatenate([pcos, psin], axis=1), BF16))


def _ret_kernel(p_ref, tab_ref, dmask_ref, qdec_ref, kdec_ref, hmqk_ref, hmv_ref, block_ref, rdec_ref, pcs_ref,
                bd_ref, o_ref, r_ref, *, nb):
    c, dm = RET_CHUNK, RET_DIM
    half = dm // 2

    @pl.when(pl.program_id(0) == 0)
    def _():
        r_ref[...] = jnp.zeros_like(r_ref)

    bs = range(nb)
    cs = [_split_dot(tab_ref[b], pcs_ref[...], 3) for b in bs]
    p = [p_ref[b] for b in bs]

    def rope(x, t):
        cos, sin = t[:, :half], t[:, half:]
        x1, x2 = x[:, :half], x[:, half:]
        return jnp.concatenate([x1 * cos - x2 * sin, x2 * cos + x1 * sin], axis=1)

    q = [rope(p[b][:, 0:dm], cs[b]) for b in bs]
    k = [rope(p[b][:, dm:2 * dm], cs[b]) for b in bs]
    vb = [p[b][:, 2 * dm:3 * dm].astype(BF16) for b in bs]
    q_bd = [(jnp.concatenate([q[b]] * RET_HEADS, axis=0) * hmqk_ref[...]).astype(BF16) for b in bs]
    scores = [(_dot_nt(q_bd[b], k[b].astype(BF16)) * dmask_ref[...]).astype(BF16) for b in bs]
    o = [_dot(scores[b], vb[b]) * hmv_ref[...] for b in bs]
    r_old = [r_ref[b] for b in bs]
    cross = [_dot((q[b] * qdec_ref[...]).astype(BF16), r_old[b].astype(BF16)) for b in bs]
    upd = [_dot_tn((k[b] * kdec_ref[...]).astype(BF16), vb[b]) for b in bs]
    for b in bs:
        r_ref[b] = r_old[b] * rdec_ref[...] + upd[b] * block_ref[...]
    y = [o[b][0:c] + o[b][c:2 * c] + o[b][2 * c:3 * c] + o[b][3 * c:4 * c] + cross[b] for b in bs]
    ms = [_split_dot(y[b] * y[b], bd_ref[...], 2) * (1.0 / RET_HEAD_DIM) for b in bs]
    for b in bs:
        g = p[b][:, 3 * dm:4 * dm]
        o_ref[b] = (g * jax.nn.sigmoid(g) * y[b] * lax.rsqrt(ms[b] + NORM_EPS)).astype(o_ref.dtype)


def _retention(p_b, tab, nb):
    bsz, s, _ = p_b.shape
    consts = _ret_tables() + (jnp.asarray(_block_ones(RET_DIM, RET_HEAD_DIM), BF16),)
    blk = lambda n: pl.BlockSpec((nb, RET_CHUNK, n), lambda j, i: (i, j, 0))
    full = lambda a: pl.BlockSpec(a.shape, lambda j, i: (0,) * a.ndim)
    assert bsz == nb, "one batch group per time block keeps the state scratch simple"
    return pl.pallas_call(
        functools.partial(_ret_kernel, nb=nb),
        grid=(s // RET_CHUNK, bsz // nb),
        in_specs=[blk(RET_IN), blk(LANES)] + [full(a) for a in consts],
        out_specs=blk(RET_DIM),
        out_shape=jax.ShapeDtypeStruct((bsz, s, RET_DIM), BF16),
        scratch_shapes=[pltpu.VMEM((nb, RET_DIM, RET_DIM), F32)],
        compiler_params=_cparams(("arbitrary", "arbitrary")),
        name="retention",
    )(p_b, tab, *consts)


def _mla_tables():
    pm = np.zeros((LANES, 3 * MLA_HEAD_PAD), np.float32)
    for j in range(MLA_FREQS):
        for rep in range(2):
            pm[TAB_MLA_COS + j, MLA_NOPE + rep * MLA_FREQS + j] = 1.0
            pm[TAB_MLA_SIN + j, MLA_HEAD_PAD + MLA_NOPE + rep * MLA_FREQS + j] = 1.0
            pm[TAB_MLA_COS + j, 2 * MLA_HEAD_PAD + rep * MLA_FREQS + j] = 1.0
            pm[TAB_MLA_SIN + j, 2 * MLA_HEAD_PAD + MLA_ROPE + rep * MLA_FREQS + j] = 1.0
    ones = np.zeros((1, MLA_HEAD_PAD), np.float32)
    ones[0, :MLA_NOPE] = 1.0
    place = np.zeros((MLA_HEAD_PAD, MLA_HEADS * MLA_HEAD_PAD), np.float32)
    for h in range(MLA_HEADS):
        for j in range(MLA_ROPE):
            place[j, h * MLA_HEAD_PAD + MLA_NOPE + j] = 1.0
            place[MLA_ROPE + j, h * MLA_HEAD_PAD + MLA_NOPE + j] = 1.0
    return jnp.asarray(pm, BF16), jnp.asarray(ones, F32), jnp.asarray(place, BF16)


def _mla_proj_kernel(pc_ref, tab_ref, qn_ref, kn_ref, wq_ref, wqr_ref, wk_ref, wvt_ref, pm_ref, ones_ref, place_ref,
                     q_ref, k_ref, vt_ref):
    pc = pc_ref[...]
    tabs = _split_dot(tab_ref[...], pm_ref[...], 3)
    cq = tabs[:, 0:MLA_HEAD_PAD] + ones_ref[...]
    sq = tabs[:, MLA_HEAD_PAD:2 * MLA_HEAD_PAD]
    ck = tabs[:, 2 * MLA_HEAD_PAD:3 * MLA_HEAD_PAD]
    nq = _rms(pc[:, 0:MLA_Q_RANK], qn_ref[...]).astype(BF16)
    nkv = _rms(pc[:, MLA_Q_RANK:MLA_Q_RANK + MLA_KV_RANK], kn_ref[...]).astype(BF16)
    kr = (pc[:, MLA_Q_RANK + MLA_KV_RANK:MLA_C_PAD] * ck).astype(BF16)
    qa = _dot(nq, wq_ref[...])
    qb = _dot(nq, wqr_ref[...])
    scale = (MLA_NOPE + MLA_ROPE) ** -0.5 * math.log2(math.e)
    for h in range(MLA_HEADS):
        hs = slice(h * MLA_HEAD_PAD, (h + 1) * MLA_HEAD_PAD)
        q_ref[:, hs] = ((qa[:, hs] * cq + qb[:, hs] * sq) * scale).astype(q_ref.dtype)
    k_ref[...] = (_dot(nkv, wk_ref[...]) + _dot(kr, place_ref[...])).astype(k_ref.dtype)
    vt_ref[0] = _dot_nt(wvt_ref[...], nkv).astype(vt_ref.dtype)


def _mla_proj(p_c, tab, qn, kn, wq, wqr, wk, wvt, layer, bsz, tm):
    t = p_c.shape[0]
    s = t // bsz
    per = s // tm
    row = lambda n: pl.BlockSpec((tm, n), lambda i: (i, 0))
    pm, ones, place = _mla_tables()
    hp = MLA_HEADS * MLA_HEAD_PAD
    lay = lambda a: _layer_spec(a.shape[1:], layer, 1)
    return pl.pallas_call(
        _mla_proj_kernel,
        grid=(t // tm,),
        in_specs=[row(MLA_C_PAD), row(LANES), lay(qn), lay(kn), lay(wq), lay(wqr), lay(wk), lay(wvt),
                  _const_spec(pm, 1), _const_spec(ones, 1), _const_spec(place, 1)],
        out_specs=[row(hp), row(hp), pl.BlockSpec((1, MLA_DIM, tm), lambda i: (i // per, 0, i % per))],
        out_shape=[jax.ShapeDtypeStruct((t, hp), BF16), jax.ShapeDtypeStruct((t, hp), BF16),
                   jax.ShapeDtypeStruct((bsz, MLA_DIM, s), BF16)],
        compiler_params=_cparams(("parallel",)),
        name="mla_proj",
    )(p_c, tab, qn, kn, wq, wqr, wk, wvt, pm, ones, place)


_NEG = -1e30


def _flash_kernel(q_ref, k_ref, vt_ref, o_ref, m_ref, l_ref, acc_ref, *, tq, tk):
    qi = pl.program_id(1)
    ki = pl.program_id(2)
    nk = pl.num_programs(2)

    @pl.when(ki == 0)
    def _():
        m_ref[...] = jnp.full_like(m_ref, _NEG)
        l_ref[...] = jnp.zeros_like(l_ref)
        acc_ref[...] = jnp.zeros_like(acc_ref)

    def step(masked):
        if masked:
            kv_i = lax.broadcasted_iota(jnp.int32, (tk, tq), 0)
            q_i = lax.broadcasted_iota(jnp.int32, (tk, tq), 1)
            keep = kv_i <= q_i
        def scores(h):
            hs = slice(h * MLA_HEAD_PAD, (h + 1) * MLA_HEAD_PAD)
            return _dot_nt(k_ref[0, :, hs], q_ref[0, :, hs])

        st_next = scores(0)
        for h in range(MLA_HEADS):
            vs = slice(h * MLA_V, (h + 1) * MLA_V)
            st = st_next
            if h + 1 < MLA_HEADS:
                st_next = scores(h + 1)
            if masked:
                st = jnp.where(keep, st, _NEG)
            m_old = m_ref[h:h + 1, :]
            m_new = jnp.maximum(m_old, jnp.max(st, axis=0, keepdims=True))
            alpha = jnp.exp2(m_old - m_new)
            p = jnp.exp2(st - m_new)
            l_ref[h:h + 1, :] = alpha * l_ref[h:h + 1, :] + jnp.sum(p, axis=0, keepdims=True)
            m_ref[h:h + 1, :] = m_new
            acc_ref[vs, :] = acc_ref[vs, :] * alpha + _dot(vt_ref[0, vs, :], p.astype(BF16))

    @pl.when(ki < qi)
    def _():
        step(False)

    @pl.when(ki == qi)
    def _():
        step(True)

    @pl.when(ki == nk - 1)
    def _():
        for h in range(MLA_HEADS):
            vs = slice(h * MLA_V, (h + 1) * MLA_V)
            o_ref[0, vs, :] = (acc_ref[vs, :] / l_ref[h:h + 1, :]).astype(o_ref.dtype)


def _flash(q, k, vt, tq):
    bsz, s, hp = q.shape
    tk = tq
    nq = s // tq
    qspec = pl.BlockSpec((1, tq, hp), lambda b, i, j: (b, i, 0))
    kspec = pl.BlockSpec((1, tk, hp), lambda b, i, j: (b, jnp.minimum(i, j), 0))
    vspec = pl.BlockSpec((1, MLA_DIM, tk), lambda b, i, j: (b, 0, jnp.minimum(i, j)))
    ospec = pl.BlockSpec((1, MLA_DIM, tq), lambda b, i, j: (b, 0, i))
    return pl.pallas_call(
        functools.partial(_flash_kernel, tq=tq, tk=tk),
        grid=(bsz, nq, nq),
        in_specs=[qspec, kspec, vspec],
        out_specs=ospec,
        out_shape=jax.ShapeDtypeStruct((bsz, MLA_DIM, s), BF16),
        scratch_shapes=[pltpu.VMEM((MLA_HEADS, tq), F32), pltpu.VMEM((MLA_HEADS, tq), F32),
                        pltpu.VMEM((MLA_DIM, tq), F32)],
        compiler_params=_cparams(("parallel", "parallel", "arbitrary")),
        name="mla_flash",
    )(q, k, vt)


def _ffn_kernel(h_ref, ya_ref, yb_ref, yct_ref, woa_ref, wob_ref, woc_ref, fg_ref, wg_ref, wu_ref, wd_ref, *rest,
                final):
    if final:
        fin_ref, o_ref, h1_ref, hn_ref, acc_ref = rest
    else:
        o_ref, h1_ref, hn_ref, acc_ref = rest
    f = pl.program_id(1)

    @pl.when(f == 0)
    def _():
        h1 = (h_ref[...] + _dot(ya_ref[...], woa_ref[...]) + _dot(yb_ref[...], wob_ref[...])
              + _dot_tn(yct_ref[0], woc_ref[...]))
        h1_ref[...] = h1
        hn_ref[...] = _rms(h1, fg_ref[...]).astype(BF16)
        acc_ref[...] = jnp.zeros_like(acc_ref)

    hn = hn_ref[...]
    gate = _dot(hn, wg_ref[...])
    up = _dot(hn, wu_ref[...])
    act = (gate * jax.nn.sigmoid(gate) * up).astype(BF16)
    acc_ref[...] += _dot(act, wd_ref[...])

    @pl.when(f == pl.num_programs(1) - 1)
    def _():
        out = h1_ref[...] + acc_ref[...]
        if final:
            out = _rms(out, fin_ref[...])
        o_ref[...] = out


def _ffn(h, ya, yb, yct, wo, fg, wgu, wd, fin, layer, tm, tf):
    t, d = h.shape
    dff = wd.shape[1]
    nf = dff // tf
    per = yct.shape[2] // tm
    row = lambda n: pl.BlockSpec((tm, n), lambda i, f: (i, 0))
    final = fin is not None
    in_specs = [row(d), row(RW_DIM), row(RET_DIM),
                pl.BlockSpec((1, MLA_DIM, tm), lambda i, f: (i // per, 0, i % per)),
                pl.BlockSpec((None, RW_DIM, d), lambda i, f: (layer, 0, 0)),
                pl.BlockSpec((None, RET_DIM, d), lambda i, f: (layer, 1, 0)),
                pl.BlockSpec((None, MLA_DIM, d), lambda i, f: (layer, 1, 0)),
                _layer_spec(fg.shape[1:], layer, 2),
                pl.BlockSpec((None, d, tf), lambda i, f: (layer, 0, f)),
                pl.BlockSpec((None, d, tf), lambda i, f: (layer, 0, f + nf)),
                pl.BlockSpec((None, tf, d), lambda i, f: (layer, f, 0))]
    args = [h, ya, yb, yct, wo, wo, wo, fg, wgu, wgu, wd]
    if final:
        in_specs.append(_const_spec(fin, 2))
        args.append(fin)
    return pl.pallas_call(
        functools.partial(_ffn_kernel, final=final),
        grid=(t // tm, nf),
        in_specs=in_specs,
        out_specs=row(d),
        out_shape=jax.ShapeDtypeStruct((t, d), F32),
        scratch_shapes=[pltpu.VMEM((tm, d), F32), pltpu.VMEM((tm, d), BF16), pltpu.VMEM((tm, d), F32)],
        compiler_params=_cparams(("parallel", "arbitrary")),
        name="outproj_ffn",
    )(*args)


def _rot_half(w):
    n = w.shape[-1]
    return jnp.concatenate([-w[..., n // 2:], w[..., :n // 2]], axis=-1)


def _ret_split_halves(w):
    lead = w.shape[:-1]
    w4 = w.reshape(lead + (RET_HEADS, 2, RET_FREQS))
    return jnp.swapaxes(w4, -3, -2).reshape(lead + (RET_DIM,))


def _prep_in_weights(w_in):
    wa = w_in[..., :RW_IN]
    wb = w_in[..., RW_IN:RW_IN + RET_IN]
    wc = w_in[..., RW_IN + RET_IN:]
    wb = jnp.concatenate([_ret_split_halves(wb[..., 0:RET_DIM]), _ret_split_halves(wb[..., RET_DIM:2 * RET_DIM]),
                          wb[..., 2 * RET_DIM:]], axis=-1)
    kr = wc[..., MLA_Q_RANK + MLA_KV_RANK:]
    pad = jnp.zeros(w_in.shape[:-1] + (MLA_C_PAD - MLA_IN - MLA_ROPE,), w_in.dtype)
    wc = jnp.concatenate([wc, _rot_half(kr), pad], axis=-1)
    return wa.astype(BF16), wb.astype(BF16), wc.astype(BF16)


def _prep_mla_weights(w_q_up, w_kv_up):
    depth = w_q_up.shape[0]
    qd = MLA_NOPE + MLA_ROPE
    hp = MLA_HEADS * MLA_HEAD_PAD
    q4 = w_q_up.reshape(depth, MLA_Q_RANK, MLA_HEADS, qd)
    wq = jnp.pad(q4, ((0, 0), (0, 0), (0, 0), (0, MLA_HEAD_PAD - qd))).reshape(depth, MLA_Q_RANK, hp)
    wqr = jnp.pad(_rot_half(q4[..., MLA_NOPE:]), ((0, 0), (0, 0), (0, 0), (MLA_NOPE, MLA_HEAD_PAD - qd)))
    wqr = wqr.reshape(depth, MLA_Q_RANK, hp)
    kv4 = w_kv_up.reshape(depth, MLA_KV_RANK, MLA_HEADS, MLA_NOPE + MLA_V)
    wk = jnp.pad(kv4[..., :MLA_NOPE], ((0, 0), (0, 0), (0, 0), (0, MLA_HEAD_PAD - MLA_NOPE))).reshape(depth, MLA_KV_RANK, hp)
    wvt = jnp.swapaxes(kv4[..., MLA_NOPE:].reshape(depth, MLA_KV_RANK, MLA_DIM), 1, 2)
    return wq.astype(BF16), wqr.astype(BF16), wk.astype(BF16), wvt.astype(BF16)


def _pad_rows(w, start, total):
    return jnp.pad(w, ((0, 0), (start, total - start - w.shape[1]), (0, 0)))


def _rope_table(positions):
    pos = positions.astype(F32)[..., None]
    inv_ret = ROPE_BASE ** (-jnp.arange(0, RET_HEAD_DIM, 2, dtype=F32) / RET_HEAD_DIM)
    inv_mla = ROPE_BASE ** (-jnp.arange(0, MLA_ROPE, 2, dtype=F32) / MLA_ROPE)
    ang_r = pos * inv_ret
    ang_m = pos * inv_mla
    zeros = jnp.zeros(pos.shape[:2] + (LANES - 2 * RET_FREQS - 2 * MLA_FREQS,), F32)
    return jnp.concatenate([jnp.cos(ang_r), jnp.sin(ang_r), jnp.cos(ang_m), jnp.sin(ang_m), zeros], axis=-1)


def kernel(x, positions, attn_norm, w_in, w_out, rw_mu, rw_w0, rw_w2, rw_a0, rw_a2, rw_g2, rw_k_k, rw_k_a, rw_r_k, rw_gn_w, rw_gn_b, rw_v0, rw_v1, rw_v2, mla_q_norm, mla_kv_norm, mla_w_q_up, mla_w_kv_up, ffn_norm, w_gate_up, w_down, final_norm):
    bsz, s, d = x.shape
    depth = w_in.shape[0]
    t = bsz * s
    tm = min(512, s)
    tf = w_down.shape[1] // 2
    hp = MLA_HEADS * MLA_HEAD_PAD

    tab = _rope_table(positions)
    wa, wb, wc = _prep_in_weights(w_in)
    wq, wqr, wk, wvt = _prep_mla_weights(mla_w_q_up, mla_w_kv_up)
    wo, wgu, wd = w_out.astype(BF16), w_gate_up.astype(BF16), w_down.astype(BF16)
    v0 = jnp.concatenate([jnp.zeros((1, RW_DIM), F32), rw_v0], axis=0)
    vec = jnp.stack([rw_w0, rw_a0, rw_k_k, rw_k_a, rw_r_k.reshape(depth, RW_DIM), rw_gn_w, rw_gn_b, v0], axis=1)
    vec = jnp.pad(vec, ((0, 0), (0, _VEC_ROWS - vec.shape[1]), (0, 0)))
    w2p = _pad_rows(rw_w2, 0, RW_LOWRANK).astype(BF16)
    a2p = _pad_rows(rw_a2, RW_DECAY_RANK, RW_LOWRANK).astype(BF16)
    g2p = _pad_rows(rw_g2, RW_DECAY_RANK + RW_A_RANK, RW_LOWRANK).astype(BF16)
    v1p = jnp.pad(rw_v1, ((0, 0), (0, 0), (0, LANES - RW_V_RANK))).astype(BF16)
    v2p = _pad_rows(rw_v2, 0, LANES).astype(BF16)
    row3 = lambda a: a[:, None, :]
    mu, an, qn, kn, fn = row3(rw_mu), row3(attn_norm), row3(mla_q_norm), row3(mla_kv_norm), row3(ffn_norm)

    h = x.reshape(t, d)
    v_first = None
    for l in range(depth):
        p_a, p_b, p_c = _inproj(h, an, wa, wb, wc, l, tm)
        res = _rwkv(p_a.reshape(bsz, s, RW_IN), v_first, mu, vec, w2p, a2p, g2p, v1p, v2p, l, bsz, min(256, s))
        if l == 0:
            y_a, v_first = res
        else:
            y_a = res
        y_b = _retention(p_b.reshape(bsz, s, RET_IN), tab, bsz)
        q, k, vt = _mla_proj(p_c, tab.reshape(t, LANES), qn, kn, wq, wqr, wk, wvt, l, bsz, tm)
        y_ct = _flash(q.reshape(bsz, s, hp), k.reshape(bsz, s, hp), vt, min(512, s))
        fin = final_norm[None, :] if l == depth - 1 else None
        h = _ffn(h, y_a.reshape(t, RW_DIM), y_b.reshape(t, RET_DIM), y_ct, wo, fn, wgu, wd, fin, l, tm, tf)
    return h.reshape(bsz, s, d)
```

```python
import functools
import math

import jax
import jax.numpy as jnp
import numpy as np
from jax import lax
from jax.experimental import pallas as pl
from jax.experimental.pallas import tpu as pltpu

F32 = jnp.float32
BF16 = jnp.bfloat16

NORM_EPS = 1e-6
ROPE_BASE = 10000.0
LANES = 128

RW_HEADS = 4
RW_HEAD_DIM = 64
RW_DIM = RW_HEADS * RW_HEAD_DIM
RW_DECAY_RANK = 32
RW_A_RANK = 32
RW_V_RANK = 32
RW_GATE_RANK = 64
RW_GN_EPS = 64e-5
RW_IN = 3 * RW_DIM + RW_DECAY_RANK + RW_A_RANK + RW_GATE_RANK
RW_LOWRANK = RW_DECAY_RANK + RW_A_RANK + RW_GATE_RANK
RW_CHUNK = 64

RET_HEADS = 4
RET_HEAD_DIM = 64
RET_DIM = RET_HEADS * RET_HEAD_DIM
RET_CHUNK = 128
RET_IN = 4 * RET_DIM
RET_FREQS = RET_HEAD_DIM // 2

MLA_HEADS = 8
MLA_NOPE = 64
MLA_ROPE = 32
MLA_V = 64
MLA_Q_RANK = 384
MLA_KV_RANK = 256
MLA_DIM = MLA_HEADS * MLA_V
MLA_IN = MLA_Q_RANK + MLA_KV_RANK + MLA_ROPE
MLA_HEAD_PAD = 128
MLA_C_PAD = 768
MLA_FREQS = MLA_ROPE // 2
MLA_VT_ROWS = MLA_V + 16

TAB_RET_COS = 0
TAB_RET_SIN = RET_FREQS
TAB_MLA_COS = 2 * RET_FREQS
TAB_MLA_SIN = 2 * RET_FREQS + MLA_FREQS

V7X_VMEM_LIMIT_BYTES = 56 * 1024 * 1024


def _cparams(semantics):
    return pltpu.CompilerParams(dimension_semantics=semantics, vmem_limit_bytes=V7X_VMEM_LIMIT_BYTES)


def _dot(a, b):
    return jnp.dot(a, b, preferred_element_type=F32)


def _dot_nt(a, b):
    return lax.dot_general(a, b, (((1,), (1,)), ((), ())), preferred_element_type=F32)


def _dot_tn(a, b):
    return lax.dot_general(a, b, (((0,), (0,)), ((), ())), preferred_element_type=F32)


def _split_dot(x, m_bf16, passes):
    acc = None
    rem = x
    for _ in range(passes):
        piece = rem.astype(BF16)
        part = _dot(piece, m_bf16)
        acc = part if acc is None else acc + part
        rem = rem - piece.astype(F32)
    return acc


def _split_dot_left(m_bf16, x):
    acc = None
    rem = x
    for _ in range(3):
        piece = rem.astype(BF16)
        part = _dot(m_bf16, piece)
        acc = part if acc is None else acc + part
        rem = rem - piece.astype(F32)
    return acc


def _rms(x, g):
    return x * lax.rsqrt(jnp.mean(x * x, axis=-1, keepdims=True) + NORM_EPS) * g


def _layer_spec(shape, layer, ngrid):
    zeros = (0,) * len(shape)
    if ngrid == 1:
        return pl.BlockSpec((None,) + tuple(shape), lambda i: (layer,) + zeros)
    return pl.BlockSpec((None,) + tuple(shape), lambda i, j: (layer,) + zeros)


def _const_spec(a, ngrid):
    zeros = (0,) * a.ndim
    if ngrid == 1:
        return pl.BlockSpec(a.shape, lambda i: zeros)
    return pl.BlockSpec(a.shape, lambda i, j: zeros)


def _inproj_kernel(x_ref, g_ref, wa_ref, wb_ref, wc_ref, pa_ref, pb_ref, pc_ref):
    hn = _rms(x_ref[...], g_ref[...]).astype(BF16)
    pa_ref[...] = _dot(hn, wa_ref[...])
    pb_ref[...] = _dot(hn, wb_ref[...])
    pc_ref[...] = _dot(hn, wc_ref[...])


def _inproj(x, g, wa, wb, wc, layer, tm):
    t, d = x.shape
    row = lambda n: pl.BlockSpec((tm, n), lambda i: (i, 0))
    ws = (wa, wb, wc)
    return pl.pallas_call(
        _inproj_kernel,
        grid=(t // tm,),
        in_specs=[row(d), _layer_spec(g.shape[1:], layer, 1)] + [_layer_spec(w.shape[1:], layer, 1) for w in ws],
        out_specs=[row(w.shape[2]) for w in ws],
        out_shape=[jax.ShapeDtypeStruct((t, w.shape[2]), F32) for w in ws],
        compiler_params=_cparams(("parallel",)),
        name="inproj",
    )(x, g, wa, wb, wc)


_VEC_ROWS = 16
(_V_W0, _V_A0, _V_KK, _V_KA, _V_RK, _V_GNW, _V_GNB, _V_V0) = range(8)


def _rwkv_kernel(*refs, has_vres, nb, tb):
    if has_vres:
        (p_ref, vf_ref, mu_ref, vec_ref, w2_ref, a2_ref, g2_ref, v1_ref, v2_ref, bd_ref, tri_ref,
         o_ref, carry_ref, s_ref, r_s, k_s, v_s, lw_s, kk_s, ka_s, g_s, bon_s, y_s) = refs
    else:
        (p_ref, mu_ref, vec_ref, w2_ref, a2_ref, g2_ref, bd_ref, tri_ref,
         o_ref, vfo_ref, carry_ref, s_ref, r_s, k_s, v_s, lw_s, kk_s, ka_s, g_s, bon_s, y_s) = refs

    L = RW_CHUNK
    C = RW_DIM
    tstep = pl.program_id(1)

    @pl.when(tstep == 0)
    def _():
        carry_ref[...] = jnp.zeros_like(carry_ref)
        s_ref[...] = jnp.zeros_like(s_ref)

    vec = vec_ref[...]
    row = lambda i: vec[i:i + 1, :]
    bd = bd_ref[...]
    mu = mu_ref[...]

    def bdsum(x):
        return _split_dot(x, bd, 2)

    def prep(b, _):
        p = p_ref[b]
        shifted = pltpu.roll(p, 1, 0)
        first = lax.broadcasted_iota(jnp.int32, p.shape, 0) == 0
        p_prev = jnp.where(first, carry_ref[b], shifted)
        carry_ref[b] = p[tb - 1:tb, :]
        ps = p + (p_prev - p) * mu
        r = ps[:, 0:C]
        k = ps[:, C:2 * C]
        v = ps[:, 2 * C:3 * C]
        lr = ps[:, 3 * C:3 * C + RW_LOWRANK]
        z = row(_V_W0) + _dot(jnp.tanh(lr).astype(BF16), w2_ref[...])
        lw = -math.exp(-0.5) * jax.nn.sigmoid(z)
        a = jax.nn.sigmoid(row(_V_A0) + _dot(lr.astype(BF16), a2_ref[...]))
        g = _dot(jax.nn.sigmoid(lr).astype(BF16), g2_ref[...])
        if has_vres:
            lat = _dot(v.astype(BF16), v1_ref[...]).astype(BF16)
            v = v + (vf_ref[b] - v) * jax.nn.sigmoid(row(_V_V0) + _dot(lat, v2_ref[...]))
        else:
            vfo_ref[b] = v
        kk = k * row(_V_KK)
        kk = kk * lax.rsqrt(jnp.maximum(bdsum(kk * kk), 1e-24))
        k2 = k * (1.0 + (a - 1.0) * row(_V_KA))
        r_s[b] = r
        k_s[b] = k2
        v_s[b] = v
        lw_s[b] = lw
        kk_s[b] = kk
        ka_s[b] = kk * a
        g_s[b] = g
        bon_s[b] = bdsum(r * k2 * row(_V_RK)) * v
        return 0

    lax.fori_loop(0, nb, prep, 0)

    hl = RW_HEADS * L
    r_i = lax.broadcasted_iota(jnp.int32, (hl, C), 0)
    c_i = lax.broadcasted_iota(jnp.int32, (hl, C), 1)
    headmask = (r_i // L) == (c_i // RW_HEAD_DIM)
    t_i = lax.broadcasted_iota(jnp.int32, (L, hl), 0)
    s_i = lax.broadcasted_iota(jnp.int32, (L, hl), 1) % L
    strict = t_i > s_i
    incl = t_i >= s_i
    tri = tri_ref[...]

    def masked4(x):
        return jnp.where(headmask, jnp.concatenate([x] * RW_HEADS, axis=0), 0.0).astype(BF16)

    def chunk(j, _):
        sl = pl.ds(pl.multiple_of(j * L, L), L)
        bs = range(nb)
        r = [r_s[b, sl, :] for b in bs]
        k2 = [k_s[b, sl, :] for b in bs]
        v = [v_s[b, sl, :] for b in bs]
        lw = [lw_s[b, sl, :] for b in bs]
        kk = [kk_s[b, sl, :] for b in bs]
        ka = [ka_s[b, sl, :] for b in bs]
        c = [_split_dot_left(tri, lw[b]) for b in bs]
        c_last = [c[b][L - 1:L, :] for b in bs]
        sig = [0.5 * c_last[b] for b in bs]
        e_neg = [jnp.exp(sig[b] - c[b]) for b in bs]
        kt = [k2[b] * e_neg[b] for b in bs]
        bt = [ka[b] * e_neg[b] for b in bs]
        x = [jnp.concatenate([-(kk[b] * jnp.exp(c[b] - lw[b] - sig[b])), r[b] * jnp.exp(c[b] - sig[b])],
                             axis=0).astype(BF16) for b in bs]
        kb = [jnp.concatenate([masked4(kt[b]), masked4(bt[b])], axis=0) for b in bs]
        ab = [_dot_nt(x[b], kb[b]) for b in bs]
        s_old = [s_ref[b] for b in bs]
        xs = [_dot_nt(x[b], (s_old[b] * jnp.exp(sig[b])).astype(BF16)) for b in bs]
        a_k = [jnp.concatenate([jnp.where(strict, ab[b][0:L, 0:hl], 0.0), jnp.where(incl, ab[b][L:2 * L, 0:hl], 0.0)],
                               axis=0).astype(BF16) for b in bs]
        n_w = [jnp.where(strict, ab[b][0:L, hl:2 * hl], 0.0) for b in bs]
        a_rb = [jnp.where(incl, ab[b][L:2 * L, hl:2 * hl], 0.0).astype(BF16) for b in bs]
        av = [_dot(a_k[b], masked4(v[b])) for b in bs]
        u = [av[b][0:L] + xs[b][0:L] for b in bs]
        for i in range(6):
            n_b = [n_w[b].astype(BF16) for b in bs]
            u = [u[b] + _dot(n_b[b], masked4(u[b])) for b in bs]
            if i < 5:
                n_w = [_dot(n_b[b], masked4(n_w[b])) for b in bs]
        y = [av[b][L:2 * L] + xs[b][L:2 * L] + _dot(a_rb[b], masked4(u[b])) for b in bs]
        for b in bs:
            y_s[b, sl, :] = y[b]
        scale = [jnp.exp(c_last[b] - sig[b]) for b in bs]
        kbs = [jnp.concatenate([kt[b] * scale[b], bt[b] * scale[b]], axis=0).astype(BF16) for b in bs]
        vu = [jnp.concatenate([v[b], u[b]], axis=0).astype(BF16) for b in bs]
        upd = [_dot_tn(vu[b], kbs[b]) for b in bs]
        for b in bs:
            s_ref[b] = s_old[b] * jnp.exp(c_last[b]) + jnp.where(headmask, upd[b], 0.0)
        return 0

    lax.fori_loop(0, tb // L, chunk, 0)

    def post(b, _):
        y = y_s[b]
        mean = bdsum(y) * (1.0 / RW_HEAD_DIM)
        d = y - mean
        var = bdsum(d * d) * (1.0 / RW_HEAD_DIM)
        yn = d * lax.rsqrt(var + RW_GN_EPS) * row(_V_GNW) + row(_V_GNB)
        o_ref[b] = ((yn + bon_s[b]) * g_s[b]).astype(o_ref.dtype)
        return 0

    lax.fori_loop(0, nb, post, 0)


def _block_ones(n, blk):
    i = np.arange(n)
    return (i[:, None] // blk == i[None, :] // blk).astype(np.float32)


def _rwkv(p_a, v_first, mu, vec, w2p, a2p, g2p, v1p, v2p, layer, nb, tb):
    bsz, s, _ = p_a.shape
    has_vres = v_first is not None
    bd = jnp.asarray(_block_ones(RW_DIM, RW_HEAD_DIM), BF16)
    tri = jnp.asarray(np.tril(np.ones((RW_CHUNK, RW_CHUNK), np.float32)), BF16)
    blk = lambda n: pl.BlockSpec((nb, tb, n), lambda i, j: (i, j, 0))
    lay = lambda a, l: _layer_spec(a.shape[1:], l, 2)
    common = [lay(mu, layer), lay(vec, layer), lay(w2p, layer), lay(a2p, layer), lay(g2p, layer)]
    consts = [_const_spec(bd, 2), _const_spec(tri, 2)]
    if has_vres:
        args = (p_a, v_first, mu, vec, w2p, a2p, g2p, v1p, v2p, bd, tri)
        in_specs = [blk(RW_IN), blk(RW_DIM)] + common + [lay(v1p, layer - 1), lay(v2p, layer - 1)] + consts
        out_specs = blk(RW_DIM)
        out_shape = jax.ShapeDtypeStruct((bsz, s, RW_DIM), BF16)
    else:
        args = (p_a, mu, vec, w2p, a2p, g2p, bd, tri)
        in_specs = [blk(RW_IN)] + common + consts
        out_specs = [blk(RW_DIM), blk(RW_DIM)]
        out_shape = [jax.ShapeDtypeStruct((bsz, s, RW_DIM), BF16), jax.ShapeDtypeStruct((bsz, s, RW_DIM), F32)]
    big = pltpu.VMEM((nb, tb, RW_DIM), F32)
    scratch = [pltpu.VMEM((nb, 1, RW_IN), F32), pltpu.VMEM((nb, RW_DIM, RW_DIM), F32)] + [big] * 9
    return pl.pallas_call(
        functools.partial(_rwkv_kernel, has_vres=has_vres, nb=nb, tb=tb),
        grid=(bsz // nb, s // tb),
        in_specs=in_specs,
        out_specs=out_specs,
        out_shape=out_shape,
        scratch_shapes=scratch,
        compiler_params=_cparams(("parallel", "arbitrary")),
        name="rwkv7",
    )(*args)


def _ret_tables():
    h, d, c = RET_HEADS, RET_HEAD_DIM, RET_CHUNK
    log_gamma = np.log(1.0 - 2.0 ** (-5.0 - np.arange(h, dtype=np.float64)))
    idx = np.arange(c, dtype=np.float64)
    diff = idx[:, None] - idx[None, :]
    dmask = np.where(diff >= 0, np.exp(log_gamma[:, None, None] * np.maximum(diff, 0.0)), 0.0)
    lane = np.arange(RET_DIM)
    head_qk = (lane % (RET_DIM // 2)) // (d // 2)
    head_v = lane // d
    qdec = np.exp(log_gamma[head_qk][None, :] * (idx[:, None] + 1.0))
    kdec = np.exp(log_gamma[head_qk][None, :] * (c - 1.0 - idx[:, None])) * d ** -0.5
    hm_qk = (np.arange(h)[:, None, None] == head_qk[None, None, :]) * np.ones((1, c, 1))
    hm_v = (np.arange(h)[:, None, None] == head_v[None, None, :]) * np.ones((1, c, 1))
    block = head_qk[:, None] == head_v[None, :]
    rdec = np.where(block, np.exp(log_gamma[head_qk] * c)[:, None], 0.0)
    pcos = np.zeros((LANES, RET_DIM // 2), np.float32)
    psin = np.zeros((LANES, RET_DIM // 2), np.float32)
    for hh in range(h):
        for j in range(RET_FREQS):
            pcos[TAB_RET_COS + j, hh * RET_FREQS + j] = 1.0
            psin[TAB_RET_SIN + j, hh * RET_FREQS + j] = 1.0
    f = lambda a: jnp.asarray(a, F32)
    return (f(dmask.reshape(h * c, c) * d ** -0.5), f(qdec), f(kdec), f(hm_qk.reshape(h * c, RET_DIM)),
            f(hm_v.reshape(h * c, RET_DIM)), f(block), f(rdec),
            jnp.asarray(np.concatenate([pcos, psin], axis=1), BF16))


def _ret_kernel(p_ref, tab_ref, dmask_ref, qdec_ref, kdec_ref, hmqk_ref, hmv_ref, block_ref, rdec_ref, pcs_ref,
                bd_ref, o_ref, r_ref, *, nb):
    c, dm = RET_CHUNK, RET_DIM
    half = dm // 2

    @pl.when(pl.program_id(0) == 0)
    def _():
        r_ref[...] = jnp.zeros_like(r_ref)

    bs = range(nb)
    cs = [_split_dot(tab_ref[b], pcs_ref[...], 3) for b in bs]
    p = [p_ref[b] for b in bs]

    def rope(x, t):
        cos, sin = t[:, :half], t[:, half:]
        x1, x2 = x[:, :half], x[:, half:]
        return jnp.concatenate([x1 * cos - x2 * sin, x2 * cos + x1 * sin], axis=1)

    q = [rope(p[b][:, 0:dm], cs[b]) for b in bs]
    k = [rope(p[b][:, dm:2 * dm], cs[b]) for b in bs]
    vb = [p[b][:, 2 * dm:3 * dm].astype(BF16) for b in bs]
    q_bd = [(jnp.concatenate([q[b]] * RET_HEADS, axis=0) * hmqk_ref[...]).astype(BF16) for b in bs]
    scores = [(_dot_nt(q_bd[b], k[b].astype(BF16)) * dmask_ref[...]).astype(BF16) for b in bs]
    o = [_dot(scores[b], vb[b]) * hmv_ref[...] for b in bs]
    r_old = [r_ref[b] for b in bs]
    cross = [_dot((q[b] * qdec_ref[...]).astype(BF16), r_old[b].astype(BF16)) for b in bs]
    upd = [_dot_tn((k[b] * kdec_ref[...]).astype(BF16), vb[b]) for b in bs]
    for b in bs:
        r_ref[b] = r_old[b] * rdec_ref[...] + upd[b] * block_ref[...]
    y = [o[b][0:c] + o[b][c:2 * c] + o[b][2 * c:3 * c] + o[b][3 * c:4 * c] + cross[b] for b in bs]
    ms = [_split_dot(y[b] * y[b], bd_ref[...], 2) * (1.0 / RET_HEAD_DIM) for b in bs]
    for b in bs:
        g = p[b][:, 3 * dm:4 * dm]
        o_ref[b] = (g * jax.nn.sigmoid(g) * y[b] * lax.rsqrt(ms[b] + NORM_EPS)).astype(o_ref.dtype)


def _retention(p_b, tab, nb):
    bsz, s, _ = p_b.shape
    consts = _ret_tables() + (jnp.asarray(_block_ones(RET_DIM, RET_HEAD_DIM), BF16),)
    blk = lambda n: pl.BlockSpec((nb, RET_CHUNK, n), lambda j, i: (i, j, 0))
    full = lambda a: pl.BlockSpec(a.shape, lambda j, i: (0,) * a.ndim)
    assert bsz == nb, "one batch group per time block keeps the state scratch simple"
    return pl.pallas_call(
        functools.partial(_ret_kernel, nb=nb),
        grid=(s // RET_CHUNK, bsz // nb),
        in_specs=[blk(RET_IN), blk(LANES)] + [full(a) for a in consts],
        out_specs=blk(RET_DIM),
        out_shape=jax.ShapeDtypeStruct((bsz, s, RET_DIM), BF16),
        scratch_shapes=[pltpu.VMEM((nb, RET_DIM, RET_DIM), F32)],
        compiler_params=_cparams(("arbitrary", "arbitrary")),
        name="retention",
    )(p_b, tab, *consts)


def _mla_tables():
    pm = np.zeros((LANES, 3 * MLA_HEAD_PAD), np.float32)
    for j in range(MLA_FREQS):
        for rep in range(2):
            pm[TAB_MLA_COS + j, MLA_NOPE + rep * MLA_FREQS + j] = 1.0
            pm[TAB_MLA_SIN + j, MLA_HEAD_PAD + MLA_NOPE + rep * MLA_FREQS + j] = 1.0
            pm[TAB_MLA_COS + j, 2 * MLA_HEAD_PAD + rep * MLA_FREQS + j] = 1.0
            pm[TAB_MLA_SIN + j, 2 * MLA_HEAD_PAD + MLA_ROPE + rep * MLA_FREQS + j] = 1.0
    ones = np.zeros((1, MLA_HEAD_PAD), np.float32)
    ones[0, :MLA_NOPE] = 1.0
    place = np.zeros((MLA_HEAD_PAD, MLA_HEADS * MLA_HEAD_PAD), np.float32)
    for h in range(MLA_HEADS):
        for j in range(MLA_ROPE):
            place[j, h * MLA_HEAD_PAD + MLA_NOPE + j] = 1.0
            place[MLA_ROPE + j, h * MLA_HEAD_PAD + MLA_NOPE + j] = 1.0
    return jnp.asarray(pm, BF16), jnp.asarray(ones, F32), jnp.asarray(place, BF16)


def _mla_proj_kernel(pc_ref, tab_ref, qn_ref, kn_ref, wq_ref, wqr_ref, wk_ref, wvt_ref, pm_ref, ones_ref, place_ref,
                     q_ref, k_ref, vt_ref):
    pc = pc_ref[...]
    tabs = _split_dot(tab_ref[...], pm_ref[...], 3)
    cq = tabs[:, 0:MLA_HEAD_PAD] + ones_ref[...]
    sq = tabs[:, MLA_HEAD_PAD:2 * MLA_HEAD_PAD]
    ck = tabs[:, 2 * MLA_HEAD_PAD:3 * MLA_HEAD_PAD]
    nq = _rms(pc[:, 0:MLA_Q_RANK], qn_ref[...]).astype(BF16)
    nkv = _rms(pc[:, MLA_Q_RANK:MLA_Q_RANK + MLA_KV_RANK], kn_ref[...]).astype(BF16)
    kr = (pc[:, MLA_Q_RANK + MLA_KV_RANK:MLA_C_PAD] * ck).astype(BF16)
    qa = _dot(nq, wq_ref[...])
    qb = _dot(nq, wqr_ref[...])
    scale = (MLA_NOPE + MLA_ROPE) ** -0.5 * math.log2(math.e)
    for h in range(MLA_HEADS):
        hs = slice(h * MLA_HEAD_PAD, (h + 1) * MLA_HEAD_PAD)
        q_ref[:, hs] = ((qa[:, hs] * cq + qb[:, hs] * sq) * scale).astype(q_ref.dtype)
    k_ref[...] = (_dot(nkv, wk_ref[...]) + _dot(kr, place_ref[...])).astype(k_ref.dtype)
    vt = _dot_nt(wvt_ref[...], nkv).astype(vt_ref.dtype)
    ones = jnp.ones((MLA_VT_ROWS - MLA_V, vt.shape[1]), vt_ref.dtype)
    for h in range(MLA_HEADS):
        vt_ref[0, h * MLA_VT_ROWS:h * MLA_VT_ROWS + MLA_V, :] = vt[h * MLA_V:(h + 1) * MLA_V, :]
        vt_ref[0, h * MLA_VT_ROWS + MLA_V:(h + 1) * MLA_VT_ROWS, :] = ones


def _mla_proj(p_c, tab, qn, kn, wq, wqr, wk, wvt, layer, bsz, tm):
    t = p_c.shape[0]
    s = t // bsz
    per = s // tm
    row = lambda n: pl.BlockSpec((tm, n), lambda i: (i, 0))
    pm, ones, place = _mla_tables()
    hp = MLA_HEADS * MLA_HEAD_PAD
    lay = lambda a: _layer_spec(a.shape[1:], layer, 1)
    return pl.pallas_call(
        _mla_proj_kernel,
        grid=(t // tm,),
        in_specs=[row(MLA_C_PAD), row(LANES), lay(qn), lay(kn), lay(wq), lay(wqr), lay(wk), lay(wvt),
                  _const_spec(pm, 1), _const_spec(ones, 1), _const_spec(place, 1)],
        out_specs=[row(hp), row(hp), pl.BlockSpec((1, MLA_HEADS * MLA_VT_ROWS, tm), lambda i: (i // per, 0, i % per))],
        out_shape=[jax.ShapeDtypeStruct((t, hp), BF16), jax.ShapeDtypeStruct((t, hp), BF16),
                   jax.ShapeDtypeStruct((bsz, MLA_HEADS * MLA_VT_ROWS, s), BF16)],
        compiler_params=_cparams(("parallel",)),
        name="mla_proj",
    )(p_c, tab, qn, kn, wq, wqr, wk, wvt, pm, ones, place)


_NEG = -1e30
FLASH_ISSUE_AHEAD = 3


def _flash_kernel(qi_ref, ki_ref, q_ref, k_ref, vt_ref, o_ref, m_ref, l_ref, acc_ref, *, tq, tk):
    pair = pl.program_id(1)
    qi = qi_ref[pair]
    ki = ki_ref[pair]

    @pl.when(ki == 0)
    def _():
        m_ref[...] = jnp.full_like(m_ref, _NEG)
        l_ref[...] = jnp.zeros_like(l_ref)
        acc_ref[...] = jnp.zeros_like(acc_ref)

    def step(masked):
        if masked:
            kv_i = lax.broadcasted_iota(jnp.int32, (tk, tq), 0)
            q_i = lax.broadcasted_iota(jnp.int32, (tk, tq), 1)
            keep = kv_i <= q_i
        def scores(h):
            hs = slice(h * MLA_HEAD_PAD, (h + 1) * MLA_HEAD_PAD)
            return _dot_nt(k_ref[0, :, hs], q_ref[0, :, hs])

        ahead = [scores(h) for h in range(FLASH_ISSUE_AHEAD)]
        for h in range(MLA_HEADS):
            vs = slice(h * MLA_V, (h + 1) * MLA_V)
            st = ahead.pop(0)
            if h + FLASH_ISSUE_AHEAD < MLA_HEADS:
                ahead.append(scores(h + FLASH_ISSUE_AHEAD))
            if masked:
                st = jnp.where(keep, st, _NEG)
            m_old = m_ref[h:h + 1, :]
            m_new = jnp.maximum(m_old, jnp.max(st, axis=0, keepdims=True))
            alpha = jnp.exp2(m_old - m_new)
            p = jnp.exp2(st - m_new).astype(BF16)
            pv = _dot(vt_ref[0, h * MLA_VT_ROWS:(h + 1) * MLA_VT_ROWS, :], p)
            l_ref[h:h + 1, :] = alpha * l_ref[h:h + 1, :] + pv[MLA_V:MLA_V + 1, :]
            m_ref[h:h + 1, :] = m_new
            acc_ref[vs, :] = acc_ref[vs, :] * alpha + pv[0:MLA_V, :]

    @pl.when(ki < qi)
    def _():
        step(False)

    @pl.when(ki == qi)
    def _():
        step(True)
        for h in range(MLA_HEADS):
            vs = slice(h * MLA_V, (h + 1) * MLA_V)
            o_ref[0, vs, :] = (acc_ref[vs, :] / l_ref[h:h + 1, :]).astype(o_ref.dtype)


def _flash(q, k, vt, tq):
    bsz, s, hp = q.shape
    tk = tq
    nq = s // tq
    pairs = [(i, j) for i in range(nq) for j in range(i + 1)]
    qi_tab = jnp.asarray(np.asarray([p[0] for p in pairs], np.int32))
    ki_tab = jnp.asarray(np.asarray([p[1] for p in pairs], np.int32))
    grid_spec = pltpu.PrefetchScalarGridSpec(
        num_scalar_prefetch=2,
        grid=(bsz, len(pairs)),
        in_specs=[pl.BlockSpec((1, tq, hp), lambda b, p, qi, ki: (b, qi[p], 0)),
                  pl.BlockSpec((1, tk, hp), lambda b, p, qi, ki: (b, ki[p], 0)),
                  pl.BlockSpec((1, MLA_HEADS * MLA_VT_ROWS, tk), lambda b, p, qi, ki: (b, 0, ki[p]))],
        out_specs=pl.BlockSpec((1, MLA_DIM, tq), lambda b, p, qi, ki: (b, 0, qi[p])),
        scratch_shapes=[pltpu.VMEM((MLA_HEADS, tq), F32), pltpu.VMEM((MLA_HEADS, tq), F32),
                        pltpu.VMEM((MLA_DIM, tq), F32)],
    )
    return pl.pallas_call(
        functools.partial(_flash_kernel, tq=tq, tk=tk),
        grid_spec=grid_spec,
        out_shape=jax.ShapeDtypeStruct((bsz, MLA_DIM, s), BF16),
        compiler_params=_cparams(("parallel", "arbitrary")),
        name="mla_flash",
    )(qi_tab, ki_tab, q, k, vt)


FFN_SUB_ROWS = 256


def _ffn_kernel(h_ref, ya_ref, yb_ref, yct_ref, woa_ref, wob_ref, woc_ref, fg_ref, wg_ref, wu_ref, wd_ref, *rest,
                final, tm):
    if final:
        fin_ref, o_ref, h1_ref, hn_ref = rest
    else:
        o_ref, h1_ref, hn_ref = rest
    h1 = (h_ref[...] + _dot(ya_ref[...], woa_ref[...]) + _dot(yb_ref[...], wob_ref[...])
          + _dot_tn(yct_ref[0], woc_ref[...]))
    h1_ref[...] = h1
    hn_ref[...] = _rms(h1, fg_ref[...]).astype(BF16)

    def gate_up(r):
        x = hn_ref[r * FFN_SUB_ROWS:(r + 1) * FFN_SUB_ROWS, :]
        return _dot(x, wg_ref[...]), _dot(x, wu_ref[...])

    nsub = tm // FFN_SUB_ROWS
    nxt = gate_up(0)
    for r in range(nsub):
        rows = slice(r * FFN_SUB_ROWS, (r + 1) * FFN_SUB_ROWS)
        gate, up = nxt
        if r + 1 < nsub:
            nxt = gate_up(r + 1)
        act = (gate * jax.nn.sigmoid(gate) * up).astype(BF16)
        out = h1_ref[rows, :] + _dot(act, wd_ref[...])
        if final:
            out = _rms(out, fin_ref[...])
        o_ref[rows, :] = out


def _ffn(h, ya, yb, yct, wo, fg, wgu, wd, fin, layer, tm):
    t, d = h.shape
    dff = wd.shape[1]
    per = yct.shape[2] // tm
    row = lambda n: pl.BlockSpec((tm, n), lambda i: (i, 0))
    once = pl.Buffered(1)
    final = fin is not None
    in_specs = [row(d), row(RW_DIM), row(RET_DIM),
                pl.BlockSpec((1, MLA_DIM, tm), lambda i: (i // per, 0, i % per)),
                pl.BlockSpec((None, RW_DIM, d), lambda i: (layer, 0, 0), pipeline_mode=once),
                pl.BlockSpec((None, RET_DIM, d), lambda i: (layer, 1, 0), pipeline_mode=once),
                pl.BlockSpec((None, MLA_DIM, d), lambda i: (layer, 1, 0), pipeline_mode=once),
                _layer_spec(fg.shape[1:], layer, 1),
                pl.BlockSpec((None, d, dff), lambda i: (layer, 0, 0), pipeline_mode=once),
                pl.BlockSpec((None, d, dff), lambda i: (layer, 0, 1), pipeline_mode=once),
                pl.BlockSpec((None, dff, d), lambda i: (layer, 0, 0), pipeline_mode=once)]
    args = [h, ya, yb, yct, wo, wo, wo, fg, wgu, wgu, wd]
    if final:
        in_specs.append(_const_spec(fin, 1))
        args.append(fin)
    return pl.pallas_call(
        functools.partial(_ffn_kernel, final=final, tm=tm),
        grid=(t // tm,),
        in_specs=in_specs,
        out_specs=row(d),
        out_shape=jax.ShapeDtypeStruct((t, d), F32),
        scratch_shapes=[pltpu.VMEM((tm, d), F32), pltpu.VMEM((tm, d), BF16)],
        compiler_params=_cparams(("parallel",)),
        name="outproj_ffn",
    )(*args)


def _relayout(w, m, transpose=False):
    out = "lki" if transpose else "lik"
    return jnp.einsum("lij,jk->" + out, w.astype(BF16), jnp.asarray(m, BF16), preferred_element_type=BF16)


def _rot_half_into(m, src, dst, n, sign=1.0):
    for j in range(n // 2):
        m[src + n // 2 + j, dst + j] = -sign
        m[src + j, dst + n // 2 + j] = sign


def _prep_in_weights(w_in):
    perm = np.zeros((2 * RET_DIM, 2 * RET_DIM), np.float32)
    for part in range(2):
        for hh in range(RET_HEADS):
            for half in range(2):
                for j in range(RET_FREQS):
                    src = part * RET_DIM + hh * RET_HEAD_DIM + half * RET_FREQS + j
                    dst = part * RET_DIM + half * (RET_DIM // 2) + hh * RET_FREQS + j
                    perm[src, dst] = 1.0
    mc = np.zeros((MLA_IN, MLA_C_PAD), np.float32)
    mc[np.arange(MLA_IN), np.arange(MLA_IN)] = 1.0
    _rot_half_into(mc, MLA_Q_RANK + MLA_KV_RANK, MLA_IN, MLA_ROPE)
    b0 = RW_IN
    c0 = RW_IN + RET_IN
    wa = w_in[..., :b0].astype(BF16)
    wb = jnp.concatenate([_relayout(w_in[..., b0:b0 + 2 * RET_DIM], perm), w_in[..., b0 + 2 * RET_DIM:c0].astype(BF16)],
                         axis=-1)
    wc = _relayout(w_in[..., c0:], mc)
    return wa, wb, wc


def _prep_mla_weights(w_q_up, w_kv_up):
    qd = MLA_NOPE + MLA_ROPE
    kvd = MLA_NOPE + MLA_V
    hp = MLA_HEADS * MLA_HEAD_PAD
    mq = np.zeros((MLA_HEADS * qd, hp), np.float32)
    mqr = np.zeros((MLA_HEADS * qd, hp), np.float32)
    mk = np.zeros((MLA_HEADS * kvd, hp), np.float32)
    mv = np.zeros((MLA_HEADS * kvd, MLA_DIM), np.float32)
    for h in range(MLA_HEADS):
        for c in range(qd):
            mq[h * qd + c, h * MLA_HEAD_PAD + c] = 1.0
        _rot_half_into(mqr, h * qd + MLA_NOPE, h * MLA_HEAD_PAD + MLA_NOPE, MLA_ROPE)
        for c in range(MLA_NOPE):
            mk[h * kvd + c, h * MLA_HEAD_PAD + c] = 1.0
        for c in range(MLA_V):
            mv[h * kvd + MLA_NOPE + c, h * MLA_V + c] = 1.0
    return (_relayout(w_q_up, mq), _relayout(w_q_up, mqr), _relayout(w_kv_up, mk),
            _relayout(w_kv_up, mv, transpose=True))


def _pad_rows(w, start, total):
    return jnp.pad(w, ((0, 0), (start, total - start - w.shape[1]), (0, 0)))


def _rope_table(positions):
    inv_ret = ROPE_BASE ** (-jnp.arange(0, RET_HEAD_DIM, 2, dtype=F32) / RET_HEAD_DIM)
    inv_mla = ROPE_BASE ** (-jnp.arange(0, MLA_ROPE, 2, dtype=F32) / MLA_ROPE)
    inv = jnp.concatenate([inv_ret, inv_ret, inv_mla, inv_mla, jnp.zeros((LANES - TAB_MLA_SIN - MLA_FREQS,), F32)])
    lane = np.arange(LANES)
    is_cos = (lane < TAB_RET_SIN) | ((lane >= TAB_MLA_COS) & (lane < TAB_MLA_SIN))
    ang = positions.astype(F32)[..., None] * inv
    return jnp.where(jnp.asarray(is_cos), jnp.cos(ang), jnp.sin(ang))


def kernel(x, positions, attn_norm, w_in, w_out, rw_mu, rw_w0, rw_w2, rw_a0, rw_a2, rw_g2, rw_k_k, rw_k_a, rw_r_k, rw_gn_w, rw_gn_b, rw_v0, rw_v1, rw_v2, mla_q_norm, mla_kv_norm, mla_w_q_up, mla_w_kv_up, ffn_norm, w_gate_up, w_down, final_norm):
    bsz, s, d = x.shape
    depth = w_in.shape[0]
    t = bsz * s
    tm = min(512, s)
    hp = MLA_HEADS * MLA_HEAD_PAD

    tab = _rope_table(positions)
    wa, wb, wc = _prep_in_weights(w_in)
    wq, wqr, wk, wvt = _prep_mla_weights(mla_w_q_up, mla_w_kv_up)
    wo, wgu, wd = w_out.astype(BF16), w_gate_up.astype(BF16), w_down.astype(BF16)
    v0 = jnp.concatenate([jnp.zeros((1, RW_DIM), F32), rw_v0], axis=0)
    vec = jnp.stack([rw_w0, rw_a0, rw_k_k, rw_k_a, rw_r_k.reshape(depth, RW_DIM), rw_gn_w, rw_gn_b, v0], axis=1)
    vec = jnp.pad(vec, ((0, 0), (0, _VEC_ROWS - vec.shape[1]), (0, 0)))
    w2p = _pad_rows(rw_w2, 0, RW_LOWRANK).astype(BF16)
    a2p = _pad_rows(rw_a2, RW_DECAY_RANK, RW_LOWRANK).astype(BF16)
    g2p = _pad_rows(rw_g2, RW_DECAY_RANK + RW_A_RANK, RW_LOWRANK).astype(BF16)
    v1p = jnp.pad(rw_v1, ((0, 0), (0, 0), (0, LANES - RW_V_RANK))).astype(BF16)
    v2p = _pad_rows(rw_v2, 0, LANES).astype(BF16)
    row3 = lambda a: a[:, None, :]
    mu, an, qn, kn, fn = row3(rw_mu), row3(attn_norm), row3(mla_q_norm), row3(mla_kv_norm), row3(ffn_norm)

    h = x.reshape(t, d)
    v_first = None
    for l in range(depth):
        p_a, p_b, p_c = _inproj(h, an, wa, wb, wc, l, tm)
        res = _rwkv(p_a.reshape(bsz, s, RW_IN), v_first, mu, vec, w2p, a2p, g2p, v1p, v2p, l, bsz, min(256, s))
        if l == 0:
            y_a, v_first = res
        else:
            y_a = res
        y_b = _retention(p_b.reshape(bsz, s, RET_IN), tab, bsz)
        q, k, vt = _mla_proj(p_c, tab.reshape(t, LANES), qn, kn, wq, wqr, wk, wvt, l, bsz, tm)
        y_ct = _flash(q.reshape(bsz, s, hp), k.reshape(bsz, s, hp), vt, min(512, s))
        fin = final_norm[None, :] if l == depth - 1 else None
        h = _ffn(h, y_a.reshape(t, RW_DIM), y_b.reshape(t, RET_DIM), y_ct, wo, fn, wgu, wd, fin, l, tm)
    return h.reshape(bsz, s, d)
```

```python
import functools
import math

import jax
import jax.numpy as jnp
import numpy as np
from jax import lax
from jax.experimental import pallas as pl
from jax.experimental.pallas import tpu as pltpu

F32 = jnp.float32
BF16 = jnp.bfloat16

NORM_EPS = 1e-6
ROPE_BASE = 10000.0
LANES = 128

RW_HEADS = 4
RW_HEAD_DIM = 64
RW_DIM = RW_HEADS * RW_HEAD_DIM
RW_DECAY_RANK = 32
RW_A_RANK = 32
RW_V_RANK = 32
RW_GATE_RANK = 64
RW_GN_EPS = 64e-5
RW_IN = 3 * RW_DIM + RW_DECAY_RANK + RW_A_RANK + RW_GATE_RANK
RW_LOWRANK = RW_DECAY_RANK + RW_A_RANK + RW_GATE_RANK
RW_CHUNK = 64

RET_HEADS = 4
RET_HEAD_DIM = 64
RET_DIM = RET_HEADS * RET_HEAD_DIM
RET_CHUNK = 128
RET_IN = 4 * RET_DIM
RET_FREQS = RET_HEAD_DIM // 2

MLA_HEADS = 8
MLA_NOPE = 64
MLA_ROPE = 32
MLA_V = 64
MLA_Q_RANK = 384
MLA_KV_RANK = 256
MLA_DIM = MLA_HEADS * MLA_V
MLA_IN = MLA_Q_RANK + MLA_KV_RANK + MLA_ROPE
MLA_HEAD_PAD = 128
MLA_C_PAD = 768
MLA_FREQS = MLA_ROPE // 2
MLA_VT_ROWS = MLA_V + 16

TAB_RET_COS = 0
TAB_RET_SIN = RET_FREQS
TAB_MLA_COS = 2 * RET_FREQS
TAB_MLA_SIN = 2 * RET_FREQS + MLA_FREQS

V7X_VMEM_LIMIT_BYTES = 56 * 1024 * 1024


def _cparams(semantics):
    return pltpu.CompilerParams(dimension_semantics=semantics, vmem_limit_bytes=V7X_VMEM_LIMIT_BYTES)


def _dot(a, b):
    return jnp.dot(a, b, preferred_element_type=F32)


def _dot_nt(a, b):
    return lax.dot_general(a, b, (((1,), (1,)), ((), ())), preferred_element_type=F32)


def _dot_tn(a, b):
    return lax.dot_general(a, b, (((0,), (0,)), ((), ())), preferred_element_type=F32)


def _split_dot(x, m_bf16, passes):
    acc = None
    rem = x
    for _ in range(passes):
        piece = rem.astype(BF16)
        part = _dot(piece, m_bf16)
        acc = part if acc is None else acc + part
        rem = rem - piece.astype(F32)
    return acc


def _split_dot_left(m_bf16, x):
    acc = None
    rem = x
    for _ in range(3):
        piece = rem.astype(BF16)
        part = _dot(m_bf16, piece)
        acc = part if acc is None else acc + part
        rem = rem - piece.astype(F32)
    return acc


def _rms(x, g):
    return x * lax.rsqrt(jnp.mean(x * x, axis=-1, keepdims=True) + NORM_EPS) * g


def _layer_spec(shape, layer, ngrid):
    zeros = (0,) * len(shape)
    if ngrid == 1:
        return pl.BlockSpec((None,) + tuple(shape), lambda i: (layer,) + zeros)
    return pl.BlockSpec((None,) + tuple(shape), lambda i, j: (layer,) + zeros)


def _const_spec(a, ngrid):
    zeros = (0,) * a.ndim
    if ngrid == 1:
        return pl.BlockSpec(a.shape, lambda i: zeros)
    return pl.BlockSpec(a.shape, lambda i, j: zeros)


def _inproj_kernel(x_ref, g_ref, wa_ref, wb_ref, wc_ref, pa_ref, pb_ref, pc_ref):
    hn = _rms(x_ref[...], g_ref[...]).astype(BF16)
    pa_ref[...] = _dot(hn, wa_ref[...])
    pb_ref[...] = _dot(hn, wb_ref[...])
    pc_ref[...] = _dot(hn, wc_ref[...])


def _inproj(x, g, wa, wb, wc, layer, tm):
    t, d = x.shape
    row = lambda n: pl.BlockSpec((tm, n), lambda i: (i, 0))
    ws = (wa, wb, wc)
    return pl.pallas_call(
        _inproj_kernel,
        grid=(t // tm,),
        in_specs=[row(d), _layer_spec(g.shape[1:], layer, 1)] + [_layer_spec(w.shape[1:], layer, 1) for w in ws],
        out_specs=[row(w.shape[2]) for w in ws],
        out_shape=[jax.ShapeDtypeStruct((t, w.shape[2]), F32) for w in ws],
        compiler_params=_cparams(("parallel",)),
        name="inproj",
    )(x, g, wa, wb, wc)


_VEC_ROWS = 16
(_V_W0, _V_A0, _V_KK, _V_KA, _V_RK, _V_GNW, _V_GNB, _V_V0) = range(8)


def _rwkv_kernel(*refs, has_vres, nb, tb):
    if has_vres:
        (p_ref, vf_ref, mu_ref, vec_ref, w2_ref, a2_ref, g2_ref, v1_ref, v2_ref, bd_ref, tri_ref,
         o_ref, carry_ref, s_ref, r_s, k_s, v_s, lw_s, kk_s, ka_s, g_s, bon_s, y_s) = refs
    else:
        (p_ref, mu_ref, vec_ref, w2_ref, a2_ref, g2_ref, bd_ref, tri_ref,
         o_ref, vfo_ref, carry_ref, s_ref, r_s, k_s, v_s, lw_s, kk_s, ka_s, g_s, bon_s, y_s) = refs

    L = RW_CHUNK
    C = RW_DIM
    tstep = pl.program_id(1)

    @pl.when(tstep == 0)
    def _():
        carry_ref[...] = jnp.zeros_like(carry_ref)
        s_ref[...] = jnp.zeros_like(s_ref)

    vec = vec_ref[...]
    row = lambda i: vec[i:i + 1, :]
    bd = bd_ref[...]
    mu = mu_ref[...]

    def bdsum(x):
        return _split_dot(x, bd, 2)

    def prep(b, _):
        p = p_ref[b]
        shifted = pltpu.roll(p, 1, 0)
        first = lax.broadcasted_iota(jnp.int32, p.shape, 0) == 0
        p_prev = jnp.where(first, carry_ref[b], shifted)
        carry_ref[b] = p[tb - 1:tb, :]
        ps = p + (p_prev - p) * mu
        r = ps[:, 0:C]
        k = ps[:, C:2 * C]
        v = ps[:, 2 * C:3 * C]
        lr = ps[:, 3 * C:3 * C + RW_LOWRANK]
        z = row(_V_W0) + _dot(jnp.tanh(lr).astype(BF16), w2_ref[...])
        lw = -math.exp(-0.5) * jax.nn.sigmoid(z)
        a = jax.nn.sigmoid(row(_V_A0) + _dot(lr.astype(BF16), a2_ref[...]))
        g = _dot(jax.nn.sigmoid(lr).astype(BF16), g2_ref[...])
        if has_vres:
            lat = _dot(v.astype(BF16), v1_ref[...]).astype(BF16)
            v = v + (vf_ref[b] - v) * jax.nn.sigmoid(row(_V_V0) + _dot(lat, v2_ref[...]))
        else:
            vfo_ref[b] = v
        kk = k * row(_V_KK)
        kk = kk * lax.rsqrt(jnp.maximum(bdsum(kk * kk), 1e-24))
        k2 = k * (1.0 + (a - 1.0) * row(_V_KA))
        r_s[b] = r
        k_s[b] = k2
        v_s[b] = v
        lw_s[b] = lw
        kk_s[b] = kk
        ka_s[b] = kk * a
        g_s[b] = g
        bon_s[b] = bdsum(r * k2 * row(_V_RK)) * v
        return 0

    lax.fori_loop(0, nb, prep, 0)

    hl = RW_HEADS * L
    r_i = lax.broadcasted_iota(jnp.int32, (hl, C), 0)
    c_i = lax.broadcasted_iota(jnp.int32, (hl, C), 1)
    headmask = (r_i // L) == (c_i // RW_HEAD_DIM)
    t_i = lax.broadcasted_iota(jnp.int32, (L, hl), 0)
    s_i = lax.broadcasted_iota(jnp.int32, (L, hl), 1) % L
    strict = t_i > s_i
    incl = t_i >= s_i
    tri = tri_ref[...]

    def masked4(x):
        return jnp.where(headmask, jnp.concatenate([x] * RW_HEADS, axis=0), 0.0).astype(BF16)

    def chunk(j, _):
        sl = pl.ds(pl.multiple_of(j * L, L), L)
        bs = range(nb)
        r = [r_s[b, sl, :] for b in bs]
        k2 = [k_s[b, sl, :] for b in bs]
        v = [v_s[b, sl, :] for b in bs]
        lw = [lw_s[b, sl, :] for b in bs]
        kk = [kk_s[b, sl, :] for b in bs]
        ka = [ka_s[b, sl, :] for b in bs]
        c = [_split_dot_left(tri, lw[b]) for b in bs]
        c_last = [c[b][L - 1:L, :] for b in bs]
        sig = [0.5 * c_last[b] for b in bs]
        e_neg = [jnp.exp(sig[b] - c[b]) for b in bs]
        kt = [k2[b] * e_neg[b] for b in bs]
        bt = [ka[b] * e_neg[b] for b in bs]
        x = [jnp.concatenate([-(kk[b] * jnp.exp(c[b] - lw[b] - sig[b])), r[b] * jnp.exp(c[b] - sig[b])],
                             axis=0).astype(BF16) for b in bs]
        kb = [jnp.concatenate([masked4(kt[b]), masked4(bt[b])], axis=0) for b in bs]
        ab = [_dot_nt(x[b], kb[b]) for b in bs]
        s_old = [s_ref[b] for b in bs]
        xs = [_dot_nt(x[b], (s_old[b] * jnp.exp(sig[b])).astype(BF16)) for b in bs]
        a_k = [jnp.concatenate([jnp.where(strict, ab[b][0:L, 0:hl], 0.0), jnp.where(incl, ab[b][L:2 * L, 0:hl], 0.0)],
                               axis=0).astype(BF16) for b in bs]
        n_w = [jnp.where(strict, ab[b][0:L, hl:2 * hl], 0.0) for b in bs]
        a_rb = [jnp.where(incl, ab[b][L:2 * L, hl:2 * hl], 0.0).astype(BF16) for b in bs]
        av = [_dot(a_k[b], masked4(v[b])) for b in bs]
        u = [av[b][0:L] + xs[b][0:L] for b in bs]
        for i in range(6):
            n_b = [n_w[b].astype(BF16) for b in bs]
            u = [u[b] + _dot(n_b[b], masked4(u[b])) for b in bs]
            if i < 5:
                n_w = [_dot(n_b[b], masked4(n_w[b])) for b in bs]
        y = [av[b][L:2 * L] + xs[b][L:2 * L] + _dot(a_rb[b], masked4(u[b])) for b in bs]
        for b in bs:
            y_s[b, sl, :] = y[b]
        scale = [jnp.exp(c_last[b] - sig[b]) for b in bs]
        kbs = [jnp.concatenate([kt[b] * scale[b], bt[b] * scale[b]], axis=0).astype(BF16) for b in bs]
        vu = [jnp.concatenate([v[b], u[b]], axis=0).astype(BF16) for b in bs]
        upd = [_dot_tn(vu[b], kbs[b]) for b in bs]
        for b in bs:
            s_ref[b] = s_old[b] * jnp.exp(c_last[b]) + jnp.where(headmask, upd[b], 0.0)
        return 0

    lax.fori_loop(0, tb // L, chunk, 0)

    def post(b, _):
        y = y_s[b]
        mean = bdsum(y) * (1.0 / RW_HEAD_DIM)
        d = y - mean
        var = bdsum(d * d) * (1.0 / RW_HEAD_DIM)
        yn = d * lax.rsqrt(var + RW_GN_EPS) * row(_V_GNW) + row(_V_GNB)
        o_ref[b] = ((yn + bon_s[b]) * g_s[b]).astype(o_ref.dtype)
        return 0

    lax.fori_loop(0, nb, post, 0)


def _block_ones(n, blk):
    i = np.arange(n)
    return (i[:, None] // blk == i[None, :] // blk).astype(np.float32)


def _rwkv(p_a, v_first, mu, vec, w2p, a2p, g2p, v1p, v2p, layer, nb, tb):
    bsz, s, _ = p_a.shape
    has_vres = v_first is not None
    bd = jnp.asarray(_block_ones(RW_DIM, RW_HEAD_DIM), BF16)
    tri = jnp.asarray(np.tril(np.ones((RW_CHUNK, RW_CHUNK), np.float32)), BF16)
    blk = lambda n: pl.BlockSpec((nb, tb, n), lambda i, j: (i, j, 0))
    lay = lambda a, l: _layer_spec(a.shape[1:], l, 2)
    common = [lay(mu, layer), lay(vec, layer), lay(w2p, layer), lay(a2p, layer), lay(g2p, layer)]
    consts = [_const_spec(bd, 2), _const_spec(tri, 2)]
    if has_vres:
        args = (p_a, v_first, mu, vec, w2p, a2p, g2p, v1p, v2p, bd, tri)
        in_specs = [blk(RW_IN), blk(RW_DIM)] + common + [lay(v1p, layer - 1), lay(v2p, layer - 1)] + consts
        out_specs = blk(RW_DIM)
        out_shape = jax.ShapeDtypeStruct((bsz, s, RW_DIM), BF16)
    else:
        args = (p_a, mu, vec, w2p, a2p, g2p, bd, tri)
        in_specs = [blk(RW_IN)] + common + consts
        out_specs = [blk(RW_DIM), blk(RW_DIM)]
        out_shape = [jax.ShapeDtypeStruct((bsz, s, RW_DIM), BF16), jax.ShapeDtypeStruct((bsz, s, RW_DIM), F32)]
    big = pltpu.VMEM((nb, tb, RW_DIM), F32)
    scratch = [pltpu.VMEM((nb, 1, RW_IN), F32), pltpu.VMEM((nb, RW_DIM, RW_DIM), F32)] + [big] * 9
    return pl.pallas_call(
        functools.partial(_rwkv_kernel, has_vres=has_vres, nb=nb, tb=tb),
        grid=(bsz // nb, s // tb),
        in_specs=in_specs,
        out_specs=out_specs,
        out_shape=out_shape,
        scratch_shapes=scratch,
        compiler_params=_cparams(("parallel", "arbitrary")),
        name="rwkv7",
    )(*args)


def _ret_tables():
    h, d, c = RET_HEADS, RET_HEAD_DIM, RET_CHUNK
    log_gamma = np.log(1.0 - 2.0 ** (-5.0 - np.arange(h, dtype=np.float64)))
    idx = np.arange(c, dtype=np.float64)
    diff = idx[:, None] - idx[None, :]
    dmask = np.where(diff >= 0, np.exp(log_gamma[:, None, None] * np.maximum(diff, 0.0)), 0.0)
    lane = np.arange(RET_DIM)
    head_qk = (lane % (RET_DIM // 2)) // (d // 2)
    head_v = lane // d
    qdec = np.exp(log_gamma[head_qk][None, :] * (idx[:, None] + 1.0))
    kdec = np.exp(log_gamma[head_qk][None, :] * (c - 1.0 - idx[:, None])) * d ** -0.5
    hm_qk = (np.arange(h)[:, None, None] == head_qk[None, None, :]) * np.ones((1, c, 1))
    hm_v = (np.arange(h)[:, None, None] == head_v[None, None, :]) * np.ones((1, c, 1))
    block = head_qk[:, None] == head_v[None, :]
    rdec = np.where(block, np.exp(log_gamma[head_qk] * c)[:, None], 0.0)
    pcos = np.zeros((LANES, RET_DIM // 2), np.float32)
    psin = np.zeros((LANES, RET_DIM // 2), np.float32)
    for hh in range(h):
        for j in range(RET_FREQS):
            pcos[TAB_RET_COS + j, hh * RET_FREQS + j] = 1.0
            psin[TAB_RET_SIN + j, hh * RET_FREQS + j] = 1.0
    f = lambda a: jnp.asarray(a, F32)
    return (f(dmask.reshape(h * c, c) * d ** -0.5), f(qdec), f(kdec), f(hm_qk.reshape(h * c, RET_DIM)),
            f(hm_v.reshape(h * c, RET_DIM)), f(block), f(rdec),
            jnp.asarray(np.concatenate([pcos, psin], axis=1), BF16))


def _ret_kernel(p_ref, tab_ref, dmask_ref, qdec_ref, kdec_ref, hmqk_ref, hmv_ref, block_ref, rdec_ref, pcs_ref,
                bd_ref, o_ref, r_ref, *, nb):
    c, dm = RET_CHUNK, RET_DIM
    half = dm // 2

    @pl.when(pl.program_id(0) == 0)
    def _():
        r_ref[...] = jnp.zeros_like(r_ref)

    bs = range(nb)
    cs = [_split_dot(tab_ref[b], pcs_ref[...], 3) for b in bs]
    p = [p_ref[b] for b in bs]

    def rope(x, t):
        cos, sin = t[:, :half], t[:, half:]
        x1, x2 = x[:, :half], x[:, half:]
        return jnp.concatenate([x1 * cos - x2 * sin, x2 * cos + x1 * sin], axis=1)

    q = [rope(p[b][:, 0:dm], cs[b]) for b in bs]
    k = [rope(p[b][:, dm:2 * dm], cs[b]) for b in bs]
    vb = [p[b][:, 2 * dm:3 * dm].astype(BF16) for b in bs]
    q_bd = [(jnp.concatenate([q[b]] * RET_HEADS, axis=0) * hmqk_ref[...]).astype(BF16) for b in bs]
    scores = [(_dot_nt(q_bd[b], k[b].astype(BF16)) * dmask_ref[...]).astype(BF16) for b in bs]
    o = [_dot(scores[b], vb[b]) * hmv_ref[...] for b in bs]
    r_old = [r_ref[b] for b in bs]
    cross = [_dot((q[b] * qdec_ref[...]).astype(BF16), r_old[b].astype(BF16)) for b in bs]
    upd = [_dot_tn((k[b] * kdec_ref[...]).astype(BF16), vb[b]) for b in bs]
    for b in bs:
        r_ref[b] = r_old[b] * rdec_ref[...] + upd[b] * block_ref[...]
    y = [o[b][0:c] + o[b][c:2 * c] + o[b][2 * c:3 * c] + o[b][3 * c:4 * c] + cross[b] for b in bs]
    ms = [_split_dot(y[b] * y[b], bd_ref[...], 2) * (1.0 / RET_HEAD_DIM) for b in bs]
    for b in bs:
        g = p[b][:, 3 * dm:4 * dm]
        o_ref[b] = (g * jax.nn.sigmoid(g) * y[b] * lax.rsqrt(ms[b] + NORM_EPS)).astype(o_ref.dtype)


def _retention(p_b, tab, nb):
    bsz, s, _ = p_b.shape
    consts = _ret_tables() + (jnp.asarray(_block_ones(RET_DIM, RET_HEAD_DIM), BF16),)
    blk = lambda n: pl.BlockSpec((nb, RET_CHUNK, n), lambda j, i: (i, j, 0))
    full = lambda a: pl.BlockSpec(a.shape, lambda j, i: (0,) * a.ndim)
    assert bsz == nb, "one batch group per time block keeps the state scratch simple"
    return pl.pallas_call(
        functools.partial(_ret_kernel, nb=nb),
        grid=(s // RET_CHUNK, bsz // nb),
        in_specs=[blk(RET_IN), blk(LANES)] + [full(a) for a in consts],
        out_specs=blk(RET_DIM),
        out_shape=jax.ShapeDtypeStruct((bsz, s, RET_DIM), BF16),
        scratch_shapes=[pltpu.VMEM((nb, RET_DIM, RET_DIM), F32)],
        compiler_params=_cparams(("arbitrary", "arbitrary")),
        name="retention",
    )(p_b, tab, *consts)


def _mla_tables():
    pk = np.zeros((LANES, MLA_HEAD_PAD), np.float32)
    pq = np.zeros((2 * MLA_FREQS, LANES), np.float32)
    for j in range(MLA_FREQS):
        for rep in range(2):
            pk[TAB_MLA_COS + j, rep * MLA_FREQS + j] = 1.0
            pk[TAB_MLA_SIN + j, MLA_ROPE + rep * MLA_FREQS + j] = 1.0
        pq[j, TAB_MLA_COS + j] = 1.0
        pq[MLA_FREQS + j, TAB_MLA_SIN + j] = 1.0
    place = np.zeros((MLA_HEAD_PAD, MLA_HEADS * MLA_HEAD_PAD), np.float32)
    for h in range(MLA_HEADS):
        for j in range(MLA_ROPE):
            place[j, h * MLA_HEAD_PAD + MLA_NOPE + j] = 1.0
            place[MLA_ROPE + j, h * MLA_HEAD_PAD + MLA_NOPE + j] = 1.0
    return jnp.asarray(pk, BF16), jnp.asarray(pq, BF16), jnp.asarray(place, BF16)


def _split_dot_nt(m_bf16, x, passes):
    acc = None
    rem = x
    for _ in range(passes):
        piece = rem.astype(BF16)
        part = _dot_nt(m_bf16, piece)
        acc = part if acc is None else acc + part
        rem = rem - piece.astype(F32)
    return acc


def _mla_proj_kernel(pc_ref, tab_ref, qn_ref, kn_ref, wqt_ref, wk_ref, wvt_ref, pk_ref, pq_ref, place_ref,
                     qt_ref, k_ref, vt_ref):
    pc = pc_ref[...]
    tab = tab_ref[0]
    ck = _split_dot(tab, pk_ref[...], 3)
    cs = _split_dot_nt(pq_ref[...], tab, 3)
    cos, sin = cs[0:MLA_FREQS], cs[MLA_FREQS:2 * MLA_FREQS]
    nq = _rms(pc[:, 0:MLA_Q_RANK], qn_ref[...]).astype(BF16)
    nkv = _rms(pc[:, MLA_Q_RANK:MLA_Q_RANK + MLA_KV_RANK], kn_ref[...]).astype(BF16)
    kr = (pc[:, MLA_Q_RANK + MLA_KV_RANK:MLA_C_PAD] * ck).astype(BF16)
    qa = _dot_nt(wqt_ref[...], nq) * ((MLA_NOPE + MLA_ROPE) ** -0.5 * math.log2(math.e))
    zeros = jnp.zeros((MLA_HEAD_PAD - MLA_NOPE - MLA_ROPE, qa.shape[1]), qt_ref.dtype)
    for h in range(MLA_HEADS):
        b0 = h * MLA_HEAD_PAD
        b1, b2, b3 = b0 + MLA_NOPE, b0 + MLA_NOPE + MLA_FREQS, b0 + MLA_NOPE + MLA_ROPE
        x1, x2 = qa[b1:b2], qa[b2:b3]
        qt_ref[0, b0:b1, :] = qa[b0:b1].astype(qt_ref.dtype)
        qt_ref[0, b1:b2, :] = (x1 * cos - x2 * sin).astype(qt_ref.dtype)
        qt_ref[0, b2:b3, :] = (x2 * cos + x1 * sin).astype(qt_ref.dtype)
        qt_ref[0, b3:b0 + MLA_HEAD_PAD, :] = zeros
    k_ref[...] = (_dot(nkv, wk_ref[...]) + _dot(kr, place_ref[...])).astype(k_ref.dtype)
    vt = _dot_nt(wvt_ref[...], nkv).astype(vt_ref.dtype)
    ones = jnp.ones((MLA_VT_ROWS - MLA_V, vt.shape[1]), vt_ref.dtype)
    for h in range(MLA_HEADS):
        vt_ref[0, h * MLA_VT_ROWS:h * MLA_VT_ROWS + MLA_V, :] = vt[h * MLA_V:(h + 1) * MLA_V, :]
        vt_ref[0, h * MLA_VT_ROWS + MLA_V:(h + 1) * MLA_VT_ROWS, :] = ones


def _mla_proj(p_c, tab, qn, kn, wqt, wk, wvt, layer, tm):
    t = p_c.shape[0]
    bsz, s, _ = tab.shape
    per = s // tm
    row = lambda n: pl.BlockSpec((tm, n), lambda i: (i, 0))
    cols = lambda n: pl.BlockSpec((1, n, tm), lambda i: (i // per, 0, i % per))
    pk, pq, place = _mla_tables()
    hp = MLA_HEADS * MLA_HEAD_PAD
    lay = lambda a: _layer_spec(a.shape[1:], layer, 1)
    return pl.pallas_call(
        _mla_proj_kernel,
        grid=(t // tm,),
        in_specs=[row(MLA_C_PAD), pl.BlockSpec((1, tm, LANES), lambda i: (i // per, i % per, 0)),
                  lay(qn), lay(kn), lay(wqt), lay(wk), lay(wvt),
                  _const_spec(pk, 1), _const_spec(pq, 1), _const_spec(place, 1)],
        out_specs=[cols(hp), row(hp), cols(MLA_HEADS * MLA_VT_ROWS)],
        out_shape=[jax.ShapeDtypeStruct((bsz, hp, s), BF16), jax.ShapeDtypeStruct((t, hp), BF16),
                   jax.ShapeDtypeStruct((bsz, MLA_HEADS * MLA_VT_ROWS, s), BF16)],
        compiler_params=_cparams(("parallel",)),
        name="mla_proj",
    )(p_c, tab, qn, kn, wqt, wk, wvt, pk, pq, place)


_NEG = -1e30
FLASH_ISSUE_AHEAD = 3


def _flash_kernel(qi_ref, ki_ref, qt_ref, k_ref, vt_ref, o_ref, m_ref, l_ref, acc_ref, *, tq, tk):
    pair = pl.program_id(1)
    qi = qi_ref[pair]
    ki = ki_ref[pair]

    @pl.when(ki == 0)
    def _():
        m_ref[...] = jnp.full_like(m_ref, _NEG)
        l_ref[...] = jnp.zeros_like(l_ref)
        acc_ref[...] = jnp.zeros_like(acc_ref)

    def step(masked):
        if masked:
            kv_i = lax.broadcasted_iota(jnp.int32, (tk, tq), 0)
            q_i = lax.broadcasted_iota(jnp.int32, (tk, tq), 1)
            keep = kv_i <= q_i
        def scores(h):
            hs = slice(h * MLA_HEAD_PAD, (h + 1) * MLA_HEAD_PAD)
            return _dot(k_ref[0, :, hs], qt_ref[0, hs, :])

        ahead = [scores(h) for h in range(FLASH_ISSUE_AHEAD)]
        for h in range(MLA_HEADS):
            vs = slice(h * MLA_V, (h + 1) * MLA_V)
            st = ahead.pop(0)
            if h + FLASH_ISSUE_AHEAD < MLA_HEADS:
                ahead.append(scores(h + FLASH_ISSUE_AHEAD))
            if masked:
                st = jnp.where(keep, st, _NEG)
            m_old = m_ref[h:h + 1, :]
            m_new = jnp.maximum(m_old, jnp.max(st, axis=0, keepdims=True))
            alpha = jnp.exp2(m_old - m_new)
            p = jnp.exp2(st - m_new).astype(BF16)
            pv = _dot(vt_ref[0, h * MLA_VT_ROWS:(h + 1) * MLA_VT_ROWS, :], p)
            l_ref[h:h + 1, :] = alpha * l_ref[h:h + 1, :] + pv[MLA_V:MLA_V + 1, :]
            m_ref[h:h + 1, :] = m_new
            acc_ref[vs, :] = acc_ref[vs, :] * alpha + pv[0:MLA_V, :]

    @pl.when(ki < qi)
    def _():
        step(False)

    @pl.when(ki == qi)
    def _():
        step(True)
        for h in range(MLA_HEADS):
            vs = slice(h * MLA_V, (h + 1) * MLA_V)
            o_ref[0, vs, :] = (acc_ref[vs, :] / l_ref[h:h + 1, :]).astype(o_ref.dtype)


def _flash(qt, k, vt, tq):
    bsz, s, hp = k.shape
    tk = tq
    nq = s // tq
    pairs = [(i, j) for i in range(nq) for j in range(i + 1)]
    qi_tab = jnp.asarray(np.asarray([p[0] for p in pairs], np.int32))
    ki_tab = jnp.asarray(np.asarray([p[1] for p in pairs], np.int32))
    grid_spec = pltpu.PrefetchScalarGridSpec(
        num_scalar_prefetch=2,
        grid=(bsz, len(pairs)),
        in_specs=[pl.BlockSpec((1, hp, tq), lambda b, p, qi, ki: (b, 0, qi[p])),
                  pl.BlockSpec((1, tk, hp), lambda b, p, qi, ki: (b, ki[p], 0)),
                  pl.BlockSpec((1, MLA_HEADS * MLA_VT_ROWS, tk), lambda b, p, qi, ki: (b, 0, ki[p]))],
        out_specs=pl.BlockSpec((1, MLA_DIM, tq), lambda b, p, qi, ki: (b, 0, qi[p])),
        scratch_shapes=[pltpu.VMEM((MLA_HEADS, tq), F32), pltpu.VMEM((MLA_HEADS, tq), F32),
                        pltpu.VMEM((MLA_DIM, tq), F32)],
    )
    return pl.pallas_call(
        functools.partial(_flash_kernel, tq=tq, tk=tk),
        grid_spec=grid_spec,
        out_shape=jax.ShapeDtypeStruct((bsz, MLA_DIM, s), BF16),
        compiler_params=_cparams(("parallel", "arbitrary")),
        name="mla_flash",
    )(qi_tab, ki_tab, qt, k, vt)


FFN_SUB_ROWS = 256


def _ffn_kernel(h_ref, ya_ref, yb_ref, yct_ref, woa_ref, wob_ref, woc_ref, fg_ref, wg_ref, wu_ref, wd_ref, *rest,
                final, tm):
    if final:
        fin_ref, o_ref, h1_ref, hn_ref = rest
    else:
        o_ref, h1_ref, hn_ref = rest
    h1 = (h_ref[...] + _dot(ya_ref[...], woa_ref[...]) + _dot(yb_ref[...], wob_ref[...])
          + _dot_tn(yct_ref[0], woc_ref[...]))
    h1_ref[...] = h1
    hn_ref[...] = _rms(h1, fg_ref[...]).astype(BF16)

    def gate_up(r):
        x = hn_ref[r * FFN_SUB_ROWS:(r + 1) * FFN_SUB_ROWS, :]
        return _dot(x, wg_ref[...]), _dot(x, wu_ref[...])

    nsub = tm // FFN_SUB_ROWS
    nxt = gate_up(0)
    for r in range(nsub):
        rows = slice(r * FFN_SUB_ROWS, (r + 1) * FFN_SUB_ROWS)
        gate, up = nxt
        if r + 1 < nsub:
            nxt = gate_up(r + 1)
        act = (gate * jax.nn.sigmoid(gate) * up).astype(BF16)
        out = h1_ref[rows, :] + _dot(act, wd_ref[...])
        if final:
            out = _rms(out, fin_ref[...])
        o_ref[rows, :] = out


def _ffn(h, ya, yb, yct, wo, fg, wgu, wd, fin, layer, tm):
    t, d = h.shape
    dff = wd.shape[1]
    per = yct.shape[2] // tm
    row = lambda n: pl.BlockSpec((tm, n), lambda i: (i, 0))
    once = pl.Buffered(1)
    final = fin is not None
    in_specs = [row(d), row(RW_DIM), row(RET_DIM),
                pl.BlockSpec((1, MLA_DIM, tm), lambda i: (i // per, 0, i % per)),
                pl.BlockSpec((None, RW_DIM, d), lambda i: (layer, 0, 0), pipeline_mode=once),
                pl.BlockSpec((None, RET_DIM, d), lambda i: (layer, 1, 0), pipeline_mode=once),
                pl.BlockSpec((None, MLA_DIM, d), lambda i: (layer, 1, 0), pipeline_mode=once),
                _layer_spec(fg.shape[1:], layer, 1),
                pl.BlockSpec((None, d, dff), lambda i: (layer, 0, 0), pipeline_mode=once),
                pl.BlockSpec((None, d, dff), lambda i: (layer, 0, 1), pipeline_mode=once),
                pl.BlockSpec((None, dff, d), lambda i: (layer, 0, 0), pipeline_mode=once)]
    args = [h, ya, yb, yct, wo, wo, wo, fg, wgu, wgu, wd]
    if final:
        in_specs.append(_const_spec(fin, 1))
        args.append(fin)
    return pl.pallas_call(
        functools.partial(_ffn_kernel, final=final, tm=tm),
        grid=(t // tm,),
        in_specs=in_specs,
        out_specs=row(d),
        out_shape=jax.ShapeDtypeStruct((t, d), F32),
        scratch_shapes=[pltpu.VMEM((tm, d), F32), pltpu.VMEM((tm, d), BF16)],
        compiler_params=_cparams(("parallel",)),
        name="outproj_ffn",
    )(*args)


def _relayout(w, m, transpose=False):
    out = "lki" if transpose else "lik"
    return jnp.einsum("lij,jk->" + out, w.astype(BF16), jnp.asarray(m, BF16), preferred_element_type=BF16)


def _rot_half_into(m, src, dst, n, sign=1.0):
    for j in range(n // 2):
        m[src + n // 2 + j, dst + j] = -sign
        m[src + j, dst + n // 2 + j] = sign


def _prep_in_weights(w_in):
    perm = np.zeros((2 * RET_DIM, 2 * RET_DIM), np.float32)
    for part in range(2):
        for hh in range(RET_HEADS):
            for half in range(2):
                for j in range(RET_FREQS):
                    src = part * RET_DIM + hh * RET_HEAD_DIM + half * RET_FREQS + j
                    dst = part * RET_DIM + half * (RET_DIM // 2) + hh * RET_FREQS + j
                    perm[src, dst] = 1.0
    mc = np.zeros((MLA_IN, MLA_C_PAD), np.float32)
    mc[np.arange(MLA_IN), np.arange(MLA_IN)] = 1.0
    _rot_half_into(mc, MLA_Q_RANK + MLA_KV_RANK, MLA_IN, MLA_ROPE)
    b0 = RW_IN
    c0 = RW_IN + RET_IN
    wa = w_in[..., :b0].astype(BF16)
    wb = jnp.concatenate([_relayout(w_in[..., b0:b0 + 2 * RET_DIM], perm), w_in[..., b0 + 2 * RET_DIM:c0].astype(BF16)],
                         axis=-1)
    wc = _relayout(w_in[..., c0:], mc)
    return wa, wb, wc


def _prep_mla_weights(w_q_up, w_kv_up):
    qd = MLA_NOPE + MLA_ROPE
    kvd = MLA_NOPE + MLA_V
    hp = MLA_HEADS * MLA_HEAD_PAD
    mq = np.zeros((MLA_HEADS * qd, hp), np.float32)
    mk = np.zeros((MLA_HEADS * kvd, hp), np.float32)
    mv = np.zeros((MLA_HEADS * kvd, MLA_DIM), np.float32)
    for h in range(MLA_HEADS):
        for c in range(qd):
            mq[h * qd + c, h * MLA_HEAD_PAD + c] = 1.0
        for c in range(MLA_NOPE):
            mk[h * kvd + c, h * MLA_HEAD_PAD + c] = 1.0
        for c in range(MLA_V):
            mv[h * kvd + MLA_NOPE + c, h * MLA_V + c] = 1.0
    return (_relayout(w_q_up, mq, transpose=True), _relayout(w_kv_up, mk),
            _relayout(w_kv_up, mv, transpose=True))


def _pad_rows(w, start, total):
    return jnp.pad(w, ((0, 0), (start, total - start - w.shape[1]), (0, 0)))


def _rope_table(positions):
    inv_ret = ROPE_BASE ** (-jnp.arange(0, RET_HEAD_DIM, 2, dtype=F32) / RET_HEAD_DIM)
    inv_mla = ROPE_BASE ** (-jnp.arange(0, MLA_ROPE, 2, dtype=F32) / MLA_ROPE)
    inv = jnp.concatenate([inv_ret, inv_ret, inv_mla, inv_mla, jnp.zeros((LANES - TAB_MLA_SIN - MLA_FREQS,), F32)])
    lane = np.arange(LANES)
    is_cos = (lane < TAB_RET_SIN) | ((lane >= TAB_MLA_COS) & (lane < TAB_MLA_SIN))
    ang = positions.astype(F32)[..., None] * inv
    return jnp.where(jnp.asarray(is_cos), jnp.cos(ang), jnp.sin(ang))


def kernel(x, positions, attn_norm, w_in, w_out, rw_mu, rw_w0, rw_w2, rw_a0, rw_a2, rw_g2, rw_k_k, rw_k_a, rw_r_k, rw_gn_w, rw_gn_b, rw_v0, rw_v1, rw_v2, mla_q_norm, mla_kv_norm, mla_w_q_up, mla_w_kv_up, ffn_norm, w_gate_up, w_down, final_norm):
    bsz, s, d = x.shape
    depth = w_in.shape[0]
    t = bsz * s
    tm = min(512, s)
    hp = MLA_HEADS * MLA_HEAD_PAD

    tab = _rope_table(positions)
    wa, wb, wc = _prep_in_weights(w_in)
    wqt, wk, wvt = _prep_mla_weights(mla_w_q_up, mla_w_kv_up)
    wo, wgu, wd = w_out.astype(BF16), w_gate_up.astype(BF16), w_down.astype(BF16)
    v0 = jnp.concatenate([jnp.zeros((1, RW_DIM), F32), rw_v0], axis=0)
    vec = jnp.stack([rw_w0, rw_a0, rw_k_k, rw_k_a, rw_r_k.reshape(depth, RW_DIM), rw_gn_w, rw_gn_b, v0], axis=1)
    vec = jnp.pad(vec, ((0, 0), (0, _VEC_ROWS - vec.shape[1]), (0, 0)))
    w2p = _pad_rows(rw_w2, 0, RW_LOWRANK).astype(BF16)
    a2p = _pad_rows(rw_a2, RW_DECAY_RANK, RW_LOWRANK).astype(BF16)
    g2p = _pad_rows(rw_g2, RW_DECAY_RANK + RW_A_RANK, RW_LOWRANK).astype(BF16)
    v1p = jnp.pad(rw_v1, ((0, 0), (0, 0), (0, LANES - RW_V_RANK))).astype(BF16)
    v2p = _pad_rows(rw_v2, 0, LANES).astype(BF16)
    row3 = lambda a: a[:, None, :]
    mu, an, qn, kn, fn = row3(rw_mu), row3(attn_norm), row3(mla_q_norm), row3(mla_kv_norm), row3(ffn_norm)

    h = x.reshape(t, d)
    v_first = None
    for l in range(depth):
        p_a, p_b, p_c = _inproj(h, an, wa, wb, wc, l, tm)
        res = _rwkv(p_a.reshape(bsz, s, RW_IN), v_first, mu, vec, w2p, a2p, g2p, v1p, v2p, l, bsz, min(256, s))
        if l == 0:
            y_a, v_first = res
        else:
            y_a = res
        y_b = _retention(p_b.reshape(bsz, s, RET_IN), tab, bsz)
        qt, k, vt = _mla_proj(p_c, tab, qn, kn, wqt, wk, wvt, l, tm)
        y_ct = _flash(qt, k.reshape(bsz, s, hp), vt, min(512, s))
        fin = final_norm[None, :] if l == depth - 1 else None
        h = _ffn(h, y_a.reshape(t, RW_DIM), y_b.reshape(t, RET_DIM), y_ct, wo, fn, wgu, wd, fin, l, tm)
    return h.reshape(bsz, s, d)
```

```python
import functools
import math

import jax
import jax.numpy as jnp
import numpy as np
from jax import lax
from jax.experimental import pallas as pl
from jax.experimental.pallas import tpu as pltpu

F32 = jnp.float32
BF16 = jnp.bfloat16

NORM_EPS = 1e-6
ROPE_BASE = 10000.0
LANES = 128

RW_HEADS = 4
RW_HEAD_DIM = 64
RW_DIM = RW_HEADS * RW_HEAD_DIM
RW_DECAY_RANK = 32
RW_A_RANK = 32
RW_V_RANK = 32
RW_GATE_RANK = 64
RW_GN_EPS = 64e-5
RW_IN = 3 * RW_DIM + RW_DECAY_RANK + RW_A_RANK + RW_GATE_RANK
RW_LOWRANK = RW_DECAY_RANK + RW_A_RANK + RW_GATE_RANK
RW_CHUNK = 64

RET_HEADS = 4
RET_HEAD_DIM = 64
RET_DIM = RET_HEADS * RET_HEAD_DIM
RET_CHUNK = 128
RET_IN = 4 * RET_DIM
RET_FREQS = RET_HEAD_DIM // 2

MLA_HEADS = 8
MLA_NOPE = 64
MLA_ROPE = 32
MLA_V = 64
MLA_Q_RANK = 384
MLA_KV_RANK = 256
MLA_DIM = MLA_HEADS * MLA_V
MLA_IN = MLA_Q_RANK + MLA_KV_RANK + MLA_ROPE
MLA_HEAD_PAD = 128
MLA_C_PAD = 768
MLA_FREQS = MLA_ROPE // 2
MLA_VT_ROWS = MLA_V + 16

TAB_RET_COS = 0
TAB_RET_SIN = RET_FREQS
TAB_MLA_COS = 2 * RET_FREQS
TAB_MLA_SIN = 2 * RET_FREQS + MLA_FREQS

V7X_VMEM_LIMIT_BYTES = 56 * 1024 * 1024


MM_SUB_ROWS = 256


def _cparams(semantics):
    return pltpu.CompilerParams(dimension_semantics=semantics, vmem_limit_bytes=V7X_VMEM_LIMIT_BYTES)


def _dot(a, b):
    return jnp.dot(a, b, preferred_element_type=F32)


def _dot_nt(a, b):
    return lax.dot_general(a, b, (((1,), (1,)), ((), ())), preferred_element_type=F32)


def _dot_tn(a, b):
    return lax.dot_general(a, b, (((0,), (0,)), ((), ())), preferred_element_type=F32)


def _split_dot(x, m_bf16, passes):
    acc = None
    rem = x
    for _ in range(passes):
        piece = rem.astype(BF16)
        part = _dot(piece, m_bf16)
        acc = part if acc is None else acc + part
        rem = rem - piece.astype(F32)
    return acc


def _split_dot_left(m_bf16, x):
    acc = None
    rem = x
    for _ in range(3):
        piece = rem.astype(BF16)
        part = _dot(m_bf16, piece)
        acc = part if acc is None else acc + part
        rem = rem - piece.astype(F32)
    return acc


def _rms(x, g):
    return x * lax.rsqrt(jnp.mean(x * x, axis=-1, keepdims=True) + NORM_EPS) * g


def _layer_spec(shape, layer, ngrid):
    zeros = (0,) * len(shape)
    if ngrid == 1:
        return pl.BlockSpec((None,) + tuple(shape), lambda i: (layer,) + zeros)
    return pl.BlockSpec((None,) + tuple(shape), lambda i, j: (layer,) + zeros)


def _const_spec(a, ngrid):
    zeros = (0,) * a.ndim
    if ngrid == 1:
        return pl.BlockSpec(a.shape, lambda i: zeros)
    return pl.BlockSpec(a.shape, lambda i, j: zeros)


def _inproj_kernel(x_ref, g_ref, wa_ref, wb_ref, wc_ref, pa_ref, pb_ref, pc_ref):
    for r in range(x_ref.shape[0] // MM_SUB_ROWS):
        rows = slice(r * MM_SUB_ROWS, (r + 1) * MM_SUB_ROWS)
        hn = _rms(x_ref[rows, :], g_ref[...]).astype(BF16)
        pa_ref[rows, :] = _dot(hn, wa_ref[...])
        pb_ref[rows, :] = _dot(hn, wb_ref[...])
        pc_ref[rows, :] = _dot(hn, wc_ref[...])


def _inproj(x, g, wa, wb, wc, layer, tm):
    t, d = x.shape
    row = lambda n: pl.BlockSpec((tm, n), lambda i: (i, 0))
    ws = (wa, wb, wc)
    return pl.pallas_call(
        _inproj_kernel,
        grid=(t // tm,),
        in_specs=[row(d), _layer_spec(g.shape[1:], layer, 1)] + [_layer_spec(w.shape[1:], layer, 1) for w in ws],
        out_specs=[row(w.shape[2]) for w in ws],
        out_shape=[jax.ShapeDtypeStruct((t, w.shape[2]), F32) for w in ws],
        compiler_params=_cparams(("parallel",)),
        name="inproj",
    )(x, g, wa, wb, wc)


_VEC_ROWS = 16
(_V_W0, _V_A0, _V_KK, _V_KA, _V_RK, _V_GNW, _V_GNB, _V_V0) = range(8)


def _rwkv_kernel(*refs, has_vres, nb, tb):
    if has_vres:
        (p_ref, vf_ref, mu_ref, vec_ref, w2_ref, a2_ref, g2_ref, v1_ref, v2_ref, bd_ref, tri_ref,
         o_ref, carry_ref, s_ref, r_s, k_s, v_s, lw_s, kk_s, ka_s, g_s, bon_s, y_s) = refs
    else:
        (p_ref, mu_ref, vec_ref, w2_ref, a2_ref, g2_ref, bd_ref, tri_ref,
         o_ref, vfo_ref, carry_ref, s_ref, r_s, k_s, v_s, lw_s, kk_s, ka_s, g_s, bon_s, y_s) = refs

    L = RW_CHUNK
    C = RW_DIM
    tstep = pl.program_id(1)

    @pl.when(tstep == 0)
    def _():
        carry_ref[...] = jnp.zeros_like(carry_ref)
        s_ref[...] = jnp.zeros_like(s_ref)

    vec = vec_ref[...]
    row = lambda i: vec[i:i + 1, :]
    bd = bd_ref[...]
    mu = mu_ref[...]

    def bdsum(x):
        return _split_dot(x, bd, 2)

    def prep(b, _):
        p = p_ref[b]
        shifted = pltpu.roll(p, 1, 0)
        first = lax.broadcasted_iota(jnp.int32, p.shape, 0) == 0
        p_prev = jnp.where(first, carry_ref[b], shifted)
        carry_ref[b] = p[tb - 1:tb, :]
        ps = p + (p_prev - p) * mu
        r = ps[:, 0:C]
        k = ps[:, C:2 * C]
        v = ps[:, 2 * C:3 * C]
        lr = ps[:, 3 * C:3 * C + RW_LOWRANK]
        z = row(_V_W0) + _dot(jnp.tanh(lr).astype(BF16), w2_ref[...])
        lw = -math.exp(-0.5) * jax.nn.sigmoid(z)
        a = jax.nn.sigmoid(row(_V_A0) + _dot(lr.astype(BF16), a2_ref[...]))
        g = _dot(jax.nn.sigmoid(lr).astype(BF16), g2_ref[...])
        if has_vres:
            lat = _dot(v.astype(BF16), v1_ref[...]).astype(BF16)
            v = v + (vf_ref[b] - v) * jax.nn.sigmoid(row(_V_V0) + _dot(lat, v2_ref[...]))
        else:
            vfo_ref[b] = v
        kk = k * row(_V_KK)
        kk = kk * lax.rsqrt(jnp.maximum(bdsum(kk * kk), 1e-24))
        k2 = k * (1.0 + (a - 1.0) * row(_V_KA))
        r_s[b] = r
        k_s[b] = k2
        v_s[b] = v
        lw_s[b] = lw
        kk_s[b] = kk
        ka_s[b] = kk * a
        g_s[b] = g
        bon_s[b] = bdsum(r * k2 * row(_V_RK)) * v
        return 0

    lax.fori_loop(0, nb, prep, 0)

    hl = RW_HEADS * L
    r_i = lax.broadcasted_iota(jnp.int32, (hl, C), 0)
    c_i = lax.broadcasted_iota(jnp.int32, (hl, C), 1)
    headmask = (r_i // L) == (c_i // RW_HEAD_DIM)
    t_i = lax.broadcasted_iota(jnp.int32, (L, hl), 0)
    s_i = lax.broadcasted_iota(jnp.int32, (L, hl), 1) % L
    strict = t_i > s_i
    incl = t_i >= s_i
    tri = tri_ref[...]

    def masked4(x):
        return jnp.where(headmask, jnp.concatenate([x] * RW_HEADS, axis=0), 0.0).astype(BF16)

    def chunk(j, _):
        sl = pl.ds(pl.multiple_of(j * L, L), L)
        bs = range(nb)
        r = [r_s[b, sl, :] for b in bs]
        k2 = [k_s[b, sl, :] for b in bs]
        v = [v_s[b, sl, :] for b in bs]
        lw = [lw_s[b, sl, :] for b in bs]
        kk = [kk_s[b, sl, :] for b in bs]
        ka = [ka_s[b, sl, :] for b in bs]
        c = [_split_dot_left(tri, lw[b]) for b in bs]
        c_last = [c[b][L - 1:L, :] for b in bs]
        sig = [0.5 * c_last[b] for b in bs]
        e_neg = [jnp.exp(sig[b] - c[b]) for b in bs]
        kt = [k2[b] * e_neg[b] for b in bs]
        bt = [ka[b] * e_neg[b] for b in bs]
        x = [jnp.concatenate([-(kk[b] * jnp.exp(c[b] - lw[b] - sig[b])), r[b] * jnp.exp(c[b] - sig[b])],
                             axis=0).astype(BF16) for b in bs]
        s_old = [s_ref[b] for b in bs]
        kbs_w = [jnp.concatenate([masked4(kt[b]), masked4(bt[b]), (s_old[b] * jnp.exp(sig[b])).astype(BF16)], axis=0)
                 for b in bs]
        abx = [_dot_nt(x[b], kbs_w[b]) for b in bs]
        ab = [abx[b][:, 0:2 * hl] for b in bs]
        xs = [abx[b][:, 2 * hl:2 * hl + C] for b in bs]
        a_k = [jnp.concatenate([jnp.where(strict, ab[b][0:L, 0:hl], 0.0), jnp.where(incl, ab[b][L:2 * L, 0:hl], 0.0)],
                               axis=0).astype(BF16) for b in bs]
        n_w = [jnp.where(strict, ab[b][0:L, hl:2 * hl], 0.0) for b in bs]
        a_rb = [jnp.where(incl, ab[b][L:2 * L, hl:2 * hl], 0.0).astype(BF16) for b in bs]
        av = [_dot(a_k[b], masked4(v[b])) for b in bs]
        u = [av[b][0:L] + xs[b][0:L] for b in bs]
        for i in range(6):
            n_b = [n_w[b].astype(BF16) for b in bs]
            if i < 5:
                nu = [_dot(n_b[b], jnp.concatenate([masked4(u[b]), masked4(n_w[b])], axis=1)) for b in bs]
                u = [u[b] + nu[b][:, 0:C] for b in bs]
                n_w = [nu[b][:, C:2 * C] for b in bs]
            else:
                u = [u[b] + _dot(n_b[b], masked4(u[b])) for b in bs]
        y = [av[b][L:2 * L] + xs[b][L:2 * L] + _dot(a_rb[b], masked4(u[b])) for b in bs]
        for b in bs:
            y_s[b, sl, :] = y[b]
        scale = [jnp.exp(c_last[b] - sig[b]) for b in bs]
        kbs = [jnp.concatenate([kt[b] * scale[b], bt[b] * scale[b]], axis=0).astype(BF16) for b in bs]
        vu = [jnp.concatenate([v[b], u[b]], axis=0).astype(BF16) for b in bs]
        upd = [_dot_tn(vu[b], kbs[b]) for b in bs]
        for b in bs:
            s_ref[b] = s_old[b] * jnp.exp(c_last[b]) + jnp.where(headmask, upd[b], 0.0)
        return 0

    lax.fori_loop(0, tb // L, chunk, 0)

    def post(b, _):
        y = y_s[b]
        mean = bdsum(y) * (1.0 / RW_HEAD_DIM)
        d = y - mean
        var = bdsum(d * d) * (1.0 / RW_HEAD_DIM)
        yn = d * lax.rsqrt(var + RW_GN_EPS) * row(_V_GNW) + row(_V_GNB)
        o_ref[b] = ((yn + bon_s[b]) * g_s[b]).astype(o_ref.dtype)
        return 0

    lax.fori_loop(0, nb, post, 0)


def _block_ones(n, blk):
    i = np.arange(n)
    return (i[:, None] // blk == i[None, :] // blk).astype(np.float32)


def _rwkv(p_a, v_first, mu, vec, w2p, a2p, g2p, v1p, v2p, layer, nb, tb):
    bsz, s, _ = p_a.shape
    has_vres = v_first is not None
    bd = jnp.asarray(_block_ones(RW_DIM, RW_HEAD_DIM), BF16)
    tri = jnp.asarray(np.tril(np.ones((RW_CHUNK, RW_CHUNK), np.float32)), BF16)
    blk = lambda n: pl.BlockSpec((nb, tb, n), lambda i, j: (i, j, 0))
    lay = lambda a, l: _layer_spec(a.shape[1:], l, 2)
    common = [lay(mu, layer), lay(vec, layer), lay(w2p, layer), lay(a2p, layer), lay(g2p, layer)]
    consts = [_const_spec(bd, 2), _const_spec(tri, 2)]
    if has_vres:
        args = (p_a, v_first, mu, vec, w2p, a2p, g2p, v1p, v2p, bd, tri)
        in_specs = [blk(RW_IN), blk(RW_DIM)] + common + [lay(v1p, layer - 1), lay(v2p, layer - 1)] + consts
        out_specs = blk(RW_DIM)
        out_shape = jax.ShapeDtypeStruct((bsz, s, RW_DIM), BF16)
    else:
        args = (p_a, mu, vec, w2p, a2p, g2p, bd, tri)
        in_specs = [blk(RW_IN)] + common + consts
        out_specs = [blk(RW_DIM), blk(RW_DIM)]
        out_shape = [jax.ShapeDtypeStruct((bsz, s, RW_DIM), BF16), jax.ShapeDtypeStruct((bsz, s, RW_DIM), F32)]
    big = pltpu.VMEM((nb, tb, RW_DIM), F32)
    scratch = [pltpu.VMEM((nb, 1, RW_IN), F32), pltpu.VMEM((nb, RW_DIM, RW_DIM), F32)] + [big] * 9
    return pl.pallas_call(
        functools.partial(_rwkv_kernel, has_vres=has_vres, nb=nb, tb=tb),
        grid=(bsz // nb, s // tb),
        in_specs=in_specs,
        out_specs=out_specs,
        out_shape=out_shape,
        scratch_shapes=scratch,
        compiler_params=_cparams(("parallel", "arbitrary")),
        name="rwkv7",
    )(*args)


def _ret_tables():
    h, d, c = RET_HEADS, RET_HEAD_DIM, RET_CHUNK
    log_gamma = np.log(1.0 - 2.0 ** (-5.0 - np.arange(h, dtype=np.float64)))
    idx = np.arange(c, dtype=np.float64)
    diff = idx[:, None] - idx[None, :]
    dmask = np.where(diff >= 0, np.exp(log_gamma[:, None, None] * np.maximum(diff, 0.0)), 0.0)
    lane = np.arange(RET_DIM)
    head_qk = (lane % (RET_DIM // 2)) // (d // 2)
    head_v = lane // d
    qdec = np.exp(log_gamma[head_qk][None, :] * (idx[:, None] + 1.0))
    kdec = np.exp(log_gamma[head_qk][None, :] * (c - 1.0 - idx[:, None])) * d ** -0.5
    hm_qk = (np.arange(h)[:, None, None] == head_qk[None, None, :]) * np.ones((1, c, 1))
    hm_v = (np.arange(h)[:, None, None] == head_v[None, None, :]) * np.ones((1, c, 1))
    block = head_qk[:, None] == head_v[None, :]
    rdec = np.where(block, np.exp(log_gamma[head_qk] * c)[:, None], 0.0)
    pcos = np.zeros((LANES, RET_DIM // 2), np.float32)
    psin = np.zeros((LANES, RET_DIM // 2), np.float32)
    for hh in range(h):
        for j in range(RET_FREQS):
            pcos[TAB_RET_COS + j, hh * RET_FREQS + j] = 1.0
            psin[TAB_RET_SIN + j, hh * RET_FREQS + j] = 1.0
    f = lambda a: jnp.asarray(a, F32)
    return (f(dmask.reshape(h * c, c) * d ** -0.5), f(qdec), f(kdec), f(hm_qk.reshape(h * c, RET_DIM)),
            f(hm_v.reshape(h * c, RET_DIM)), f(block), f(rdec),
            jnp.asarray(np.concatenate([pcos, psin], axis=1), BF16))


def _ret_kernel(p_ref, tab_ref, dmask_ref, qdec_ref, kdec_ref, hmqk_ref, hmv_ref, block_ref, rdec_ref, pcs_ref,
                bd_ref, o_ref, r_ref, *, nb):
    c, dm = RET_CHUNK, RET_DIM
    half = dm // 2

    @pl.when(pl.program_id(0) == 0)
    def _():
        r_ref[...] = jnp.zeros_like(r_ref)

    bs = range(nb)
    cs = [_split_dot(tab_ref[b], pcs_ref[...], 3) for b in bs]
    p = [p_ref[b] for b in bs]

    def rope(x, t):
        cos, sin = t[:, :half], t[:, half:]
        x1, x2 = x[:, :half], x[:, half:]
        return jnp.concatenate([x1 * cos - x2 * sin, x2 * cos + x1 * sin], axis=1)

    q = [rope(p[b][:, 0:dm], cs[b]) for b in bs]
    k = [rope(p[b][:, dm:2 * dm], cs[b]) for b in bs]
    vb = [p[b][:, 2 * dm:3 * dm].astype(BF16) for b in bs]
    q_bd = [(jnp.concatenate([q[b]] * RET_HEADS, axis=0) * hmqk_ref[...]).astype(BF16) for b in bs]
    scores = [(_dot_nt(q_bd[b], k[b].astype(BF16)) * dmask_ref[...]).astype(BF16) for b in bs]
    o = [_dot(scores[b], vb[b]) * hmv_ref[...] for b in bs]
    r_old = [r_ref[b] for b in bs]
    cross = [_dot((q[b] * qdec_ref[...]).astype(BF16), r_old[b].astype(BF16)) for b in bs]
    upd = [_dot_tn((k[b] * kdec_ref[...]).astype(BF16), vb[b]) for b in bs]
    for b in bs:
        r_ref[b] = r_old[b] * rdec_ref[...] + upd[b] * block_ref[...]
    y = [o[b][0:c] + o[b][c:2 * c] + o[b][2 * c:3 * c] + o[b][3 * c:4 * c] + cross[b] for b in bs]
    ms = [_split_dot(y[b] * y[b], bd_ref[...], 2) * (1.0 / RET_HEAD_DIM) for b in bs]
    for b in bs:
        g = p[b][:, 3 * dm:4 * dm]
        o_ref[b] = (g * jax.nn.sigmoid(g) * y[b] * lax.rsqrt(ms[b] + NORM_EPS)).astype(o_ref.dtype)


def _retention(p_b, tab, nb):
    bsz, s, _ = p_b.shape
    consts = _ret_tables() + (jnp.asarray(_block_ones(RET_DIM, RET_HEAD_DIM), BF16),)
    blk = lambda n: pl.BlockSpec((nb, RET_CHUNK, n), lambda j, i: (i, j, 0))
    full = lambda a: pl.BlockSpec(a.shape, lambda j, i: (0,) * a.ndim)
    assert bsz == nb, "one batch group per time block keeps the state scratch simple"
    return pl.pallas_call(
        functools.partial(_ret_kernel, nb=nb),
        grid=(s // RET_CHUNK, bsz // nb),
        in_specs=[blk(RET_IN), blk(LANES)] + [full(a) for a in consts],
        out_specs=blk(RET_DIM),
        out_shape=jax.ShapeDtypeStruct((bsz, s, RET_DIM), BF16),
        scratch_shapes=[pltpu.VMEM((nb, RET_DIM, RET_DIM), F32)],
        compiler_params=_cparams(("arbitrary", "arbitrary")),
        name="retention",
    )(p_b, tab, *consts)


def _mla_tables():
    pk = np.zeros((LANES, MLA_HEAD_PAD), np.float32)
    pq = np.zeros((2 * MLA_FREQS, LANES), np.float32)
    for j in range(MLA_FREQS):
        for rep in range(2):
            pk[TAB_MLA_COS + j, rep * MLA_FREQS + j] = 1.0
            pk[TAB_MLA_SIN + j, MLA_ROPE + rep * MLA_FREQS + j] = 1.0
        pq[j, TAB_MLA_COS + j] = 1.0
        pq[MLA_FREQS + j, TAB_MLA_SIN + j] = 1.0
    place = np.zeros((MLA_HEAD_PAD, MLA_HEADS * MLA_HEAD_PAD), np.float32)
    for h in range(MLA_HEADS):
        for j in range(MLA_ROPE):
            place[j, h * MLA_HEAD_PAD + MLA_NOPE + j] = 1.0
            place[MLA_ROPE + j, h * MLA_HEAD_PAD + MLA_NOPE + j] = 1.0
    return jnp.asarray(pk, BF16), jnp.asarray(pq, BF16), jnp.asarray(place, BF16)


def _split_dot_nt(m_bf16, x, passes):
    acc = None
    rem = x
    for _ in range(passes):
        piece = rem.astype(BF16)
        part = _dot_nt(m_bf16, piece)
        acc = part if acc is None else acc + part
        rem = rem - piece.astype(F32)
    return acc


def _mla_proj_kernel(pc_ref, tab_ref, qn_ref, kn_ref, wqt_ref, wk_ref, wvt_ref, pk_ref, pq_ref, place_ref,
                     qt_ref, k_ref, vt_ref):
    pc = pc_ref[...]
    tab = tab_ref[0]
    ck = _split_dot(tab, pk_ref[...], 3)
    cs = _split_dot_nt(pq_ref[...], tab, 3)
    cos, sin = cs[0:MLA_FREQS], cs[MLA_FREQS:2 * MLA_FREQS]
    nq = _rms(pc[:, 0:MLA_Q_RANK], qn_ref[...]).astype(BF16)
    nkv = _rms(pc[:, MLA_Q_RANK:MLA_Q_RANK + MLA_KV_RANK], kn_ref[...]).astype(BF16)
    kr = (pc[:, MLA_Q_RANK + MLA_KV_RANK:MLA_C_PAD] * ck).astype(BF16)
    qa = _dot_nt(wqt_ref[...], nq) * ((MLA_NOPE + MLA_ROPE) ** -0.5 * math.log2(math.e))
    zeros = jnp.zeros((MLA_HEAD_PAD - MLA_NOPE - MLA_ROPE, qa.shape[1]), qt_ref.dtype)
    for h in range(MLA_HEADS):
        b0 = h * MLA_HEAD_PAD
        b1, b2, b3 = b0 + MLA_NOPE, b0 + MLA_NOPE + MLA_FREQS, b0 + MLA_NOPE + MLA_ROPE
        x1, x2 = qa[b1:b2], qa[b2:b3]
        qt_ref[0, b0:b1, :] = qa[b0:b1].astype(qt_ref.dtype)
        qt_ref[0, b1:b2, :] = (x1 * cos - x2 * sin).astype(qt_ref.dtype)
        qt_ref[0, b2:b3, :] = (x2 * cos + x1 * sin).astype(qt_ref.dtype)
        qt_ref[0, b3:b0 + MLA_HEAD_PAD, :] = zeros
    k_ref[...] = (_dot(nkv, wk_ref[...]) + _dot(kr, place_ref[...])).astype(k_ref.dtype)
    vt = _dot_nt(wvt_ref[...], nkv).astype(vt_ref.dtype)
    ones = jnp.ones((MLA_VT_ROWS - MLA_V, vt.shape[1]), vt_ref.dtype)
    for h in range(MLA_HEADS):
        vt_ref[0, h * MLA_VT_ROWS:h * MLA_VT_ROWS + MLA_V, :] = vt[h * MLA_V:(h + 1) * MLA_V, :]
        vt_ref[0, h * MLA_VT_ROWS + MLA_V:(h + 1) * MLA_VT_ROWS, :] = ones


def _mla_proj(p_c, tab, qn, kn, wqt, wk, wvt, layer, tm):
    t = p_c.shape[0]
    bsz, s, _ = tab.shape
    per = s // tm
    row = lambda n: pl.BlockSpec((tm, n), lambda i: (i, 0))
    cols = lambda n: pl.BlockSpec((1, n, tm), lambda i: (i // per, 0, i % per))
    pk, pq, place = _mla_tables()
    hp = MLA_HEADS * MLA_HEAD_PAD
    lay = lambda a: _layer_spec(a.shape[1:], layer, 1)
    return pl.pallas_call(
        _mla_proj_kernel,
        grid=(t // tm,),
        in_specs=[row(MLA_C_PAD), pl.BlockSpec((1, tm, LANES), lambda i: (i // per, i % per, 0)),
                  lay(qn), lay(kn), lay(wqt), lay(wk), lay(wvt),
                  _const_spec(pk, 1), _const_spec(pq, 1), _const_spec(place, 1)],
        out_specs=[cols(hp), row(hp), cols(MLA_HEADS * MLA_VT_ROWS)],
        out_shape=[jax.ShapeDtypeStruct((bsz, hp, s), BF16), jax.ShapeDtypeStruct((t, hp), BF16),
                   jax.ShapeDtypeStruct((bsz, MLA_HEADS * MLA_VT_ROWS, s), BF16)],
        compiler_params=_cparams(("parallel",)),
        name="mla_proj",
    )(p_c, tab, qn, kn, wqt, wk, wvt, pk, pq, place)


_NEG = -1e30
FLASH_ISSUE_AHEAD = 4
FLASH_KV_SPLIT = 2


def _flash_kernel(qi_ref, ki_ref, qt_ref, k_ref, vt_ref, o_ref, m_ref, l_ref, acc_ref, *, tq, tk):
    pair = pl.program_id(1)
    qi = qi_ref[pair]
    ki = ki_ref[pair]

    @pl.when(ki == 0)
    def _():
        m_ref[...] = jnp.full_like(m_ref, _NEG)
        l_ref[...] = jnp.zeros_like(l_ref)
        acc_ref[...] = jnp.zeros_like(acc_ref)

    def step(masked):
        tks = tk // FLASH_KV_SPLIT
        units = [(h, sub) for h in range(MLA_HEADS) for sub in range(FLASH_KV_SPLIT)]

        def scores(unit):
            h, sub = unit
            hs = slice(h * MLA_HEAD_PAD, (h + 1) * MLA_HEAD_PAD)
            return _dot(k_ref[0, sub * tks:(sub + 1) * tks, hs], qt_ref[0, hs, :])

        ahead = [scores(u) for u in units[:FLASH_ISSUE_AHEAD]]
        for i, (h, sub) in enumerate(units):
            vs = slice(h * MLA_V, (h + 1) * MLA_V)
            st = ahead.pop(0)
            if i + FLASH_ISSUE_AHEAD < len(units):
                ahead.append(scores(units[i + FLASH_ISSUE_AHEAD]))
            if masked:
                kv_i = lax.broadcasted_iota(jnp.int32, (tks, tq), 0) + sub * tks
                q_i = lax.broadcasted_iota(jnp.int32, (tks, tq), 1)
                st = jnp.where(kv_i <= q_i, st, _NEG)
            m_old = m_ref[h:h + 1, :]
            m_new = jnp.maximum(m_old, jnp.max(st, axis=0, keepdims=True))
            alpha = jnp.exp2(m_old - m_new)
            p = jnp.exp2(st - m_new).astype(BF16)
            pv = _dot(vt_ref[0, h * MLA_VT_ROWS:(h + 1) * MLA_VT_ROWS, sub * tks:(sub + 1) * tks], p)
            l_ref[h:h + 1, :] = alpha * l_ref[h:h + 1, :] + pv[MLA_V:MLA_V + 1, :]
            m_ref[h:h + 1, :] = m_new
            acc_ref[vs, :] = acc_ref[vs, :] * alpha + pv[0:MLA_V, :]

    @pl.when(ki < qi)
    def _():
        step(False)

    @pl.when(ki == qi)
    def _():
        step(True)
        for h in range(MLA_HEADS):
            vs = slice(h * MLA_V, (h + 1) * MLA_V)
            o_ref[0, vs, :] = (acc_ref[vs, :] / l_ref[h:h + 1, :]).astype(o_ref.dtype)


def _flash(qt, k, vt, tq):
    bsz, s, hp = k.shape
    tk = tq
    nq = s // tq
    pairs = [(i, j) for i in range(nq) for j in range(i + 1)]
    qi_tab = jnp.asarray(np.asarray([p[0] for p in pairs], np.int32))
    ki_tab = jnp.asarray(np.asarray([p[1] for p in pairs], np.int32))
    grid_spec = pltpu.PrefetchScalarGridSpec(
        num_scalar_prefetch=2,
        grid=(bsz, len(pairs)),
        in_specs=[pl.BlockSpec((1, hp, tq), lambda b, p, qi, ki: (b, 0, qi[p])),
                  pl.BlockSpec((1, tk, hp), lambda b, p, qi, ki: (b, ki[p], 0)),
                  pl.BlockSpec((1, MLA_HEADS * MLA_VT_ROWS, tk), lambda b, p, qi, ki: (b, 0, ki[p]))],
        out_specs=pl.BlockSpec((1, MLA_DIM, tq), lambda b, p, qi, ki: (b, 0, qi[p])),
        scratch_shapes=[pltpu.VMEM((MLA_HEADS, tq), F32), pltpu.VMEM((MLA_HEADS, tq), F32),
                        pltpu.VMEM((MLA_DIM, tq), F32)],
    )
    return pl.pallas_call(
        functools.partial(_flash_kernel, tq=tq, tk=tk),
        grid_spec=grid_spec,
        out_shape=jax.ShapeDtypeStruct((bsz, MLA_DIM, s), BF16),
        compiler_params=_cparams(("parallel", "arbitrary")),
        name="mla_flash",
    )(qi_tab, ki_tab, qt, k, vt)


def _ffn_kernel(h_ref, ya_ref, yb_ref, yct_ref, woa_ref, wob_ref, woc_ref, fg_ref, wg_ref, wu_ref, wd_ref, *rest,
                final, tm):
    if final:
        fin_ref, o_ref, h1_ref, hn_ref = rest
    else:
        o_ref, h1_ref, hn_ref = rest
    h1 = (h_ref[...] + _dot(ya_ref[...], woa_ref[...]) + _dot(yb_ref[...], wob_ref[...])
          + _dot_tn(yct_ref[0], woc_ref[...]))
    h1_ref[...] = h1
    hn_ref[...] = _rms(h1, fg_ref[...]).astype(BF16)

    def gate_up(r):
        x = hn_ref[r * MM_SUB_ROWS:(r + 1) * MM_SUB_ROWS, :]
        return _dot(x, wg_ref[...]), _dot(x, wu_ref[...])

    nsub = tm // MM_SUB_ROWS
    nxt = gate_up(0)
    for r in range(nsub):
        rows = slice(r * MM_SUB_ROWS, (r + 1) * MM_SUB_ROWS)
        gate, up = nxt
        if r + 1 < nsub:
            nxt = gate_up(r + 1)
        act = (gate * jax.nn.sigmoid(gate) * up).astype(BF16)
        out = h1_ref[rows, :] + _dot(act, wd_ref[...])
        if final:
            out = _rms(out, fin_ref[...])
        o_ref[rows, :] = out


def _ffn(h, ya, yb, yct, wo, fg, wgu, wd, fin, layer, tm):
    t, d = h.shape
    dff = wd.shape[1]
    per = yct.shape[2] // tm
    row = lambda n: pl.BlockSpec((tm, n), lambda i: (i, 0))
    once = pl.Buffered(1)
    final = fin is not None
    in_specs = [row(d), row(RW_DIM), row(RET_DIM),
                pl.BlockSpec((1, MLA_DIM, tm), lambda i: (i // per, 0, i % per)),
                pl.BlockSpec((None, RW_DIM, d), lambda i: (layer, 0, 0), pipeline_mode=once),
                pl.BlockSpec((None, RET_DIM, d), lambda i: (layer, 1, 0), pipeline_mode=once),
                pl.BlockSpec((None, MLA_DIM, d), lambda i: (layer, 1, 0), pipeline_mode=once),
                _layer_spec(fg.shape[1:], layer, 1),
                pl.BlockSpec((None, d, dff), lambda i: (layer, 0, 0), pipeline_mode=once),
                pl.BlockSpec((None, d, dff), lambda i: (layer, 0, 1), pipeline_mode=once),
                pl.BlockSpec((None, dff, d), lambda i: (layer, 0, 0), pipeline_mode=once)]
    args = [h, ya, yb, yct, wo, wo, wo, fg, wgu, wgu, wd]
    if final:
        in_specs.append(_const_spec(fin, 1))
        args.append(fin)
    return pl.pallas_call(
        functools.partial(_ffn_kernel, final=final, tm=tm),
        grid=(t // tm,),
        in_specs=in_specs,
        out_specs=row(d),
        out_shape=jax.ShapeDtypeStruct((t, d), F32),
        scratch_shapes=[pltpu.VMEM((tm, d), F32), pltpu.VMEM((tm, d), BF16)],
        compiler_params=_cparams(("parallel",)),
        name="outproj_ffn",
    )(*args)


def _relayout(w, m, transpose=False):
    out = "lki" if transpose else "lik"
    return jnp.einsum("lij,jk->" + out, w.astype(BF16), jnp.asarray(m, BF16), preferred_element_type=BF16)


def _rot_half_into(m, src, dst, n, sign=1.0):
    for j in range(n // 2):
        m[src + n // 2 + j, dst + j] = -sign
        m[src + j, dst + n // 2 + j] = sign


def _prep_in_weights(w_in):
    perm = np.zeros((2 * RET_DIM, 2 * RET_DIM), np.float32)
    for part in range(2):
        for hh in range(RET_HEADS):
            for half in range(2):
                for j in range(RET_FREQS):
                    src = part * RET_DIM + hh * RET_HEAD_DIM + half * RET_FREQS + j
                    dst = part * RET_DIM + half * (RET_DIM // 2) + hh * RET_FREQS + j
                    perm[src, dst] = 1.0
    mc = np.zeros((MLA_IN, MLA_C_PAD), np.float32)
    mc[np.arange(MLA_IN), np.arange(MLA_IN)] = 1.0
    _rot_half_into(mc, MLA_Q_RANK + MLA_KV_RANK, MLA_IN, MLA_ROPE)
    b0 = RW_IN
    c0 = RW_IN + RET_IN
    wa = w_in[..., :b0].astype(BF16)
    wb = jnp.concatenate([_relayout(w_in[..., b0:b0 + 2 * RET_DIM], perm), w_in[..., b0 + 2 * RET_DIM:c0].astype(BF16)],
                         axis=-1)
    wc = _relayout(w_in[..., c0:], mc)
    return wa, wb, wc


def _prep_mla_weights(w_q_up, w_kv_up):
    qd = MLA_NOPE + MLA_ROPE
    kvd = MLA_NOPE + MLA_V
    hp = MLA_HEADS * MLA_HEAD_PAD
    mq = np.zeros((MLA_HEADS * qd, hp), np.float32)
    mk = np.zeros((MLA_HEADS * kvd, hp), np.float32)
    mv = np.zeros((MLA_HEADS * kvd, MLA_DIM), np.float32)
    for h in range(MLA_HEADS):
        for c in range(qd):
            mq[h * qd + c, h * MLA_HEAD_PAD + c] = 1.0
        for c in range(MLA_NOPE):
            mk[h * kvd + c, h * MLA_HEAD_PAD + c] = 1.0
        for c in range(MLA_V):
            mv[h * kvd + MLA_NOPE + c, h * MLA_V + c] = 1.0
    return (_relayout(w_q_up, mq, transpose=True), _relayout(w_kv_up, mk),
            _relayout(w_kv_up, mv, transpose=True))


def _pad_rows(w, start, total):
    return jnp.pad(w, ((0, 0), (start, total - start - w.shape[1]), (0, 0)))


def _rope_table(positions):
    inv_ret = ROPE_BASE ** (-jnp.arange(0, RET_HEAD_DIM, 2, dtype=F32) / RET_HEAD_DIM)
    inv_mla = ROPE_BASE ** (-jnp.arange(0, MLA_ROPE, 2, dtype=F32) / MLA_ROPE)
    inv = jnp.concatenate([inv_ret, inv_ret, inv_mla, inv_mla, jnp.zeros((LANES - TAB_MLA_SIN - MLA_FREQS,), F32)])
    lane = np.arange(LANES)
    is_cos = (lane < TAB_RET_SIN) | ((lane >= TAB_MLA_COS) & (lane < TAB_MLA_SIN))
    ang = positions.astype(F32)[..., None] * inv
    return jnp.where(jnp.asarray(is_cos), jnp.cos(ang), jnp.sin(ang))


def kernel(x, positions, attn_norm, w_in, w_out, rw_mu, rw_w0, rw_w2, rw_a0, rw_a2, rw_g2, rw_k_k, rw_k_a, rw_r_k, rw_gn_w, rw_gn_b, rw_v0, rw_v1, rw_v2, mla_q_norm, mla_kv_norm, mla_w_q_up, mla_w_kv_up, ffn_norm, w_gate_up, w_down, final_norm):
    bsz, s, d = x.shape
    depth = w_in.shape[0]
    t = bsz * s
    tm = min(512, s)
    hp = MLA_HEADS * MLA_HEAD_PAD

    tab = _rope_table(positions)
    wa, wb, wc = _prep_in_weights(w_in)
    wqt, wk, wvt = _prep_mla_weights(mla_w_q_up, mla_w_kv_up)
    wo, wgu, wd = w_out.astype(BF16), w_gate_up.astype(BF16), w_down.astype(BF16)
    v0 = jnp.concatenate([jnp.zeros((1, RW_DIM), F32), rw_v0], axis=0)
    vec = jnp.stack([rw_w0, rw_a0, rw_k_k, rw_k_a, rw_r_k.reshape(depth, RW_DIM), rw_gn_w, rw_gn_b, v0], axis=1)
    vec = jnp.pad(vec, ((0, 0), (0, _VEC_ROWS - vec.shape[1]), (0, 0)))
    w2p = _pad_rows(rw_w2, 0, RW_LOWRANK).astype(BF16)
    a2p = _pad_rows(rw_a2, RW_DECAY_RANK, RW_LOWRANK).astype(BF16)
    g2p = _pad_rows(rw_g2, RW_DECAY_RANK + RW_A_RANK, RW_LOWRANK).astype(BF16)
    v1p = jnp.pad(rw_v1, ((0, 0), (0, 0), (0, LANES - RW_V_RANK))).astype(BF16)
    v2p = _pad_rows(rw_v2, 0, LANES).astype(BF16)
    row3 = lambda a: a[:, None, :]
    mu, an, qn, kn, fn = row3(rw_mu), row3(attn_norm), row3(mla_q_norm), row3(mla_kv_norm), row3(ffn_norm)

    h = x.reshape(t, d)
    v_first = None
    for l in range(depth):
        p_a, p_b, p_c = _inproj(h, an, wa, wb, wc, l, tm)
        res = _rwkv(p_a.reshape(bsz, s, RW_IN), v_first, mu, vec, w2p, a2p, g2p, v1p, v2p, l, bsz, min(256, s))
        if l == 0:
            y_a, v_first = res
        else:
            y_a = res
        y_b = _retention(p_b.reshape(bsz, s, RET_IN), tab, bsz)
        qt, k, vt = _mla_proj(p_c, tab, qn, kn, wqt, wk, wvt, l, tm)
        y_ct = _flash(qt, k.reshape(bsz, s, hp), vt, min(512, s))
        fin = final_norm[None, :] if l == depth - 1 else None
        h = _ffn(h, y_a.reshape(t, RW_DIM), y_b.reshape(t, RET_DIM), y_ct, wo, fn, wgu, wd, fin, l, tm)
    return h.reshape(bsz, s, d)
```

```python
import functools
import math

import jax
import jax.numpy as jnp
import numpy as np
from jax import lax
from jax.experimental import pallas as pl
from jax.experimental.pallas import tpu as pltpu

F32 = jnp.float32
BF16 = jnp.bfloat16

NORM_EPS = 1e-6
ROPE_BASE = 10000.0
LANES = 128

RW_HEADS = 4
RW_HEAD_DIM = 64
RW_DIM = RW_HEADS * RW_HEAD_DIM
RW_DECAY_RANK = 32
RW_A_RANK = 32
RW_V_RANK = 32
RW_GATE_RANK = 64
RW_GN_EPS = 64e-5
RW_IN = 3 * RW_DIM + RW_DECAY_RANK + RW_A_RANK + RW_GATE_RANK
RW_LOWRANK = RW_DECAY_RANK + RW_A_RANK + RW_GATE_RANK
RW_CHUNK = 64

RET_HEADS = 4
RET_HEAD_DIM = 64
RET_DIM = RET_HEADS * RET_HEAD_DIM
RET_CHUNK = 128
RET_IN = 4 * RET_DIM
RET_FREQS = RET_HEAD_DIM // 2

MLA_HEADS = 8
MLA_NOPE = 64
MLA_ROPE = 32
MLA_V = 64
MLA_Q_RANK = 384
MLA_KV_RANK = 256
MLA_DIM = MLA_HEADS * MLA_V
MLA_IN = MLA_Q_RANK + MLA_KV_RANK + MLA_ROPE
MLA_HEAD_PAD = 128
MLA_C_PAD = 768
MLA_FREQS = MLA_ROPE // 2
MLA_VT_ROWS = MLA_V + 16

TAB_RET_COS = 0
TAB_RET_SIN = RET_FREQS
TAB_MLA_COS = 2 * RET_FREQS
TAB_MLA_SIN = 2 * RET_FREQS + MLA_FREQS

V7X_VMEM_LIMIT_BYTES = 56 * 1024 * 1024


MM_SUB_ROWS = 256


def _cparams(semantics):
    return pltpu.CompilerParams(dimension_semantics=semantics, vmem_limit_bytes=V7X_VMEM_LIMIT_BYTES)


def _dot(a, b):
    return jnp.dot(a, b, preferred_element_type=F32)


def _dot_nt(a, b):
    return lax.dot_general(a, b, (((1,), (1,)), ((), ())), preferred_element_type=F32)


def _dot_tn(a, b):
    return lax.dot_general(a, b, (((0,), (0,)), ((), ())), preferred_element_type=F32)


def _split_dot(x, m_bf16, passes):
    acc = None
    rem = x
    for _ in range(passes):
        piece = rem.astype(BF16)
        part = _dot(piece, m_bf16)
        acc = part if acc is None else acc + part
        rem = rem - piece.astype(F32)
    return acc


def _split_dot_left(m_bf16, x):
    acc = None
    rem = x
    for _ in range(3):
        piece = rem.astype(BF16)
        part = _dot(m_bf16, piece)
        acc = part if acc is None else acc + part
        rem = rem - piece.astype(F32)
    return acc


def _rms(x, g):
    return x * lax.rsqrt(jnp.mean(x * x, axis=-1, keepdims=True) + NORM_EPS) * g


def _layer_spec(shape, layer, ngrid):
    zeros = (0,) * len(shape)
    if ngrid == 1:
        return pl.BlockSpec((None,) + tuple(shape), lambda i: (layer,) + zeros)
    return pl.BlockSpec((None,) + tuple(shape), lambda i, j: (layer,) + zeros)


def _const_spec(a, ngrid):
    zeros = (0,) * a.ndim
    if ngrid == 1:
        return pl.BlockSpec(a.shape, lambda i: zeros)
    return pl.BlockSpec(a.shape, lambda i, j: zeros)


def _inproj_kernel(x_ref, g_ref, wa_ref, wb_ref, wc_ref, pa_ref, pb_ref, pc_ref):
    for r in range(x_ref.shape[0] // MM_SUB_ROWS):
        rows = slice(r * MM_SUB_ROWS, (r + 1) * MM_SUB_ROWS)
        hn = _rms(x_ref[rows, :], g_ref[...]).astype(BF16)
        pa_ref[rows, :] = _dot(hn, wa_ref[...])
        pb_ref[rows, :] = _dot(hn, wb_ref[...])
        pc_ref[rows, :] = _dot(hn, wc_ref[...])


def _inproj(x, g, wa, wb, wc, layer, tm):
    t, d = x.shape
    row = lambda n: pl.BlockSpec((tm, n), lambda i: (i, 0))
    ws = (wa, wb, wc)
    return pl.pallas_call(
        _inproj_kernel,
        grid=(t // tm,),
        in_specs=[row(d), _layer_spec(g.shape[1:], layer, 1)] + [_layer_spec(w.shape[1:], layer, 1) for w in ws],
        out_specs=[row(w.shape[2]) for w in ws],
        out_shape=[jax.ShapeDtypeStruct((t, w.shape[2]), F32) for w in ws],
        compiler_params=_cparams(("parallel",)),
        name="inproj",
    )(x, g, wa, wb, wc)


_VEC_ROWS = 16
(_V_W0, _V_A0, _V_KK, _V_KA, _V_RK, _V_GNW, _V_GNB, _V_V0) = range(8)


def _rwkv_kernel(*refs, has_vres, nb, tb):
    if has_vres:
        (p_ref, vf_ref, mu_ref, vec_ref, w2_ref, a2_ref, g2_ref, v1_ref, v2_ref, bd_ref, tri_ref,
         o_ref, carry_ref, s_ref, r_s, k_s, v_s, lw_s, kk_s, ka_s, g_s, bon_s, y_s) = refs
    else:
        (p_ref, mu_ref, vec_ref, w2_ref, a2_ref, g2_ref, bd_ref, tri_ref,
         o_ref, vfo_ref, carry_ref, s_ref, r_s, k_s, v_s, lw_s, kk_s, ka_s, g_s, bon_s, y_s) = refs

    L = RW_CHUNK
    C = RW_DIM
    tstep = pl.program_id(1)

    @pl.when(tstep == 0)
    def _():
        carry_ref[...] = jnp.zeros_like(carry_ref)
        s_ref[...] = jnp.zeros_like(s_ref)

    vec = vec_ref[...]
    row = lambda i: vec[i:i + 1, :]
    bd = bd_ref[...]
    mu = mu_ref[...]

    def bdsum(x):
        return _split_dot(x, bd, 2)

    def prep(b, _):
        p = p_ref[b]
        shifted = pltpu.roll(p, 1, 0)
        first = lax.broadcasted_iota(jnp.int32, p.shape, 0) == 0
        p_prev = jnp.where(first, carry_ref[b], shifted)
        carry_ref[b] = p[tb - 1:tb, :]
        ps = p + (p_prev - p) * mu
        r = ps[:, 0:C]
        k = ps[:, C:2 * C]
        v = ps[:, 2 * C:3 * C]
        lr = ps[:, 3 * C:3 * C + RW_LOWRANK]
        z = row(_V_W0) + _dot(jnp.tanh(lr).astype(BF16), w2_ref[...])
        lw = -math.exp(-0.5) * jax.nn.sigmoid(z)
        a = jax.nn.sigmoid(row(_V_A0) + _dot(lr.astype(BF16), a2_ref[...]))
        g = _dot(jax.nn.sigmoid(lr).astype(BF16), g2_ref[...])
        if has_vres:
            lat = _dot(v.astype(BF16), v1_ref[...]).astype(BF16)
            v = v + (vf_ref[b] - v) * jax.nn.sigmoid(row(_V_V0) + _dot(lat, v2_ref[...]))
        else:
            vfo_ref[b] = v
        kk = k * row(_V_KK)
        kk = kk * lax.rsqrt(jnp.maximum(bdsum(kk * kk), 1e-24))
        k2 = k * (1.0 + (a - 1.0) * row(_V_KA))
        r_s[b] = r
        k_s[b] = k2
        v_s[b] = v
        lw_s[b] = lw
        kk_s[b] = kk
        ka_s[b] = kk * a
        g_s[b] = g
        bon_s[b] = bdsum(r * k2 * row(_V_RK)) * v
        return 0

    lax.fori_loop(0, nb, prep, 0)

    hl = RW_HEADS * L
    r_i = lax.broadcasted_iota(jnp.int32, (hl, C), 0)
    c_i = lax.broadcasted_iota(jnp.int32, (hl, C), 1)
    headmask = (r_i // L) == (c_i // RW_HEAD_DIM)
    t_i = lax.broadcasted_iota(jnp.int32, (L, hl), 0)
    s_i = lax.broadcasted_iota(jnp.int32, (L, hl), 1) % L
    strict = t_i > s_i
    incl = t_i >= s_i
    tri = tri_ref[...]

    def masked4(x):
        return jnp.where(headmask, jnp.concatenate([x] * RW_HEADS, axis=0), 0.0).astype(BF16)

    def chunk(j, _):
        sl = pl.ds(pl.multiple_of(j * L, L), L)
        bs = range(nb)
        r = [r_s[b, sl, :] for b in bs]
        k2 = [k_s[b, sl, :] for b in bs]
        v = [v_s[b, sl, :] for b in bs]
        lw = [lw_s[b, sl, :] for b in bs]
        kk = [kk_s[b, sl, :] for b in bs]
        ka = [ka_s[b, sl, :] for b in bs]
        c = [_split_dot_left(tri, lw[b]) for b in bs]
        c_last = [c[b][L - 1:L, :] for b in bs]
        sig = [0.5 * c_last[b] for b in bs]
        e_neg = [jnp.exp(sig[b] - c[b]) for b in bs]
        kt = [k2[b] * e_neg[b] for b in bs]
        bt = [ka[b] * e_neg[b] for b in bs]
        x = [jnp.concatenate([-(kk[b] * jnp.exp(c[b] - lw[b] - sig[b])), r[b] * jnp.exp(c[b] - sig[b])],
                             axis=0).astype(BF16) for b in bs]
        s_old = [s_ref[b] for b in bs]
        kbs_w = [jnp.concatenate([masked4(kt[b]), masked4(bt[b]), (s_old[b] * jnp.exp(sig[b])).astype(BF16)], axis=0)
                 for b in bs]
        abx = [_dot_nt(x[b], kbs_w[b]) for b in bs]
        ab = [abx[b][:, 0:2 * hl] for b in bs]
        xs = [abx[b][:, 2 * hl:2 * hl + C] for b in bs]
        a_k = [jnp.concatenate([jnp.where(strict, ab[b][0:L, 0:hl], 0.0), jnp.where(incl, ab[b][L:2 * L, 0:hl], 0.0)],
                               axis=0).astype(BF16) for b in bs]
        n_w = [jnp.where(strict, ab[b][0:L, hl:2 * hl], 0.0) for b in bs]
        a_rb = [jnp.where(incl, ab[b][L:2 * L, hl:2 * hl], 0.0).astype(BF16) for b in bs]
        av = [_dot(a_k[b], masked4(v[b])) for b in bs]
        u = [av[b][0:L] + xs[b][0:L] for b in bs]
        for i in range(6):
            n_b = [n_w[b].astype(BF16) for b in bs]
            if i < 5:
                nu = [_dot(n_b[b], jnp.concatenate([masked4(u[b]), masked4(n_w[b])], axis=1)) for b in bs]
                u = [u[b] + nu[b][:, 0:C] for b in bs]
                n_w = [nu[b][:, C:2 * C] for b in bs]
            else:
                u = [u[b] + _dot(n_b[b], masked4(u[b])) for b in bs]
        y = [av[b][L:2 * L] + xs[b][L:2 * L] + _dot(a_rb[b], masked4(u[b])) for b in bs]
        for b in bs:
            y_s[b, sl, :] = y[b]
        scale = [jnp.exp(c_last[b] - sig[b]) for b in bs]
        kbs = [jnp.concatenate([kt[b] * scale[b], bt[b] * scale[b]], axis=0).astype(BF16) for b in bs]
        vu = [jnp.concatenate([v[b], u[b]], axis=0).astype(BF16) for b in bs]
        upd = [_dot_tn(vu[b], kbs[b]) for b in bs]
        for b in bs:
            s_ref[b] = s_old[b] * jnp.exp(c_last[b]) + jnp.where(headmask, upd[b], 0.0)
        return 0

    lax.fori_loop(0, tb // L, chunk, 0)

    def post(b, _):
        y = y_s[b]
        mean = bdsum(y) * (1.0 / RW_HEAD_DIM)
        d = y - mean
        var = bdsum(d * d) * (1.0 / RW_HEAD_DIM)
        yn = d * lax.rsqrt(var + RW_GN_EPS) * row(_V_GNW) + row(_V_GNB)
        o_ref[b] = ((yn + bon_s[b]) * g_s[b]).astype(o_ref.dtype)
        return 0

    lax.fori_loop(0, nb, post, 0)


def _block_ones(n, blk):
    i = np.arange(n)
    return (i[:, None] // blk == i[None, :] // blk).astype(np.float32)


def _rwkv(p_a, v_first, mu, vec, w2p, a2p, g2p, v1p, v2p, layer, nb, tb):
    bsz, s, _ = p_a.shape
    has_vres = v_first is not None
    bd = jnp.asarray(_block_ones(RW_DIM, RW_HEAD_DIM), BF16)
    tri = jnp.asarray(np.tril(np.ones((RW_CHUNK, RW_CHUNK), np.float32)), BF16)
    blk = lambda n: pl.BlockSpec((nb, tb, n), lambda i, j: (i, j, 0))
    lay = lambda a, l: _layer_spec(a.shape[1:], l, 2)
    common = [lay(mu, layer), lay(vec, layer), lay(w2p, layer), lay(a2p, layer), lay(g2p, layer)]
    consts = [_const_spec(bd, 2), _const_spec(tri, 2)]
    if has_vres:
        args = (p_a, v_first, mu, vec, w2p, a2p, g2p, v1p, v2p, bd, tri)
        in_specs = [blk(RW_IN), blk(RW_DIM)] + common + [lay(v1p, layer - 1), lay(v2p, layer - 1)] + consts
        out_specs = blk(RW_DIM)
        out_shape = jax.ShapeDtypeStruct((bsz, s, RW_DIM), BF16)
    else:
        args = (p_a, mu, vec, w2p, a2p, g2p, bd, tri)
        in_specs = [blk(RW_IN)] + common + consts
        out_specs = [blk(RW_DIM), blk(RW_DIM)]
        out_shape = [jax.ShapeDtypeStruct((bsz, s, RW_DIM), BF16), jax.ShapeDtypeStruct((bsz, s, RW_DIM), F32)]
    big = pltpu.VMEM((nb, tb, RW_DIM), F32)
    scratch = [pltpu.VMEM((nb, 1, RW_IN), F32), pltpu.VMEM((nb, RW_DIM, RW_DIM), F32)] + [big] * 9
    return pl.pallas_call(
        functools.partial(_rwkv_kernel, has_vres=has_vres, nb=nb, tb=tb),
        grid=(bsz // nb, s // tb),
        in_specs=in_specs,
        out_specs=out_specs,
        out_shape=out_shape,
        scratch_shapes=scratch,
        compiler_params=_cparams(("parallel", "arbitrary")),
        name="rwkv7",
    )(*args)


def _ret_tables():
    h, d, c = RET_HEADS, RET_HEAD_DIM, RET_CHUNK
    log_gamma = np.log(1.0 - 2.0 ** (-5.0 - np.arange(h, dtype=np.float64)))
    idx = np.arange(c, dtype=np.float64)
    diff = idx[:, None] - idx[None, :]
    dmask = np.where(diff >= 0, np.exp(log_gamma[:, None, None] * np.maximum(diff, 0.0)), 0.0)
    lane = np.arange(RET_DIM)
    head_qk = (lane % (RET_DIM // 2)) // (d // 2)
    head_v = lane // d
    qdec = np.exp(log_gamma[head_qk][None, :] * (idx[:, None] + 1.0))
    kdec = np.exp(log_gamma[head_qk][None, :] * (c - 1.0 - idx[:, None])) * d ** -0.5
    hm_qk = (np.arange(h)[:, None, None] == head_qk[None, None, :]) * np.ones((1, c, 1))
    hm_v = (np.arange(h)[:, None, None] == head_v[None, None, :]) * np.ones((1, c, 1))
    block = head_qk[:, None] == head_v[None, :]
    rdec = np.where(block, np.exp(log_gamma[head_qk] * c)[:, None], 0.0)
    pcos = np.zeros((LANES, RET_DIM // 2), np.float32)
    psin = np.zeros((LANES, RET_DIM // 2), np.float32)
    for hh in range(h):
        for j in range(RET_FREQS):
            pcos[TAB_RET_COS + j, hh * RET_FREQS + j] = 1.0
            psin[TAB_RET_SIN + j, hh * RET_FREQS + j] = 1.0
    f = lambda a: jnp.asarray(a, F32)
    return (f(dmask.reshape(h * c, c) * d ** -0.5), f(qdec), f(kdec), f(hm_qk.reshape(h * c, RET_DIM)),
            f(hm_v.reshape(h * c, RET_DIM)), f(block), f(rdec),
            jnp.asarray(np.concatenate([pcos, psin], axis=1), BF16))


def _ret_kernel(p_ref, tab_ref, dmask_ref, qdec_ref, kdec_ref, hmqk_ref, hmv_ref, block_ref, rdec_ref, pcs_ref,
                bd_ref, o_ref, r_ref, *, nb):
    c, dm = RET_CHUNK, RET_DIM
    half = dm // 2

    @pl.when(pl.program_id(0) == 0)
    def _():
        r_ref[...] = jnp.zeros_like(r_ref)

    bs = range(nb)
    cs = [_split_dot(tab_ref[b], pcs_ref[...], 3) for b in bs]
    p = [p_ref[b] for b in bs]

    def rope(x, t):
        cos, sin = t[:, :half], t[:, half:]
        x1, x2 = x[:, :half], x[:, half:]
        return jnp.concatenate([x1 * cos - x2 * sin, x2 * cos + x1 * sin], axis=1)

    q = [rope(p[b][:, 0:dm], cs[b]) for b in bs]
    k = [rope(p[b][:, dm:2 * dm], cs[b]) for b in bs]
    vb = [p[b][:, 2 * dm:3 * dm].astype(BF16) for b in bs]
    q_bd = [(jnp.concatenate([q[b]] * RET_HEADS, axis=0) * hmqk_ref[...]).astype(BF16) for b in bs]
    scores = [(_dot_nt(q_bd[b], k[b].astype(BF16)) * dmask_ref[...]).astype(BF16) for b in bs]
    o = [_dot(scores[b], vb[b]) * hmv_ref[...] for b in bs]
    r_old = [r_ref[b] for b in bs]
    cross = [_dot((q[b] * qdec_ref[...]).astype(BF16), r_old[b].astype(BF16)) for b in bs]
    upd = [_dot_tn((k[b] * kdec_ref[...]).astype(BF16), vb[b]) for b in bs]
    for b in bs:
        r_ref[b] = r_old[b] * rdec_ref[...] + upd[b] * block_ref[...]
    y = [o[b][0:c] + o[b][c:2 * c] + o[b][2 * c:3 * c] + o[b][3 * c:4 * c] + cross[b] for b in bs]
    ms = [_split_dot(y[b] * y[b], bd_ref[...], 2) * (1.0 / RET_HEAD_DIM) for b in bs]
    for b in bs:
        g = p[b][:, 3 * dm:4 * dm]
        o_ref[b] = (g * jax.nn.sigmoid(g) * y[b] * lax.rsqrt(ms[b] + NORM_EPS)).astype(o_ref.dtype)


def _retention(p_b, tab, nb):
    bsz, s, _ = p_b.shape
    consts = _ret_tables() + (jnp.asarray(_block_ones(RET_DIM, RET_HEAD_DIM), BF16),)
    blk = lambda n: pl.BlockSpec((nb, RET_CHUNK, n), lambda j, i: (i, j, 0))
    full = lambda a: pl.BlockSpec(a.shape, lambda j, i: (0,) * a.ndim)
    assert bsz == nb, "one batch group per time block keeps the state scratch simple"
    return pl.pallas_call(
        functools.partial(_ret_kernel, nb=nb),
        grid=(s // RET_CHUNK, bsz // nb),
        in_specs=[blk(RET_IN), blk(LANES)] + [full(a) for a in consts],
        out_specs=blk(RET_DIM),
        out_shape=jax.ShapeDtypeStruct((bsz, s, RET_DIM), BF16),
        scratch_shapes=[pltpu.VMEM((nb, RET_DIM, RET_DIM), F32)],
        compiler_params=_cparams(("arbitrary", "arbitrary")),
        name="retention",
    )(p_b, tab, *consts)


def _mla_tables():
    pk = np.zeros((LANES, MLA_HEAD_PAD), np.float32)
    pq = np.zeros((2 * MLA_FREQS, LANES), np.float32)
    for j in range(MLA_FREQS):
        for rep in range(2):
            pk[TAB_MLA_COS + j, rep * MLA_FREQS + j] = 1.0
            pk[TAB_MLA_SIN + j, MLA_ROPE + rep * MLA_FREQS + j] = 1.0
        pq[j, TAB_MLA_COS + j] = 1.0
        pq[MLA_FREQS + j, TAB_MLA_SIN + j] = 1.0
    place = np.zeros((MLA_HEAD_PAD, MLA_HEADS * MLA_HEAD_PAD), np.float32)
    for h in range(MLA_HEADS):
        for j in range(MLA_ROPE):
            place[j, h * MLA_HEAD_PAD + MLA_NOPE + j] = 1.0
            place[MLA_ROPE + j, h * MLA_HEAD_PAD + MLA_NOPE + j] = 1.0
    return jnp.asarray(pk, BF16), jnp.asarray(pq, BF16), jnp.asarray(place, BF16)


def _split_dot_nt(m_bf16, x, passes):
    acc = None
    rem = x
    for _ in range(passes):
        piece = rem.astype(BF16)
        part = _dot_nt(m_bf16, piece)
        acc = part if acc is None else acc + part
        rem = rem - piece.astype(F32)
    return acc


def _mla_proj_kernel(pc_ref, tab_ref, qn_ref, kn_ref, wqt_ref, wk_ref, wvt_ref, pk_ref, pq_ref, place_ref,
                     qt_ref, k_ref, vt_ref):
    pc = pc_ref[...]
    tab = tab_ref[0]
    ck = _split_dot(tab, pk_ref[...], 3)
    cs = _split_dot_nt(pq_ref[...], tab, 3)
    cos, sin = cs[0:MLA_FREQS], cs[MLA_FREQS:2 * MLA_FREQS]
    nq = _rms(pc[:, 0:MLA_Q_RANK], qn_ref[...]).astype(BF16)
    nkv = _rms(pc[:, MLA_Q_RANK:MLA_Q_RANK + MLA_KV_RANK], kn_ref[...]).astype(BF16)
    kr = (pc[:, MLA_Q_RANK + MLA_KV_RANK:MLA_C_PAD] * ck).astype(BF16)
    qa = _dot_nt(wqt_ref[...], nq) * ((MLA_NOPE + MLA_ROPE) ** -0.5 * math.log2(math.e))
    zeros = jnp.zeros((MLA_HEAD_PAD - MLA_NOPE - MLA_ROPE, qa.shape[1]), qt_ref.dtype)
    for h in range(MLA_HEADS):
        b0 = h * MLA_HEAD_PAD
        b1, b2, b3 = b0 + MLA_NOPE, b0 + MLA_NOPE + MLA_FREQS, b0 + MLA_NOPE + MLA_ROPE
        x1, x2 = qa[b1:b2], qa[b2:b3]
        qt_ref[0, b0:b1, :] = qa[b0:b1].astype(qt_ref.dtype)
        qt_ref[0, b1:b2, :] = (x1 * cos - x2 * sin).astype(qt_ref.dtype)
        qt_ref[0, b2:b3, :] = (x2 * cos + x1 * sin).astype(qt_ref.dtype)
        qt_ref[0, b3:b0 + MLA_HEAD_PAD, :] = zeros
    k_ref[...] = (_dot(nkv, wk_ref[...]) + _dot(kr, place_ref[...])).astype(k_ref.dtype)
    vt = _dot_nt(wvt_ref[...], nkv).astype(vt_ref.dtype)
    ones = jnp.ones((MLA_VT_ROWS - MLA_V, vt.shape[1]), vt_ref.dtype)
    for h in range(MLA_HEADS):
        vt_ref[0, h * MLA_VT_ROWS:h * MLA_VT_ROWS + MLA_V, :] = vt[h * MLA_V:(h + 1) * MLA_V, :]
        vt_ref[0, h * MLA_VT_ROWS + MLA_V:(h + 1) * MLA_VT_ROWS, :] = ones


def _mla_proj(p_c, tab, qn, kn, wqt, wk, wvt, layer, tm):
    t = p_c.shape[0]
    bsz, s, _ = tab.shape
    per = s // tm
    row = lambda n: pl.BlockSpec((tm, n), lambda i: (i, 0))
    cols = lambda n: pl.BlockSpec((1, n, tm), lambda i: (i // per, 0, i % per))
    pk, pq, place = _mla_tables()
    hp = MLA_HEADS * MLA_HEAD_PAD
    lay = lambda a: _layer_spec(a.shape[1:], layer, 1)
    return pl.pallas_call(
        _mla_proj_kernel,
        grid=(t // tm,),
        in_specs=[row(MLA_C_PAD), pl.BlockSpec((1, tm, LANES), lambda i: (i // per, i % per, 0)),
                  lay(qn), lay(kn), lay(wqt), lay(wk), lay(wvt),
                  _const_spec(pk, 1), _const_spec(pq, 1), _const_spec(place, 1)],
        out_specs=[cols(hp), row(hp), cols(MLA_HEADS * MLA_VT_ROWS)],
        out_shape=[jax.ShapeDtypeStruct((bsz, hp, s), BF16), jax.ShapeDtypeStruct((t, hp), BF16),
                   jax.ShapeDtypeStruct((bsz, MLA_HEADS * MLA_VT_ROWS, s), BF16)],
        compiler_params=_cparams(("parallel",)),
        name="mla_proj",
    )(p_c, tab, qn, kn, wqt, wk, wvt, pk, pq, place)


_NEG = -1e30
FLASH_ISSUE_AHEAD = 4
FLASH_KV_SUB = 256
FLASH_Q_SUB = 512


def _flash_units(t, diagonal):
    units = []
    for q0 in range(0, t, FLASH_Q_SUB):
        for h in range(MLA_HEADS):
            for kv0 in range(0, t, FLASH_KV_SUB):
                if not diagonal:
                    units.append((h, kv0, q0, q0 + FLASH_Q_SUB, False))
                    continue
                q_lo = max(q0, kv0)
                if q_lo >= q0 + FLASH_Q_SUB:
                    continue
                units.append((h, kv0, q_lo, q0 + FLASH_Q_SUB, kv0 + FLASH_KV_SUB - 1 > q_lo))
    return units


def _flash_kernel(qi_ref, ki_ref, qt_ref, k_ref, vt_ref, o_ref, m_ref, l_ref, acc_ref, *, t):
    pair = pl.program_id(1)
    qi = qi_ref[pair]
    ki = ki_ref[pair]

    @pl.when(ki == 0)
    def _():
        m_ref[...] = jnp.full_like(m_ref, _NEG)
        l_ref[...] = jnp.zeros_like(l_ref)
        acc_ref[...] = jnp.zeros_like(acc_ref)

    def step(diagonal):
        units = _flash_units(t, diagonal)

        def scores(unit):
            h, kv0, q_lo, q_hi, _ = unit
            hs = slice(h * MLA_HEAD_PAD, (h + 1) * MLA_HEAD_PAD)
            return _dot(k_ref[0, kv0:kv0 + FLASH_KV_SUB, hs], qt_ref[0, hs, q_lo:q_hi])

        ahead = [scores(u) for u in units[:FLASH_ISSUE_AHEAD]]
        for i, (h, kv0, q_lo, q_hi, masked) in enumerate(units):
            vs = slice(h * MLA_V, (h + 1) * MLA_V)
            qs = slice(q_lo, q_hi)
            st = ahead.pop(0)
            if i + FLASH_ISSUE_AHEAD < len(units):
                ahead.append(scores(units[i + FLASH_ISSUE_AHEAD]))
            if masked:
                kv_i = lax.broadcasted_iota(jnp.int32, st.shape, 0) + kv0
                q_i = lax.broadcasted_iota(jnp.int32, st.shape, 1) + q_lo
                st = jnp.where(kv_i <= q_i, st, _NEG)
            m_old = m_ref[h:h + 1, qs]
            m_new = jnp.maximum(m_old, jnp.max(st, axis=0, keepdims=True))
            alpha = jnp.exp2(m_old - m_new)
            p = jnp.exp2(st - m_new).astype(BF16)
            pv = _dot(vt_ref[0, h * MLA_VT_ROWS:(h + 1) * MLA_VT_ROWS, kv0:kv0 + FLASH_KV_SUB], p)
            l_ref[h:h + 1, qs] = alpha * l_ref[h:h + 1, qs] + pv[MLA_V:MLA_V + 1, :]
            m_ref[h:h + 1, qs] = m_new
            acc_ref[vs, qs] = acc_ref[vs, qs] * alpha + pv[0:MLA_V, :]

    @pl.when(ki < qi)
    def _():
        step(False)

    @pl.when(ki == qi)
    def _():
        step(True)
        for h in range(MLA_HEADS):
            vs = slice(h * MLA_V, (h + 1) * MLA_V)
            o_ref[0, vs, :] = (acc_ref[vs, :] / l_ref[h:h + 1, :]).astype(o_ref.dtype)


def _flash(qt, k, vt, t):
    bsz, s, hp = k.shape
    tq = tk = t
    nq = s // t
    pairs = [(i, j) for i in range(nq) for j in range(i + 1)]
    qi_tab = jnp.asarray(np.asarray([p[0] for p in pairs], np.int32))
    ki_tab = jnp.asarray(np.asarray([p[1] for p in pairs], np.int32))
    grid_spec = pltpu.PrefetchScalarGridSpec(
        num_scalar_prefetch=2,
        grid=(bsz, len(pairs)),
        in_specs=[pl.BlockSpec((1, hp, tq), lambda b, p, qi, ki: (b, 0, qi[p])),
                  pl.BlockSpec((1, tk, hp), lambda b, p, qi, ki: (b, ki[p], 0)),
                  pl.BlockSpec((1, MLA_HEADS * MLA_VT_ROWS, tk), lambda b, p, qi, ki: (b, 0, ki[p]))],
        out_specs=pl.BlockSpec((1, MLA_DIM, tq), lambda b, p, qi, ki: (b, 0, qi[p])),
        scratch_shapes=[pltpu.VMEM((MLA_HEADS, tq), F32), pltpu.VMEM((MLA_HEADS, tq), F32),
                        pltpu.VMEM((MLA_DIM, tq), F32)],
    )
    return pl.pallas_call(
        functools.partial(_flash_kernel, t=t),
        grid_spec=grid_spec,
        out_shape=jax.ShapeDtypeStruct((bsz, MLA_DIM, s), BF16),
        compiler_params=_cparams(("parallel", "arbitrary")),
        name="mla_flash",
    )(qi_tab, ki_tab, qt, k, vt)


def _ffn_kernel(h_ref, ya_ref, yb_ref, yct_ref, woa_ref, wob_ref, woc_ref, fg_ref, wg_ref, wu_ref, wd_ref, *rest,
                final, tm):
    if final:
        fin_ref, o_ref, h1_ref, hn_ref = rest
    else:
        o_ref, h1_ref, hn_ref = rest
    h1 = (h_ref[...] + _dot(ya_ref[...], woa_ref[...]) + _dot(yb_ref[...], wob_ref[...])
          + _dot_tn(yct_ref[0], woc_ref[...]))
    h1_ref[...] = h1
    hn_ref[...] = _rms(h1, fg_ref[...]).astype(BF16)

    def gate_up(r):
        x = hn_ref[r * MM_SUB_ROWS:(r + 1) * MM_SUB_ROWS, :]
        return _dot(x, wg_ref[...]), _dot(x, wu_ref[...])

    nsub = tm // MM_SUB_ROWS
    nxt = gate_up(0)
    for r in range(nsub):
        rows = slice(r * MM_SUB_ROWS, (r + 1) * MM_SUB_ROWS)
        gate, up = nxt
        if r + 1 < nsub:
            nxt = gate_up(r + 1)
        act = (gate * jax.nn.sigmoid(gate) * up).astype(BF16)
        out = h1_ref[rows, :] + _dot(act, wd_ref[...])
        if final:
            out = _rms(out, fin_ref[...])
        o_ref[rows, :] = out


def _ffn(h, ya, yb, yct, wo, fg, wgu, wd, fin, layer, tm):
    t, d = h.shape
    dff = wd.shape[1]
    per = yct.shape[2] // tm
    row = lambda n: pl.BlockSpec((tm, n), lambda i: (i, 0))
    once = pl.Buffered(1)
    final = fin is not None
    in_specs = [row(d), row(RW_DIM), row(RET_DIM),
                pl.BlockSpec((1, MLA_DIM, tm), lambda i: (i // per, 0, i % per)),
                pl.BlockSpec((None, RW_DIM, d), lambda i: (layer, 0, 0), pipeline_mode=once),
                pl.BlockSpec((None, RET_DIM, d), lambda i: (layer, 1, 0), pipeline_mode=once),
                pl.BlockSpec((None, MLA_DIM, d), lambda i: (layer, 1, 0), pipeline_mode=once),
                _layer_spec(fg.shape[1:], layer, 1),
                pl.BlockSpec((None, d, dff), lambda i: (layer, 0, 0), pipeline_mode=once),
                pl.BlockSpec((None, d, dff), lambda i: (layer, 0, 1), pipeline_mode=once),
                pl.BlockSpec((None, dff, d), lambda i: (layer, 0, 0), pipeline_mode=once)]
    args = [h, ya, yb, yct, wo, wo, wo, fg, wgu, wgu, wd]
    if final:
        in_specs.append(_const_spec(fin, 1))
        args.append(fin)
    return pl.pallas_call(
        functools.partial(_ffn_kernel, final=final, tm=tm),
        grid=(t // tm,),
        in_specs=in_specs,
        out_specs=row(d),
        out_shape=jax.ShapeDtypeStruct((t, d), F32),
        scratch_shapes=[pltpu.VMEM((tm, d), F32), pltpu.VMEM((tm, d), BF16)],
        compiler_params=_cparams(("parallel",)),
        name="outproj_ffn",
    )(*args)


def _relayout(w, m, transpose=False):
    out = "lki" if transpose else "lik"
    return jnp.einsum("lij,jk->" + out, w.astype(BF16), jnp.asarray(m, BF16), preferred_element_type=BF16)


def _rot_half_into(m, src, dst, n, sign=1.0):
    for j in range(n // 2):
        m[src + n // 2 + j, dst + j] = -sign
        m[src + j, dst + n // 2 + j] = sign


def _prep_in_weights(w_in):
    perm = np.zeros((2 * RET_DIM, 2 * RET_DIM), np.float32)
    for part in range(2):
        for hh in range(RET_HEADS):
            for half in range(2):
                for j in range(RET_FREQS):
                    src = part * RET_DIM + hh * RET_HEAD_DIM + half * RET_FREQS + j
                    dst = part * RET_DIM + half * (RET_DIM // 2) + hh * RET_FREQS + j
                    perm[src, dst] = 1.0
    mc = np.zeros((MLA_IN, MLA_C_PAD), np.float32)
    mc[np.arange(MLA_IN), np.arange(MLA_IN)] = 1.0
    _rot_half_into(mc, MLA_Q_RANK + MLA_KV_RANK, MLA_IN, MLA_ROPE)
    b0 = RW_IN
    c0 = RW_IN + RET_IN
    wa = w_in[..., :b0].astype(BF16)
    wb = jnp.concatenate([_relayout(w_in[..., b0:b0 + 2 * RET_DIM], perm), w_in[..., b0 + 2 * RET_DIM:c0].astype(BF16)],
                         axis=-1)
    wc = _relayout(w_in[..., c0:], mc)
    return wa, wb, wc


def _prep_mla_weights(w_q_up, w_kv_up):
    qd = MLA_NOPE + MLA_ROPE
    kvd = MLA_NOPE + MLA_V
    hp = MLA_HEADS * MLA_HEAD_PAD
    mq = np.zeros((MLA_HEADS * qd, hp), np.float32)
    mk = np.zeros((MLA_HEADS * kvd, hp), np.float32)
    mv = np.zeros((MLA_HEADS * kvd, MLA_DIM), np.float32)
    for h in range(MLA_HEADS):
        for c in range(qd):
            mq[h * qd + c, h * MLA_HEAD_PAD + c] = 1.0
        for c in range(MLA_NOPE):
            mk[h * kvd + c, h * MLA_HEAD_PAD + c] = 1.0
        for c in range(MLA_V):
            mv[h * kvd + MLA_NOPE + c, h * MLA_V + c] = 1.0
    return (_relayout(w_q_up, mq, transpose=True), _relayout(w_kv_up, mk),
            _relayout(w_kv_up, mv, transpose=True))


def _pad_rows(w, start, total):
    return jnp.pad(w, ((0, 0), (start, total - start - w.shape[1]), (0, 0)))


def _rope_table(positions):
    inv_ret = ROPE_BASE ** (-jnp.arange(0, RET_HEAD_DIM, 2, dtype=F32) / RET_HEAD_DIM)
    inv_mla = ROPE_BASE ** (-jnp.arange(0, MLA_ROPE, 2, dtype=F32) / MLA_ROPE)
    inv = jnp.concatenate([inv_ret, inv_ret, inv_mla, inv_mla, jnp.zeros((LANES - TAB_MLA_SIN - MLA_FREQS,), F32)])
    lane = np.arange(LANES)
    is_cos = (lane < TAB_RET_SIN) | ((lane >= TAB_MLA_COS) & (lane < TAB_MLA_SIN))
    ang = positions.astype(F32)[..., None] * inv
    return jnp.where(jnp.asarray(is_cos), jnp.cos(ang), jnp.sin(ang))


def kernel(x, positions, attn_norm, w_in, w_out, rw_mu, rw_w0, rw_w2, rw_a0, rw_a2, rw_g2, rw_k_k, rw_k_a, rw_r_k, rw_gn_w, rw_gn_b, rw_v0, rw_v1, rw_v2, mla_q_norm, mla_kv_norm, mla_w_q_up, mla_w_kv_up, ffn_norm, w_gate_up, w_down, final_norm):
    bsz, s, d = x.shape
    depth = w_in.shape[0]
    t = bsz * s
    tm = min(512, s)
    tm_proj = min(1024, s)
    hp = MLA_HEADS * MLA_HEAD_PAD

    tab = _rope_table(positions)
    wa, wb, wc = _prep_in_weights(w_in)
    wqt, wk, wvt = _prep_mla_weights(mla_w_q_up, mla_w_kv_up)
    wo, wgu, wd = w_out.astype(BF16), w_gate_up.astype(BF16), w_down.astype(BF16)
    v0 = jnp.concatenate([jnp.zeros((1, RW_DIM), F32), rw_v0], axis=0)
    vec = jnp.stack([rw_w0, rw_a0, rw_k_k, rw_k_a, rw_r_k.reshape(depth, RW_DIM), rw_gn_w, rw_gn_b, v0], axis=1)
    vec = jnp.pad(vec, ((0, 0), (0, _VEC_ROWS - vec.shape[1]), (0, 0)))
    w2p = _pad_rows(rw_w2, 0, RW_LOWRANK).astype(BF16)
    a2p = _pad_rows(rw_a2, RW_DECAY_RANK, RW_LOWRANK).astype(BF16)
    g2p = _pad_rows(rw_g2, RW_DECAY_RANK + RW_A_RANK, RW_LOWRANK).astype(BF16)
    v1p = jnp.pad(rw_v1, ((0, 0), (0, 0), (0, LANES - RW_V_RANK))).astype(BF16)
    v2p = _pad_rows(rw_v2, 0, LANES).astype(BF16)
    row3 = lambda a: a[:, None, :]
    mu, an, qn, kn, fn = row3(rw_mu), row3(attn_norm), row3(mla_q_norm), row3(mla_kv_norm), row3(ffn_norm)

    h = x.reshape(t, d)
    v_first = None
    for l in range(depth):
        p_a, p_b, p_c = _inproj(h, an, wa, wb, wc, l, tm_proj)
        res = _rwkv(p_a.reshape(bsz, s, RW_IN), v_first, mu, vec, w2p, a2p, g2p, v1p, v2p, l, bsz, min(256, s))
        if l == 0:
            y_a, v_first = res
        else:
            y_a = res
        y_b = _retention(p_b.reshape(bsz, s, RET_IN), tab, bsz)
        qt, k, vt = _mla_proj(p_c, tab, qn, kn, wqt, wk, wvt, l, tm_proj)
        y_ct = _flash(qt, k.reshape(bsz, s, hp), vt, min(1024, s))
        fin = final_norm[None, :] if l == depth - 1 else None
        h = _ffn(h, y_a.reshape(t, RW_DIM), y_b.reshape(t, RET_DIM), y_ct, wo, fn, wgu, wd, fin, l, tm)
    return h.reshape(bsz, s, d)
```

```python
import functools
import math

import jax
import jax.numpy as jnp
import numpy as np
from jax import lax
from jax.experimental import pallas as pl
from jax.experimental.pallas import tpu as pltpu

F32 = jnp.float32
BF16 = jnp.bfloat16

NORM_EPS = 1e-6
ROPE_BASE = 10000.0
LANES = 128

RW_HEADS = 4
RW_HEAD_DIM = 64
RW_DIM = RW_HEADS * RW_HEAD_DIM
RW_DECAY_RANK = 32
RW_A_RANK = 32
RW_V_RANK = 32
RW_GATE_RANK = 64
RW_GN_EPS = 64e-5
RW_IN = 3 * RW_DIM + RW_DECAY_RANK + RW_A_RANK + RW_GATE_RANK
RW_LOWRANK = RW_DECAY_RANK + RW_A_RANK + RW_GATE_RANK
RW_CHUNK = 64

RET_HEADS = 4
RET_HEAD_DIM = 64
RET_DIM = RET_HEADS * RET_HEAD_DIM
RET_CHUNK = 128
RET_IN = 4 * RET_DIM
RET_FREQS = RET_HEAD_DIM // 2

MLA_HEADS = 8
MLA_NOPE = 64
MLA_ROPE = 32
MLA_V = 64
MLA_Q_RANK = 384
MLA_KV_RANK = 256
MLA_DIM = MLA_HEADS * MLA_V
MLA_IN = MLA_Q_RANK + MLA_KV_RANK + MLA_ROPE
MLA_HEAD_PAD = 128
MLA_C_PAD = 768
MLA_FREQS = MLA_ROPE // 2
MLA_VT_ROWS = MLA_V + 16

TAB_RET_COS = 0
TAB_RET_SIN = RET_FREQS
TAB_MLA_COS = 2 * RET_FREQS
TAB_MLA_SIN = 2 * RET_FREQS + MLA_FREQS

V7X_VMEM_LIMIT_BYTES = 56 * 1024 * 1024


MM_SUB_ROWS = 256


def _cparams(semantics):
    return pltpu.CompilerParams(dimension_semantics=semantics, vmem_limit_bytes=V7X_VMEM_LIMIT_BYTES)


def _dot(a, b):
    return jnp.dot(a, b, preferred_element_type=F32)


def _dot_nt(a, b):
    return lax.dot_general(a, b, (((1,), (1,)), ((), ())), preferred_element_type=F32)


def _dot_tn(a, b):
    return lax.dot_general(a, b, (((0,), (0,)), ((), ())), preferred_element_type=F32)


def _split_dot(x, m_bf16, passes):
    acc = None
    rem = x
    for _ in range(passes):
        piece = rem.astype(BF16)
        part = _dot(piece, m_bf16)
        acc = part if acc is None else acc + part
        rem = rem - piece.astype(F32)
    return acc


def _split_dot_left(m_bf16, x):
    acc = None
    rem = x
    for _ in range(3):
        piece = rem.astype(BF16)
        part = _dot(m_bf16, piece)
        acc = part if acc is None else acc + part
        rem = rem - piece.astype(F32)
    return acc


def _rms(x, g):
    return x * lax.rsqrt(jnp.mean(x * x, axis=-1, keepdims=True) + NORM_EPS) * g


def _layer_spec(shape, layer, ngrid):
    zeros = (0,) * len(shape)
    if ngrid == 1:
        return pl.BlockSpec((None,) + tuple(shape), lambda i: (layer,) + zeros)
    return pl.BlockSpec((None,) + tuple(shape), lambda i, j: (layer,) + zeros)


def _const_spec(a, ngrid):
    zeros = (0,) * a.ndim
    if ngrid == 1:
        return pl.BlockSpec(a.shape, lambda i: zeros)
    return pl.BlockSpec(a.shape, lambda i, j: zeros)


def _inproj_kernel(x_ref, g_ref, wa_ref, wb_ref, wc_ref, pa_ref, pb_ref, pc_ref):
    for r in range(x_ref.shape[0] // MM_SUB_ROWS):
        rows = slice(r * MM_SUB_ROWS, (r + 1) * MM_SUB_ROWS)
        hn = _rms(x_ref[rows, :], g_ref[...]).astype(BF16)
        pa_ref[rows, :] = _dot(hn, wa_ref[...])
        pb_ref[rows, :] = _dot(hn, wb_ref[...]).astype(pb_ref.dtype)
        pc_ref[rows, :] = _dot(hn, wc_ref[...]).astype(pc_ref.dtype)


def _inproj(x, g, wa, wb, wc, layer, tm):
    t, d = x.shape
    row = lambda n: pl.BlockSpec((tm, n), lambda i: (i, 0))
    ws = (wa, wb, wc)
    return pl.pallas_call(
        _inproj_kernel,
        grid=(t // tm,),
        in_specs=[row(d), _layer_spec(g.shape[1:], layer, 1)] + [_layer_spec(w.shape[1:], layer, 1) for w in ws],
        out_specs=[row(w.shape[2]) for w in ws],
        out_shape=[jax.ShapeDtypeStruct((t, w.shape[2]), dt) for w, dt in zip(ws, (F32, BF16, BF16))],
        compiler_params=_cparams(("parallel",)),
        name="inproj",
    )(x, g, wa, wb, wc)


_VEC_ROWS = 16
(_V_W0, _V_A0, _V_KK, _V_KA, _V_RK, _V_GNW, _V_GNB, _V_V0) = range(8)


def _rwkv_kernel(*refs, has_vres, nb, tb):
    if has_vres:
        (p_ref, vf_ref, mu_ref, vec_ref, w2_ref, a2_ref, g2_ref, v1_ref, v2_ref, bd_ref, tri_ref,
         o_ref, carry_ref, s_ref, r_s, k_s, v_s, lw_s, kk_s, ka_s, g_s, bon_s, y_s) = refs
    else:
        (p_ref, mu_ref, vec_ref, w2_ref, a2_ref, g2_ref, bd_ref, tri_ref,
         o_ref, vfo_ref, carry_ref, s_ref, r_s, k_s, v_s, lw_s, kk_s, ka_s, g_s, bon_s, y_s) = refs

    L = RW_CHUNK
    C = RW_DIM
    tstep = pl.program_id(1)

    @pl.when(tstep == 0)
    def _():
        carry_ref[...] = jnp.zeros_like(carry_ref)
        s_ref[...] = jnp.zeros_like(s_ref)

    vec = vec_ref[...]
    row = lambda i: vec[i:i + 1, :]
    bd = bd_ref[...]
    mu = mu_ref[...]

    def bdsum(x):
        return _split_dot(x, bd, 2)

    def prep(b, _):
        p = p_ref[b]
        shifted = pltpu.roll(p, 1, 0)
        first = lax.broadcasted_iota(jnp.int32, p.shape, 0) == 0
        p_prev = jnp.where(first, carry_ref[b], shifted)
        carry_ref[b] = p[tb - 1:tb, :]
        ps = p + (p_prev - p) * mu
        r = ps[:, 0:C]
        k = ps[:, C:2 * C]
        v = ps[:, 2 * C:3 * C]
        lr = ps[:, 3 * C:3 * C + RW_LOWRANK]
        z = row(_V_W0) + _dot(jnp.tanh(lr).astype(BF16), w2_ref[...])
        lw = -math.exp(-0.5) * jax.nn.sigmoid(z)
        a = jax.nn.sigmoid(row(_V_A0) + _dot(lr.astype(BF16), a2_ref[...]))
        g = _dot(jax.nn.sigmoid(lr).astype(BF16), g2_ref[...])
        if has_vres:
            lat = _dot(v.astype(BF16), v1_ref[...]).astype(BF16)
            v = v + (vf_ref[b] - v) * jax.nn.sigmoid(row(_V_V0) + _dot(lat, v2_ref[...]))
        else:
            vfo_ref[b] = v
        kk = k * row(_V_KK)
        kk = kk * lax.rsqrt(jnp.maximum(bdsum(kk * kk), 1e-24))
        k2 = k * (1.0 + (a - 1.0) * row(_V_KA))
        r_s[b] = r
        k_s[b] = k2
        v_s[b] = v
        lw_s[b] = lw
        kk_s[b] = kk
        ka_s[b] = kk * a
        g_s[b] = g
        bon_s[b] = bdsum(r * k2 * row(_V_RK)) * v
        return 0

    lax.fori_loop(0, nb, prep, 0)

    hl = RW_HEADS * L
    r_i = lax.broadcasted_iota(jnp.int32, (hl, C), 0)
    c_i = lax.broadcasted_iota(jnp.int32, (hl, C), 1)
    headmask = (r_i // L) == (c_i // RW_HEAD_DIM)
    t_i = lax.broadcasted_iota(jnp.int32, (L, hl), 0)
    s_i = lax.broadcasted_iota(jnp.int32, (L, hl), 1) % L
    strict = t_i > s_i
    incl = t_i >= s_i
    tri = tri_ref[...]

    def masked4(x):
        return jnp.where(headmask, jnp.concatenate([x] * RW_HEADS, axis=0), 0.0).astype(BF16)

    def chunk(j, _):
        sl = pl.ds(pl.multiple_of(j * L, L), L)
        bs = range(nb)
        r = [r_s[b, sl, :] for b in bs]
        k2 = [k_s[b, sl, :] for b in bs]
        v = [v_s[b, sl, :] for b in bs]
        lw = [lw_s[b, sl, :] for b in bs]
        kk = [kk_s[b, sl, :] for b in bs]
        ka = [ka_s[b, sl, :] for b in bs]
        c = [_split_dot_left(tri, lw[b]) for b in bs]
        c_last = [c[b][L - 1:L, :] for b in bs]
        sig = [0.5 * c_last[b] for b in bs]
        e_neg = [jnp.exp(sig[b] - c[b]) for b in bs]
        kt = [k2[b] * e_neg[b] for b in bs]
        bt = [ka[b] * e_neg[b] for b in bs]
        x = [jnp.concatenate([-(kk[b] * jnp.exp(c[b] - lw[b] - sig[b])), r[b] * jnp.exp(c[b] - sig[b])],
                             axis=0).astype(BF16) for b in bs]
        s_old = [s_ref[b] for b in bs]
        kbs_w = [jnp.concatenate([masked4(kt[b]), masked4(bt[b]), (s_old[b] * jnp.exp(sig[b])).astype(BF16)], axis=0)
                 for b in bs]
        abx = [_dot_nt(x[b], kbs_w[b]) for b in bs]
        ab = [abx[b][:, 0:2 * hl] for b in bs]
        xs = [abx[b][:, 2 * hl:2 * hl + C] for b in bs]
        a_k = [jnp.concatenate([jnp.where(strict, ab[b][0:L, 0:hl], 0.0), jnp.where(incl, ab[b][L:2 * L, 0:hl], 0.0)],
                               axis=0).astype(BF16) for b in bs]
        n_w = [jnp.where(strict, ab[b][0:L, hl:2 * hl], 0.0) for b in bs]
        a_rb = [jnp.where(incl, ab[b][L:2 * L, hl:2 * hl], 0.0).astype(BF16) for b in bs]
        av = [_dot(a_k[b], masked4(v[b])) for b in bs]
        u = [av[b][0:L] + xs[b][0:L] for b in bs]
        for i in range(6):
            n_b = [n_w[b].astype(BF16) for b in bs]
            if i < 5:
                nu = [_dot(n_b[b], jnp.concatenate([masked4(u[b]), masked4(n_w[b])], axis=1)) for b in bs]
                u = [u[b] + nu[b][:, 0:C] for b in bs]
                n_w = [nu[b][:, C:2 * C] for b in bs]
            else:
                u = [u[b] + _dot(n_b[b], masked4(u[b])) for b in bs]
        y = [av[b][L:2 * L] + xs[b][L:2 * L] + _dot(a_rb[b], masked4(u[b])) for b in bs]
        for b in bs:
            y_s[b, sl, :] = y[b]
        scale = [jnp.exp(c_last[b] - sig[b]) for b in bs]
        kbs = [jnp.concatenate([kt[b] * scale[b], bt[b] * scale[b]], axis=0).astype(BF16) for b in bs]
        vu = [jnp.concatenate([v[b], u[b]], axis=0).astype(BF16) for b in bs]
        upd = [_dot_tn(vu[b], kbs[b]) for b in bs]
        for b in bs:
            s_ref[b] = s_old[b] * jnp.exp(c_last[b]) + jnp.where(headmask, upd[b], 0.0)
        return 0

    lax.fori_loop(0, tb // L, chunk, 0)

    def post(b, _):
        y = y_s[b]
        mean = bdsum(y) * (1.0 / RW_HEAD_DIM)
        d = y - mean
        var = bdsum(d * d) * (1.0 / RW_HEAD_DIM)
        yn = d * lax.rsqrt(var + RW_GN_EPS) * row(_V_GNW) + row(_V_GNB)
        o_ref[b] = ((yn + bon_s[b]) * g_s[b]).astype(o_ref.dtype)
        return 0

    lax.fori_loop(0, nb, post, 0)


def _block_ones(n, blk):
    i = np.arange(n)
    return (i[:, None] // blk == i[None, :] // blk).astype(np.float32)


def _rwkv(p_a, v_first, mu, vec, w2p, a2p, g2p, v1p, v2p, layer, nb, tb):
    bsz, s, _ = p_a.shape
    has_vres = v_first is not None
    bd = jnp.asarray(_block_ones(RW_DIM, RW_HEAD_DIM), BF16)
    tri = jnp.asarray(np.tril(np.ones((RW_CHUNK, RW_CHUNK), np.float32)), BF16)
    blk = lambda n: pl.BlockSpec((nb, tb, n), lambda i, j: (i, j, 0))
    lay = lambda a, l: _layer_spec(a.shape[1:], l, 2)
    common = [lay(mu, layer), lay(vec, layer), lay(w2p, layer), lay(a2p, layer), lay(g2p, layer)]
    consts = [_const_spec(bd, 2), _const_spec(tri, 2)]
    if has_vres:
        args = (p_a, v_first, mu, vec, w2p, a2p, g2p, v1p, v2p, bd, tri)
        in_specs = [blk(RW_IN), blk(RW_DIM)] + common + [lay(v1p, layer - 1), lay(v2p, layer - 1)] + consts
        out_specs = blk(RW_DIM)
        out_shape = jax.ShapeDtypeStruct((bsz, s, RW_DIM), BF16)
    else:
        args = (p_a, mu, vec, w2p, a2p, g2p, bd, tri)
        in_specs = [blk(RW_IN)] + common + consts
        out_specs = [blk(RW_DIM), blk(RW_DIM)]
        out_shape = [jax.ShapeDtypeStruct((bsz, s, RW_DIM), BF16), jax.ShapeDtypeStruct((bsz, s, RW_DIM), F32)]
    big = pltpu.VMEM((nb, tb, RW_DIM), F32)
    scratch = [pltpu.VMEM((nb, 1, RW_IN), F32), pltpu.VMEM((nb, RW_DIM, RW_DIM), F32)] + [big] * 9
    return pl.pallas_call(
        functools.partial(_rwkv_kernel, has_vres=has_vres, nb=nb, tb=tb),
        grid=(bsz // nb, s // tb),
        in_specs=in_specs,
        out_specs=out_specs,
        out_shape=out_shape,
        scratch_shapes=scratch,
        compiler_params=_cparams(("parallel", "arbitrary")),
        name="rwkv7",
    )(*args)


def _ret_tables():
    h, d, c = RET_HEADS, RET_HEAD_DIM, RET_CHUNK
    log_gamma = np.log(1.0 - 2.0 ** (-5.0 - np.arange(h, dtype=np.float64)))
    idx = np.arange(c, dtype=np.float64)
    diff = idx[:, None] - idx[None, :]
    dmask = np.where(diff >= 0, np.exp(log_gamma[:, None, None] * np.maximum(diff, 0.0)), 0.0)
    lane = np.arange(RET_DIM)
    head_qk = (lane % (RET_DIM // 2)) // (d // 2)
    head_v = lane // d
    qdec = np.exp(log_gamma[head_qk][None, :] * (idx[:, None] + 1.0))
    kdec = np.exp(log_gamma[head_qk][None, :] * (c - 1.0 - idx[:, None])) * d ** -0.5
    hm_qk = (np.arange(h)[:, None, None] == head_qk[None, None, :]) * np.ones((1, c, 1))
    hm_v = (np.arange(h)[:, None, None] == head_v[None, None, :]) * np.ones((1, c, 1))
    block = head_qk[:, None] == head_v[None, :]
    rdec = np.where(block, np.exp(log_gamma[head_qk] * c)[:, None], 0.0)
    pcos = np.zeros((LANES, RET_DIM // 2), np.float32)
    psin = np.zeros((LANES, RET_DIM // 2), np.float32)
    for hh in range(h):
        for j in range(RET_FREQS):
            pcos[TAB_RET_COS + j, hh * RET_FREQS + j] = 1.0
            psin[TAB_RET_SIN + j, hh * RET_FREQS + j] = 1.0
    f = lambda a: jnp.asarray(a, F32)
    return (f(dmask.reshape(h * c, c) * d ** -0.5), f(qdec), f(kdec), f(hm_qk.reshape(h * c, RET_DIM)),
            f(hm_v.reshape(h * c, RET_DIM)), f(block), f(rdec),
            jnp.asarray(np.concatenate([pcos, psin], axis=1), BF16))


def _ret_kernel(p_ref, tab_ref, dmask_ref, qdec_ref, kdec_ref, hmqk_ref, hmv_ref, block_ref, rdec_ref, pcs_ref,
                bd_ref, o_ref, r_ref, *, nb):
    c, dm = RET_CHUNK, RET_DIM
    half = dm // 2

    @pl.when(pl.program_id(0) == 0)
    def _():
        r_ref[...] = jnp.zeros_like(r_ref)

    bs = range(nb)
    cs = [_split_dot(tab_ref[b], pcs_ref[...], 3) for b in bs]
    p = [p_ref[b].astype(F32) for b in bs]

    def rope(x, t):
        cos, sin = t[:, :half], t[:, half:]
        x1, x2 = x[:, :half], x[:, half:]
        return jnp.concatenate([x1 * cos - x2 * sin, x2 * cos + x1 * sin], axis=1)

    q = [rope(p[b][:, 0:dm], cs[b]) for b in bs]
    k = [rope(p[b][:, dm:2 * dm], cs[b]) for b in bs]
    vb = [p[b][:, 2 * dm:3 * dm].astype(BF16) for b in bs]
    q_bd = [(jnp.concatenate([q[b]] * RET_HEADS, axis=0) * hmqk_ref[...]).astype(BF16) for b in bs]
    scores = [(_dot_nt(q_bd[b], k[b].astype(BF16)) * dmask_ref[...]).astype(BF16) for b in bs]
    o = [_dot(scores[b], vb[b]) * hmv_ref[...] for b in bs]
    r_old = [r_ref[b] for b in bs]
    cross = [_dot((q[b] * qdec_ref[...]).astype(BF16), r_old[b].astype(BF16)) for b in bs]
    upd = [_dot_tn((k[b] * kdec_ref[...]).astype(BF16), vb[b]) for b in bs]
    for b in bs:
        r_ref[b] = r_old[b] * rdec_ref[...] + upd[b] * block_ref[...]
    y = [o[b][0:c] + o[b][c:2 * c] + o[b][2 * c:3 * c] + o[b][3 * c:4 * c] + cross[b] for b in bs]
    ms = [_split_dot(y[b] * y[b], bd_ref[...], 2) * (1.0 / RET_HEAD_DIM) for b in bs]
    for b in bs:
        g = p[b][:, 3 * dm:4 * dm]
        o_ref[b] = (g * jax.nn.sigmoid(g) * y[b] * lax.rsqrt(ms[b] + NORM_EPS)).astype(o_ref.dtype)


def _retention(p_b, tab, nb):
    bsz, s, _ = p_b.shape
    consts = _ret_tables() + (jnp.asarray(_block_ones(RET_DIM, RET_HEAD_DIM), BF16),)
    blk = lambda n: pl.BlockSpec((nb, RET_CHUNK, n), lambda j, i: (i, j, 0))
    full = lambda a: pl.BlockSpec(a.shape, lambda j, i: (0,) * a.ndim)
    assert bsz == nb, "one batch group per time block keeps the state scratch simple"
    return pl.pallas_call(
        functools.partial(_ret_kernel, nb=nb),
        grid=(s // RET_CHUNK, bsz // nb),
        in_specs=[blk(RET_IN), blk(LANES)] + [full(a) for a in consts],
        out_specs=blk(RET_DIM),
        out_shape=jax.ShapeDtypeStruct((bsz, s, RET_DIM), BF16),
        scratch_shapes=[pltpu.VMEM((nb, RET_DIM, RET_DIM), F32)],
        compiler_params=_cparams(("arbitrary", "arbitrary")),
        name="retention",
    )(p_b, tab, *consts)


def _mla_tables():
    pk = np.zeros((LANES, MLA_HEAD_PAD), np.float32)
    pq = np.zeros((2 * MLA_FREQS, LANES), np.float32)
    for j in range(MLA_FREQS):
        for rep in range(2):
            pk[TAB_MLA_COS + j, rep * MLA_FREQS + j] = 1.0
            pk[TAB_MLA_SIN + j, MLA_ROPE + rep * MLA_FREQS + j] = 1.0
        pq[j, TAB_MLA_COS + j] = 1.0
        pq[MLA_FREQS + j, TAB_MLA_SIN + j] = 1.0
    place = np.zeros((MLA_HEAD_PAD, MLA_HEADS * MLA_HEAD_PAD), np.float32)
    for h in range(MLA_HEADS):
        for j in range(MLA_ROPE):
            place[j, h * MLA_HEAD_PAD + MLA_NOPE + j] = 1.0
            place[MLA_ROPE + j, h * MLA_HEAD_PAD + MLA_NOPE + j] = 1.0
    return jnp.asarray(pk, BF16), jnp.asarray(pq, BF16), jnp.asarray(place, BF16)


def _split_dot_nt(m_bf16, x, passes):
    acc = None
    rem = x
    for _ in range(passes):
        piece = rem.astype(BF16)
        part = _dot_nt(m_bf16, piece)
        acc = part if acc is None else acc + part
        rem = rem - piece.astype(F32)
    return acc


def _mla_proj_kernel(pc_ref, tab_ref, qn_ref, kn_ref, wqt_ref, wk_ref, wvt_ref, pk_ref, pq_ref, place_ref,
                     qt_ref, k_ref, vt_ref):
    pc = pc_ref[...].astype(F32)
    tab = tab_ref[0]
    ck = _split_dot(tab, pk_ref[...], 3)
    cs = _split_dot_nt(pq_ref[...], tab, 3)
    cos, sin = cs[0:MLA_FREQS], cs[MLA_FREQS:2 * MLA_FREQS]
    nq = _rms(pc[:, 0:MLA_Q_RANK], qn_ref[...]).astype(BF16)
    nkv = _rms(pc[:, MLA_Q_RANK:MLA_Q_RANK + MLA_KV_RANK], kn_ref[...]).astype(BF16)
    kr = (pc[:, MLA_Q_RANK + MLA_KV_RANK:MLA_C_PAD] * ck).astype(BF16)
    qa = _dot_nt(wqt_ref[...], nq) * ((MLA_NOPE + MLA_ROPE) ** -0.5 * math.log2(math.e))
    zeros = jnp.zeros((MLA_HEAD_PAD - MLA_NOPE - MLA_ROPE, qa.shape[1]), qt_ref.dtype)
    for h in range(MLA_HEADS):
        b0 = h * MLA_HEAD_PAD
        b1, b2, b3 = b0 + MLA_NOPE, b0 + MLA_NOPE + MLA_FREQS, b0 + MLA_NOPE + MLA_ROPE
        x1, x2 = qa[b1:b2], qa[b2:b3]
        qt_ref[0, b0:b1, :] = qa[b0:b1].astype(qt_ref.dtype)
        qt_ref[0, b1:b2, :] = (x1 * cos - x2 * sin).astype(qt_ref.dtype)
        qt_ref[0, b2:b3, :] = (x2 * cos + x1 * sin).astype(qt_ref.dtype)
        qt_ref[0, b3:b0 + MLA_HEAD_PAD, :] = zeros
    k_ref[...] = (_dot(nkv, wk_ref[...]) + _dot(kr, place_ref[...])).astype(k_ref.dtype)
    vt = _dot_nt(wvt_ref[...], nkv).astype(vt_ref.dtype)
    ones = jnp.ones((MLA_VT_ROWS - MLA_V, vt.shape[1]), vt_ref.dtype)
    for h in range(MLA_HEADS):
        vt_ref[0, h * MLA_VT_ROWS:h * MLA_VT_ROWS + MLA_V, :] = vt[h * MLA_V:(h + 1) * MLA_V, :]
        vt_ref[0, h * MLA_VT_ROWS + MLA_V:(h + 1) * MLA_VT_ROWS, :] = ones


def _mla_proj(p_c, tab, qn, kn, wqt, wk, wvt, layer, tm):
    t = p_c.shape[0]
    bsz, s, _ = tab.shape
    per = s // tm
    row = lambda n: pl.BlockSpec((tm, n), lambda i: (i, 0))
    cols = lambda n: pl.BlockSpec((1, n, tm), lambda i: (i // per, 0, i % per))
    pk, pq, place = _mla_tables()
    hp = MLA_HEADS * MLA_HEAD_PAD
    lay = lambda a: _layer_spec(a.shape[1:], layer, 1)
    return pl.pallas_call(
        _mla_proj_kernel,
        grid=(t // tm,),
        in_specs=[row(MLA_C_PAD), pl.BlockSpec((1, tm, LANES), lambda i: (i // per, i % per, 0)),
                  lay(qn), lay(kn), lay(wqt), lay(wk), lay(wvt),
                  _const_spec(pk, 1), _const_spec(pq, 1), _const_spec(place, 1)],
        out_specs=[cols(hp), row(hp), cols(MLA_HEADS * MLA_VT_ROWS)],
        out_shape=[jax.ShapeDtypeStruct((bsz, hp, s), BF16), jax.ShapeDtypeStruct((t, hp), BF16),
                   jax.ShapeDtypeStruct((bsz, MLA_HEADS * MLA_VT_ROWS, s), BF16)],
        compiler_params=_cparams(("parallel",)),
        name="mla_proj",
    )(p_c, tab, qn, kn, wqt, wk, wvt, pk, pq, place)


_NEG = -1e30
FLASH_ISSUE_AHEAD = 4
FLASH_KV_SUB = 256
FLASH_Q_SUB = 512


def _flash_units(t, diagonal):
    units = []
    for q0 in range(0, t, FLASH_Q_SUB):
        for h in range(MLA_HEADS):
            for kv0 in range(0, t, FLASH_KV_SUB):
                if not diagonal:
                    units.append((h, kv0, q0, q0 + FLASH_Q_SUB, False))
                    continue
                q_lo = max(q0, kv0)
                if q_lo >= q0 + FLASH_Q_SUB:
                    continue
                units.append((h, kv0, q_lo, q0 + FLASH_Q_SUB, kv0 + FLASH_KV_SUB - 1 > q_lo))
    return units


def _flash_kernel(qi_ref, ki_ref, qt_ref, k_ref, vt_ref, o_ref, m_ref, l_ref, acc_ref, *, t):
    pair = pl.program_id(1)
    qi = qi_ref[pair]
    ki = ki_ref[pair]

    @pl.when(ki == 0)
    def _():
        m_ref[...] = jnp.full_like(m_ref, _NEG)
        l_ref[...] = jnp.zeros_like(l_ref)
        acc_ref[...] = jnp.zeros_like(acc_ref)

    def step(diagonal):
        units = _flash_units(t, diagonal)

        def scores(unit):
            h, kv0, q_lo, q_hi, _ = unit
            hs = slice(h * MLA_HEAD_PAD, (h + 1) * MLA_HEAD_PAD)
            return _dot(k_ref[0, kv0:kv0 + FLASH_KV_SUB, hs], qt_ref[0, hs, q_lo:q_hi])

        ahead = [scores(u) for u in units[:FLASH_ISSUE_AHEAD]]
        for i, (h, kv0, q_lo, q_hi, masked) in enumerate(units):
            vs = slice(h * MLA_V, (h + 1) * MLA_V)
            qs = slice(q_lo, q_hi)
            st = ahead.pop(0)
            if i + FLASH_ISSUE_AHEAD < len(units):
                ahead.append(scores(units[i + FLASH_ISSUE_AHEAD]))
            if masked:
                kv_i = lax.broadcasted_iota(jnp.int32, st.shape, 0) + kv0
                q_i = lax.broadcasted_iota(jnp.int32, st.shape, 1) + q_lo
                st = jnp.where(kv_i <= q_i, st, _NEG)
            m_old = m_ref[h:h + 1, qs]
            m_new = jnp.maximum(m_old, jnp.max(st, axis=0, keepdims=True))
            alpha = jnp.exp2(m_old - m_new)
            p = jnp.exp2(st - m_new).astype(BF16)
            pv = _dot(vt_ref[0, h * MLA_VT_ROWS:(h + 1) * MLA_VT_ROWS, kv0:kv0 + FLASH_KV_SUB], p)
            l_ref[h:h + 1, qs] = alpha * l_ref[h:h + 1, qs] + pv[MLA_V:MLA_V + 1, :]
            m_ref[h:h + 1, qs] = m_new
            acc_ref[vs, qs] = acc_ref[vs, qs] * alpha + pv[0:MLA_V, :]

    @pl.when(ki < qi)
    def _():
        step(False)

    @pl.when(ki == qi)
    def _():
        step(True)
        for h in range(MLA_HEADS):
            vs = slice(h * MLA_V, (h + 1) * MLA_V)
            o_ref[0, vs, :] = (acc_ref[vs, :] / l_ref[h:h + 1, :]).astype(o_ref.dtype)


def _flash(qt, k, vt, t):
    bsz, s, hp = k.shape
    tq = tk = t
    nq = s // t
    pairs = [(i, j) for i in range(nq) for j in range(i + 1)]
    qi_tab = jnp.asarray(np.asarray([p[0] for p in pairs], np.int32))
    ki_tab = jnp.asarray(np.asarray([p[1] for p in pairs], np.int32))
    grid_spec = pltpu.PrefetchScalarGridSpec(
        num_scalar_prefetch=2,
        grid=(bsz, len(pairs)),
        in_specs=[pl.BlockSpec((1, hp, tq), lambda b, p, qi, ki: (b, 0, qi[p])),
                  pl.BlockSpec((1, tk, hp), lambda b, p, qi, ki: (b, ki[p], 0)),
                  pl.BlockSpec((1, MLA_HEADS * MLA_VT_ROWS, tk), lambda b, p, qi, ki: (b, 0, ki[p]))],
        out_specs=pl.BlockSpec((1, MLA_DIM, tq), lambda b, p, qi, ki: (b, 0, qi[p])),
        scratch_shapes=[pltpu.VMEM((MLA_HEADS, tq), F32), pltpu.VMEM((MLA_HEADS, tq), F32),
                        pltpu.VMEM((MLA_DIM, tq), F32)],
    )
    return pl.pallas_call(
        functools.partial(_flash_kernel, t=t),
        grid_spec=grid_spec,
        out_shape=jax.ShapeDtypeStruct((bsz, MLA_DIM, s), BF16),
        compiler_params=_cparams(("parallel", "arbitrary")),
        name="mla_flash",
    )(qi_tab, ki_tab, qt, k, vt)


def _ffn_kernel(h_ref, ya_ref, yb_ref, yct_ref, woa_ref, wob_ref, woc_ref, fg_ref, wg_ref, wu_ref, wd_ref, *rest,
                final, tm):
    if final:
        fin_ref, o_ref, h1_ref, hn_ref = rest
    else:
        o_ref, h1_ref, hn_ref = rest
    h1 = (h_ref[...] + _dot(ya_ref[...], woa_ref[...]) + _dot(yb_ref[...], wob_ref[...])
          + _dot_tn(yct_ref[0], woc_ref[...]))
    h1_ref[...] = h1
    hn_ref[...] = _rms(h1, fg_ref[...]).astype(BF16)

    def gate_up(r):
        x = hn_ref[r * MM_SUB_ROWS:(r + 1) * MM_SUB_ROWS, :]
        return _dot(x, wg_ref[...]), _dot(x, wu_ref[...])

    nsub = tm // MM_SUB_ROWS
    nxt = gate_up(0)
    for r in range(nsub):
        rows = slice(r * MM_SUB_ROWS, (r + 1) * MM_SUB_ROWS)
        gate, up = nxt
        if r + 1 < nsub:
            nxt = gate_up(r + 1)
        act = (gate * jax.nn.sigmoid(gate) * up).astype(BF16)
        out = h1_ref[rows, :] + _dot(act, wd_ref[...])
        if final:
            out = _rms(out, fin_ref[...])
        o_ref[rows, :] = out


def _ffn(h, ya, yb, yct, wo, fg, wgu, wd, fin, layer, tm):
    t, d = h.shape
    dff = wd.shape[1]
    per = yct.shape[2] // tm
    row = lambda n: pl.BlockSpec((tm, n), lambda i: (i, 0))
    once = pl.Buffered(1)
    final = fin is not None
    in_specs = [row(d), row(RW_DIM), row(RET_DIM),
                pl.BlockSpec((1, MLA_DIM, tm), lambda i: (i // per, 0, i % per)),
                pl.BlockSpec((None, RW_DIM, d), lambda i: (layer, 0, 0), pipeline_mode=once),
                pl.BlockSpec((None, RET_DIM, d), lambda i: (layer, 1, 0), pipeline_mode=once),
                pl.BlockSpec((None, MLA_DIM, d), lambda i: (layer, 1, 0), pipeline_mode=once),
                _layer_spec(fg.shape[1:], layer, 1),
                pl.BlockSpec((None, d, dff), lambda i: (layer, 0, 0), pipeline_mode=once),
                pl.BlockSpec((None, d, dff), lambda i: (layer, 0, 1), pipeline_mode=once),
                pl.BlockSpec((None, dff, d), lambda i: (layer, 0, 0), pipeline_mode=once)]
    args = [h, ya, yb, yct, wo, wo, wo, fg, wgu, wgu, wd]
    if final:
        in_specs.append(_const_spec(fin, 1))
        args.append(fin)
    return pl.pallas_call(
        functools.partial(_ffn_kernel, final=final, tm=tm),
        grid=(t // tm,),
        in_specs=in_specs,
        out_specs=row(d),
        out_shape=jax.ShapeDtypeStruct((t, d), F32),
        scratch_shapes=[pltpu.VMEM((tm, d), F32), pltpu.VMEM((tm, d), BF16)],
        compiler_params=_cparams(("parallel",)),
        name="outproj_ffn",
    )(*args)


def _relayout(w, m, transpose=False):
    out = "lki" if transpose else "lik"
    return jnp.einsum("lij,jk->" + out, w.astype(BF16), jnp.asarray(m, BF16), preferred_element_type=BF16)


def _rot_half_into(m, src, dst, n, sign=1.0):
    for j in range(n // 2):
        m[src + n // 2 + j, dst + j] = -sign
        m[src + j, dst + n // 2 + j] = sign


def _prep_in_weights(w_in):
    perm = np.zeros((2 * RET_DIM, 2 * RET_DIM), np.float32)
    for part in range(2):
        for hh in range(RET_HEADS):
            for half in range(2):
                for j in range(RET_FREQS):
                    src = part * RET_DIM + hh * RET_HEAD_DIM + half * RET_FREQS + j
                    dst = part * RET_DIM + half * (RET_DIM // 2) + hh * RET_FREQS + j
                    perm[src, dst] = 1.0
    mc = np.zeros((MLA_IN, MLA_C_PAD), np.float32)
    mc[np.arange(MLA_IN), np.arange(MLA_IN)] = 1.0
    _rot_half_into(mc, MLA_Q_RANK + MLA_KV_RANK, MLA_IN, MLA_ROPE)
    b0 = RW_IN
    c0 = RW_IN + RET_IN
    wa = w_in[..., :b0].astype(BF16)
    wb = jnp.concatenate([_relayout(w_in[..., b0:b0 + 2 * RET_DIM], perm), w_in[..., b0 + 2 * RET_DIM:c0].astype(BF16)],
                         axis=-1)
    wc = _relayout(w_in[..., c0:], mc)
    return wa, wb, wc


def _prep_mla_weights(w_q_up, w_kv_up):
    qd = MLA_NOPE + MLA_ROPE
    kvd = MLA_NOPE + MLA_V
    hp = MLA_HEADS * MLA_HEAD_PAD
    mq = np.zeros((MLA_HEADS * qd, hp), np.float32)
    mk = np.zeros((MLA_HEADS * kvd, hp), np.float32)
    mv = np.zeros((MLA_HEADS * kvd, MLA_DIM), np.float32)
    for h in range(MLA_HEADS):
        for c in range(qd):
            mq[h * qd + c, h * MLA_HEAD_PAD + c] = 1.0
        for c in range(MLA_NOPE):
            mk[h * kvd + c, h * MLA_HEAD_PAD + c] = 1.0
        for c in range(MLA_V):
            mv[h * kvd + MLA_NOPE + c, h * MLA_V + c] = 1.0
    return (_relayout(w_q_up, mq, transpose=True), _relayout(w_kv_up, mk),
            _relayout(w_kv_up, mv, transpose=True))


def _pad_rows(w, start, total):
    return jnp.pad(w, ((0, 0), (start, total - start - w.shape[1]), (0, 0)))


def _rope_table(positions):
    inv_ret = ROPE_BASE ** (-jnp.arange(0, RET_HEAD_DIM, 2, dtype=F32) / RET_HEAD_DIM)
    inv_mla = ROPE_BASE ** (-jnp.arange(0, MLA_ROPE, 2, dtype=F32) / MLA_ROPE)
    inv = jnp.concatenate([inv_ret, inv_ret, inv_mla, inv_mla, jnp.zeros((LANES - TAB_MLA_SIN - MLA_FREQS,), F32)])
    lane = np.arange(LANES)
    is_cos = (lane < TAB_RET_SIN) | ((lane >= TAB_MLA_COS) & (lane < TAB_MLA_SIN))
    ang = positions.astype(F32)[..., None] * inv
    return jnp.where(jnp.asarray(is_cos), jnp.cos(ang), jnp.sin(ang))


def kernel(x, positions, attn_norm, w_in, w_out, rw_mu, rw_w0, rw_w2, rw_a0, rw_a2, rw_g2, rw_k_k, rw_k_a, rw_r_k, rw_gn_w, rw_gn_b, rw_v0, rw_v1, rw_v2, mla_q_norm, mla_kv_norm, mla_w_q_up, mla_w_kv_up, ffn_norm, w_gate_up, w_down, final_norm):
    bsz, s, d = x.shape
    depth = w_in.shape[0]
    t = bsz * s
    tm = min(512, s)
    tm_proj = min(1024, s)
    hp = MLA_HEADS * MLA_HEAD_PAD

    tab = _rope_table(positions)
    wa, wb, wc = _prep_in_weights(w_in)
    wqt, wk, wvt = _prep_mla_weights(mla_w_q_up, mla_w_kv_up)
    wo, wgu, wd = w_out.astype(BF16), w_gate_up.astype(BF16), w_down.astype(BF16)
    v0 = jnp.concatenate([jnp.zeros((1, RW_DIM), F32), rw_v0], axis=0)
    vec = jnp.stack([rw_w0, rw_a0, rw_k_k, rw_k_a, rw_r_k.reshape(depth, RW_DIM), rw_gn_w, rw_gn_b, v0], axis=1)
    vec = jnp.pad(vec, ((0, 0), (0, _VEC_ROWS - vec.shape[1]), (0, 0)))
    w2p = _pad_rows(rw_w2, 0, RW_LOWRANK).astype(BF16)
    a2p = _pad_rows(rw_a2, RW_DECAY_RANK, RW_LOWRANK).astype(BF16)
    g2p = _pad_rows(rw_g2, RW_DECAY_RANK + RW_A_RANK, RW_LOWRANK).astype(BF16)
    v1p = jnp.pad(rw_v1, ((0, 0), (0, 0), (0, LANES - RW_V_RANK))).astype(BF16)
    v2p = _pad_rows(rw_v2, 0, LANES).astype(BF16)
    row3 = lambda a: a[:, None, :]
    mu, an, qn, kn, fn = row3(rw_mu), row3(attn_norm), row3(mla_q_norm), row3(mla_kv_norm), row3(ffn_norm)

    h = x.reshape(t, d)
    v_first = None
    for l in range(depth):
        p_a, p_b, p_c = _inproj(h, an, wa, wb, wc, l, tm_proj)
        res = _rwkv(p_a.reshape(bsz, s, RW_IN), v_first, mu, vec, w2p, a2p, g2p, v1p, v2p, l, bsz, min(256, s))
        if l == 0:
            y_a, v_first = res
        else:
            y_a = res
        y_b = _retention(p_b.reshape(bsz, s, RET_IN), tab, bsz)
        qt, k, vt = _mla_proj(p_c, tab, qn, kn, wqt, wk, wvt, l, tm_proj)
        y_ct = _flash(qt, k.reshape(bsz, s, hp), vt, min(1024, s))
        fin = final_norm[None, :] if l == depth - 1 else None
        h = _ffn(h, y_a.reshape(t, RW_DIM), y_b.reshape(t, RET_DIM), y_ct, wo, fn, wgu, wd, fin, l, tm)
    return h.reshape(bsz, s, d)
```

```python
import functools
import math

import jax
import jax.numpy as jnp
import numpy as np
from jax import lax
from jax.experimental import pallas as pl
from jax.experimental.pallas import tpu as pltpu

F32 = jnp.float32
BF16 = jnp.bfloat16

NORM_EPS = 1e-6
ROPE_BASE = 10000.0
LANES = 128

RW_HEADS = 4
RW_HEAD_DIM = 64
RW_DIM = RW_HEADS * RW_HEAD_DIM
RW_DECAY_RANK = 32
RW_A_RANK = 32
RW_V_RANK = 32
RW_GATE_RANK = 64
RW_GN_EPS = 64e-5
RW_IN = 3 * RW_DIM + RW_DECAY_RANK + RW_A_RANK + RW_GATE_RANK
RW_LOWRANK = RW_DECAY_RANK + RW_A_RANK + RW_GATE_RANK
RW_CHUNK = 64

RET_HEADS = 4
RET_HEAD_DIM = 64
RET_DIM = RET_HEADS * RET_HEAD_DIM
RET_CHUNK = 128
RET_IN = 4 * RET_DIM
RET_FREQS = RET_HEAD_DIM // 2

MLA_HEADS = 8
MLA_NOPE = 64
MLA_ROPE = 32
MLA_V = 64
MLA_Q_RANK = 384
MLA_KV_RANK = 256
MLA_DIM = MLA_HEADS * MLA_V
MLA_IN = MLA_Q_RANK + MLA_KV_RANK + MLA_ROPE
MLA_HEAD_PAD = 128
MLA_C_PAD = 768
MLA_FREQS = MLA_ROPE // 2
MLA_VT_ROWS = MLA_V + 16

TAB_RET_COS = 0
TAB_RET_SIN = RET_FREQS
TAB_MLA_COS = 2 * RET_FREQS
TAB_MLA_SIN = 2 * RET_FREQS + MLA_FREQS

V7X_VMEM_LIMIT_BYTES = 56 * 1024 * 1024


MM_SUB_ROWS = 256


def _cparams(semantics):
    return pltpu.CompilerParams(dimension_semantics=semantics, vmem_limit_bytes=V7X_VMEM_LIMIT_BYTES)


def _dot(a, b):
    return jnp.dot(a, b, preferred_element_type=F32)


def _dot_nt(a, b):
    return lax.dot_general(a, b, (((1,), (1,)), ((), ())), preferred_element_type=F32)


def _dot_tn(a, b):
    return lax.dot_general(a, b, (((0,), (0,)), ((), ())), preferred_element_type=F32)


def _split_dot(x, m_bf16, passes):
    acc = None
    rem = x
    for _ in range(passes):
        piece = rem.astype(BF16)
        part = _dot(piece, m_bf16)
        acc = part if acc is None else acc + part
        rem = rem - piece.astype(F32)
    return acc


def _split_dot_left(m_bf16, x):
    acc = None
    rem = x
    for _ in range(3):
        piece = rem.astype(BF16)
        part = _dot(m_bf16, piece)
        acc = part if acc is None else acc + part
        rem = rem - piece.astype(F32)
    return acc


def _rms(x, g):
    return x * lax.rsqrt(jnp.mean(x * x, axis=-1, keepdims=True) + NORM_EPS) * g


def _layer_spec(shape, layer, ngrid):
    zeros = (0,) * len(shape)
    if ngrid == 1:
        return pl.BlockSpec((None,) + tuple(shape), lambda i: (layer,) + zeros)
    return pl.BlockSpec((None,) + tuple(shape), lambda i, j: (layer,) + zeros)


def _const_spec(a, ngrid):
    zeros = (0,) * a.ndim
    if ngrid == 1:
        return pl.BlockSpec(a.shape, lambda i: zeros)
    return pl.BlockSpec(a.shape, lambda i, j: zeros)


def _inproj_kernel(x_ref, g_ref, wa_ref, wb_ref, wc_ref, pa_ref, pb_ref, pc_ref):
    for r in range(x_ref.shape[0] // MM_SUB_ROWS):
        rows = slice(r * MM_SUB_ROWS, (r + 1) * MM_SUB_ROWS)
        hn = _rms(x_ref[rows, :], g_ref[...]).astype(BF16)
        pa_ref[rows, :] = _dot(hn, wa_ref[...])
        pb_ref[rows, :] = _dot(hn, wb_ref[...])
        pc_ref[rows, :] = _dot(hn, wc_ref[...])


def _inproj(x, g, wa, wb, wc, layer, tm):
    t, d = x.shape
    row = lambda n: pl.BlockSpec((tm, n), lambda i: (i, 0))
    ws = (wa, wb, wc)
    return pl.pallas_call(
        _inproj_kernel,
        grid=(t // tm,),
        in_specs=[row(d), _layer_spec(g.shape[1:], layer, 1)] + [_layer_spec(w.shape[1:], layer, 1) for w in ws],
        out_specs=[row(w.shape[2]) for w in ws],
        out_shape=[jax.ShapeDtypeStruct((t, w.shape[2]), F32) for w in ws],
        compiler_params=_cparams(("parallel",)),
        name="inproj",
    )(x, g, wa, wb, wc)


_VEC_ROWS = 16
(_V_W0, _V_A0, _V_KK, _V_KA, _V_RK, _V_GNW, _V_GNB, _V_V0) = range(8)


def _rwkv_kernel(*refs, has_vres, nb, tb):
    if has_vres:
        (p_ref, vf_ref, mu_ref, vec_ref, w2_ref, a2_ref, g2_ref, v1_ref, v2_ref, bd_ref, tri_ref,
         o_ref, carry_ref, s_ref, r_s, k_s, v_s, lw_s, kk_s, ka_s, g_s, bon_s, y_s) = refs
    else:
        (p_ref, mu_ref, vec_ref, w2_ref, a2_ref, g2_ref, bd_ref, tri_ref,
         o_ref, vfo_ref, carry_ref, s_ref, r_s, k_s, v_s, lw_s, kk_s, ka_s, g_s, bon_s, y_s) = refs

    L = RW_CHUNK
    C = RW_DIM
    tstep = pl.program_id(1)

    @pl.when(tstep == 0)
    def _():
        carry_ref[...] = jnp.zeros_like(carry_ref)
        s_ref[...] = jnp.zeros_like(s_ref)

    vec = vec_ref[...]
    row = lambda i: vec[i:i + 1, :]
    bd = bd_ref[...]
    mu = mu_ref[...]

    def bdsum(x):
        return _split_dot(x, bd, 2)

    def prep(b, _):
        p = p_ref[b]
        shifted = pltpu.roll(p, 1, 0)
        first = lax.broadcasted_iota(jnp.int32, p.shape, 0) == 0
        p_prev = jnp.where(first, carry_ref[b], shifted)
        carry_ref[b] = p[tb - 1:tb, :]
        ps = p + (p_prev - p) * mu
        r = ps[:, 0:C]
        k = ps[:, C:2 * C]
        v = ps[:, 2 * C:3 * C]
        lr = ps[:, 3 * C:3 * C + RW_LOWRANK]
        z = row(_V_W0) + _dot(jnp.tanh(lr).astype(BF16), w2_ref[...])
        lw = -math.exp(-0.5) * jax.nn.sigmoid(z)
        a = jax.nn.sigmoid(row(_V_A0) + _dot(lr.astype(BF16), a2_ref[...]))
        g = _dot(jax.nn.sigmoid(lr).astype(BF16), g2_ref[...])
        if has_vres:
            lat = _dot(v.astype(BF16), v1_ref[...]).astype(BF16)
            v = v + (vf_ref[b] - v) * jax.nn.sigmoid(row(_V_V0) + _dot(lat, v2_ref[...]))
        else:
            vfo_ref[b] = v
        kk = k * row(_V_KK)
        kk = kk * lax.rsqrt(jnp.maximum(bdsum(kk * kk), 1e-24))
        k2 = k * (1.0 + (a - 1.0) * row(_V_KA))
        r_s[b] = r
        k_s[b] = k2
        v_s[b] = v
        lw_s[b] = lw
        kk_s[b] = kk
        ka_s[b] = kk * a
        g_s[b] = g
        bon_s[b] = bdsum(r * k2 * row(_V_RK)) * v
        return 0

    lax.fori_loop(0, nb, prep, 0)

    hl = RW_HEADS * L
    r_i = lax.broadcasted_iota(jnp.int32, (hl, C), 0)
    c_i = lax.broadcasted_iota(jnp.int32, (hl, C), 1)
    headmask = (r_i // L) == (c_i // RW_HEAD_DIM)
    t_i = lax.broadcasted_iota(jnp.int32, (L, hl), 0)
    s_i = lax.broadcasted_iota(jnp.int32, (L, hl), 1) % L
    strict = t_i > s_i
    incl = t_i >= s_i
    tri = tri_ref[...]

    def masked4(x):
        return jnp.where(headmask, jnp.concatenate([x] * RW_HEADS, axis=0), 0.0).astype(BF16)

    def chunk(j, _):
        sl = pl.ds(pl.multiple_of(j * L, L), L)
        bs = range(nb)
        r = [r_s[b, sl, :] for b in bs]
        k2 = [k_s[b, sl, :] for b in bs]
        v = [v_s[b, sl, :] for b in bs]
        lw = [lw_s[b, sl, :] for b in bs]
        kk = [kk_s[b, sl, :] for b in bs]
        ka = [ka_s[b, sl, :] for b in bs]
        c = [_split_dot_left(tri, lw[b]) for b in bs]
        c_last = [c[b][L - 1:L, :] for b in bs]
        sig = [0.5 * c_last[b] for b in bs]
        e_neg = [jnp.exp(sig[b] - c[b]) for b in bs]
        kt = [k2[b] * e_neg[b] for b in bs]
        bt = [ka[b] * e_neg[b] for b in bs]
        x = [jnp.concatenate([-(kk[b] * jnp.exp(c[b] - lw[b] - sig[b])), r[b] * jnp.exp(c[b] - sig[b])],
                             axis=0).astype(BF16) for b in bs]
        s_old = [s_ref[b] for b in bs]
        kbs_w = [jnp.concatenate([masked4(kt[b]), masked4(bt[b]), (s_old[b] * jnp.exp(sig[b])).astype(BF16)], axis=0)
                 for b in bs]
        abx = [_dot_nt(x[b], kbs_w[b]) for b in bs]
        ab = [abx[b][:, 0:2 * hl] for b in bs]
        xs = [abx[b][:, 2 * hl:2 * hl + C] for b in bs]
        a_k = [jnp.concatenate([jnp.where(strict, ab[b][0:L, 0:hl], 0.0), jnp.where(incl, ab[b][L:2 * L, 0:hl], 0.0)],
                               axis=0).astype(BF16) for b in bs]
        n_w = [jnp.where(strict, ab[b][0:L, hl:2 * hl], 0.0) for b in bs]
        a_rb = [jnp.where(incl, ab[b][L:2 * L, hl:2 * hl], 0.0).astype(BF16) for b in bs]
        av = [_dot(a_k[b], masked4(v[b])) for b in bs]
        u = [av[b][0:L] + xs[b][0:L] for b in bs]
        for i in range(6):
            n_b = [n_w[b].astype(BF16) for b in bs]
            if i < 5:
                nu = [_dot(n_b[b], jnp.concatenate([masked4(u[b]), masked4(n_w[b])], axis=1)) for b in bs]
                u = [u[b] + nu[b][:, 0:C] for b in bs]
                n_w = [nu[b][:, C:2 * C] for b in bs]
            else:
                u = [u[b] + _dot(n_b[b], masked4(u[b])) for b in bs]
        y = [av[b][L:2 * L] + xs[b][L:2 * L] + _dot(a_rb[b], masked4(u[b])) for b in bs]
        for b in bs:
            y_s[b, sl, :] = y[b]
        scale = [jnp.exp(c_last[b] - sig[b]) for b in bs]
        kbs = [jnp.concatenate([kt[b] * scale[b], bt[b] * scale[b]], axis=0).astype(BF16) for b in bs]
        vu = [jnp.concatenate([v[b], u[b]], axis=0).astype(BF16) for b in bs]
        upd = [_dot_tn(vu[b], kbs[b]) for b in bs]
        for b in bs:
            s_ref[b] = s_old[b] * jnp.exp(c_last[b]) + jnp.where(headmask, upd[b], 0.0)
        return 0

    lax.fori_loop(0, tb // L, chunk, 0)

    bs = range(nb)
    mean = [bdsum(y_s[b]) * (1.0 / RW_HEAD_DIM) for b in bs]
    d = [y_s[b] - mean[b] for b in bs]
    var = [bdsum(d[b] * d[b]) * (1.0 / RW_HEAD_DIM) for b in bs]
    for b in bs:
        yn = d[b] * lax.rsqrt(var[b] + RW_GN_EPS) * row(_V_GNW) + row(_V_GNB)
        o_ref[b] = ((yn + bon_s[b]) * g_s[b]).astype(o_ref.dtype)


def _block_ones(n, blk):
    i = np.arange(n)
    return (i[:, None] // blk == i[None, :] // blk).astype(np.float32)


def _rwkv(p_a, v_first, mu, vec, w2p, a2p, g2p, v1p, v2p, layer, nb, tb):
    bsz, s, _ = p_a.shape
    has_vres = v_first is not None
    bd = jnp.asarray(_block_ones(RW_DIM, RW_HEAD_DIM), BF16)
    tri = jnp.asarray(np.tril(np.ones((RW_CHUNK, RW_CHUNK), np.float32)), BF16)
    blk = lambda n: pl.BlockSpec((nb, tb, n), lambda i, j: (i, j, 0))
    lay = lambda a, l: _layer_spec(a.shape[1:], l, 2)
    common = [lay(mu, layer), lay(vec, layer), lay(w2p, layer), lay(a2p, layer), lay(g2p, layer)]
    consts = [_const_spec(bd, 2), _const_spec(tri, 2)]
    if has_vres:
        args = (p_a, v_first, mu, vec, w2p, a2p, g2p, v1p, v2p, bd, tri)
        in_specs = [blk(RW_IN), blk(RW_DIM)] + common + [lay(v1p, layer - 1), lay(v2p, layer - 1)] + consts
        out_specs = blk(RW_DIM)
        out_shape = jax.ShapeDtypeStruct((bsz, s, RW_DIM), BF16)
    else:
        args = (p_a, mu, vec, w2p, a2p, g2p, bd, tri)
        in_specs = [blk(RW_IN)] + common + consts
        out_specs = [blk(RW_DIM), blk(RW_DIM)]
        out_shape = [jax.ShapeDtypeStruct((bsz, s, RW_DIM), BF16), jax.ShapeDtypeStruct((bsz, s, RW_DIM), F32)]
    big = pltpu.VMEM((nb, tb, RW_DIM), F32)
    scratch = [pltpu.VMEM((nb, 1, RW_IN), F32), pltpu.VMEM((nb, RW_DIM, RW_DIM), F32)] + [big] * 9
    return pl.pallas_call(
        functools.partial(_rwkv_kernel, has_vres=has_vres, nb=nb, tb=tb),
        grid=(bsz // nb, s // tb),
        in_specs=in_specs,
        out_specs=out_specs,
        out_shape=out_shape,
        scratch_shapes=scratch,
        compiler_params=_cparams(("parallel", "arbitrary")),
        name="rwkv7",
    )(*args)


def _ret_tables():
    h, d, c = RET_HEADS, RET_HEAD_DIM, RET_CHUNK
    log_gamma = np.log(1.0 - 2.0 ** (-5.0 - np.arange(h, dtype=np.float64)))
    idx = np.arange(c, dtype=np.float64)
    diff = idx[:, None] - idx[None, :]
    dmask = np.where(diff >= 0, np.exp(log_gamma[:, None, None] * np.maximum(diff, 0.0)), 0.0)
    lane = np.arange(RET_DIM)
    head_qk = (lane % (RET_DIM // 2)) // (d // 2)
    head_v = lane // d
    qdec = np.exp(log_gamma[head_qk][None, :] * (idx[:, None] + 1.0))
    kdec = np.exp(log_gamma[head_qk][None, :] * (c - 1.0 - idx[:, None])) * d ** -0.5
    hm_qk = (np.arange(h)[:, None, None] == head_qk[None, None, :]) * np.ones((1, c, 1))
    hm_v = (np.arange(h)[:, None, None] == head_v[None, None, :]) * np.ones((1, c, 1))
    block = head_qk[:, None] == head_v[None, :]
    rdec = np.where(block, np.exp(log_gamma[head_qk] * c)[:, None], 0.0)
    pcos = np.zeros((LANES, RET_DIM // 2), np.float32)
    psin = np.zeros((LANES, RET_DIM // 2), np.float32)
    for hh in range(h):
        for j in range(RET_FREQS):
            pcos[TAB_RET_COS + j, hh * RET_FREQS + j] = 1.0
            psin[TAB_RET_SIN + j, hh * RET_FREQS + j] = 1.0
    f = lambda a: jnp.asarray(a, F32)
    return (f(dmask.reshape(h * c, c) * d ** -0.5), f(qdec), f(kdec), f(hm_qk.reshape(h * c, RET_DIM)),
            f(hm_v.reshape(h * c, RET_DIM)), f(block), f(rdec),
            jnp.asarray(np.concatenate([pcos, psin], axis=1), BF16))


def _ret_kernel(p_ref, tab_ref, dmask_ref, qdec_ref, kdec_ref, hmqk_ref, hmv_ref, block_ref, rdec_ref, pcs_ref,
                bd_ref, o_ref, r_ref, *, nb):
    c, dm = RET_CHUNK, RET_DIM
    half = dm // 2

    @pl.when(pl.program_id(0) == 0)
    def _():
        r_ref[...] = jnp.zeros_like(r_ref)

    bs = range(nb)
    cs = [_split_dot(tab_ref[b], pcs_ref[...], 3) for b in bs]
    p = [p_ref[b] for b in bs]

    def rope(x, t):
        cos, sin = t[:, :half], t[:, half:]
        x1, x2 = x[:, :half], x[:, half:]
        return jnp.concatenate([x1 * cos - x2 * sin, x2 * cos + x1 * sin], axis=1)

    q = [rope(p[b][:, 0:dm], cs[b]) for b in bs]
    k = [rope(p[b][:, dm:2 * dm], cs[b]) for b in bs]
    vb = [p[b][:, 2 * dm:3 * dm].astype(BF16) for b in bs]
    q_bd = [(jnp.concatenate([q[b]] * RET_HEADS, axis=0) * hmqk_ref[...]).astype(BF16) for b in bs]
    scores = [(_dot_nt(q_bd[b], k[b].astype(BF16)) * dmask_ref[...]).astype(BF16) for b in bs]
    o = [_dot(scores[b], vb[b]) * hmv_ref[...] for b in bs]
    r_old = [r_ref[b] for b in bs]
    cross = [_dot((q[b] * qdec_ref[...]).astype(BF16), r_old[b].astype(BF16)) for b in bs]
    upd = [_dot_tn((k[b] * kdec_ref[...]).astype(BF16), vb[b]) for b in bs]
    for b in bs:
        r_ref[b] = r_old[b] * rdec_ref[...] + upd[b] * block_ref[...]
    y = [o[b][0:c] + o[b][c:2 * c] + o[b][2 * c:3 * c] + o[b][3 * c:4 * c] + cross[b] for b in bs]
    ms = [_split_dot(y[b] * y[b], bd_ref[...], 2) * (1.0 / RET_HEAD_DIM) for b in bs]
    for b in bs:
        g = p[b][:, 3 * dm:4 * dm]
        o_ref[b] = (g * jax.nn.sigmoid(g) * y[b] * lax.rsqrt(ms[b] + NORM_EPS)).astype(o_ref.dtype)


def _retention(p_b, tab, nb):
    bsz, s, _ = p_b.shape
    consts = _ret_tables() + (jnp.asarray(_block_ones(RET_DIM, RET_HEAD_DIM), BF16),)
    blk = lambda n: pl.BlockSpec((nb, RET_CHUNK, n), lambda j, i: (i, j, 0))
    full = lambda a: pl.BlockSpec(a.shape, lambda j, i: (0,) * a.ndim)
    assert bsz == nb, "one batch group per time block keeps the state scratch simple"
    return pl.pallas_call(
        functools.partial(_ret_kernel, nb=nb),
        grid=(s // RET_CHUNK, bsz // nb),
        in_specs=[blk(RET_IN), blk(LANES)] + [full(a) for a in consts],
        out_specs=blk(RET_DIM),
        out_shape=jax.ShapeDtypeStruct((bsz, s, RET_DIM), BF16),
        scratch_shapes=[pltpu.VMEM((nb, RET_DIM, RET_DIM), F32)],
        compiler_params=_cparams(("arbitrary", "arbitrary")),
        name="retention",
    )(p_b, tab, *consts)


def _mla_tables():
    pk = np.zeros((LANES, MLA_HEAD_PAD), np.float32)
    pq = np.zeros((2 * MLA_FREQS, LANES), np.float32)
    for j in range(MLA_FREQS):
        for rep in range(2):
            pk[TAB_MLA_COS + j, rep * MLA_FREQS + j] = 1.0
            pk[TAB_MLA_SIN + j, MLA_ROPE + rep * MLA_FREQS + j] = 1.0
        pq[j, TAB_MLA_COS + j] = 1.0
        pq[MLA_FREQS + j, TAB_MLA_SIN + j] = 1.0
    place = np.zeros((MLA_HEAD_PAD, MLA_HEADS * MLA_HEAD_PAD), np.float32)
    for h in range(MLA_HEADS):
        for j in range(MLA_ROPE):
            place[j, h * MLA_HEAD_PAD + MLA_NOPE + j] = 1.0
            place[MLA_ROPE + j, h * MLA_HEAD_PAD + MLA_NOPE + j] = 1.0
    return jnp.asarray(pk, BF16), jnp.asarray(pq, BF16), jnp.asarray(place, BF16)


def _split_dot_nt(m_bf16, x, passes):
    acc = None
    rem = x
    for _ in range(passes):
        piece = rem.astype(BF16)
        part = _dot_nt(m_bf16, piece)
        acc = part if acc is None else acc + part
        rem = rem - piece.astype(F32)
    return acc


def _mla_proj_kernel(pc_ref, tab_ref, qn_ref, kn_ref, wqt_ref, wk_ref, wvt_ref, pk_ref, pq_ref, place_ref,
                     qt_ref, k_ref, vt_ref):
    pc = pc_ref[...]
    tab = tab_ref[0]
    ck = _split_dot(tab, pk_ref[...], 3)
    cs = _split_dot_nt(pq_ref[...], tab, 3)
    cos, sin = cs[0:MLA_FREQS], cs[MLA_FREQS:2 * MLA_FREQS]
    nq = _rms(pc[:, 0:MLA_Q_RANK], qn_ref[...]).astype(BF16)
    nkv = _rms(pc[:, MLA_Q_RANK:MLA_Q_RANK + MLA_KV_RANK], kn_ref[...]).astype(BF16)
    kr = (pc[:, MLA_Q_RANK + MLA_KV_RANK:MLA_C_PAD] * ck).astype(BF16)
    qa = _dot_nt(wqt_ref[...], nq) * ((MLA_NOPE + MLA_ROPE) ** -0.5 * math.log2(math.e))
    zeros = jnp.zeros((MLA_HEAD_PAD - MLA_NOPE - MLA_ROPE, qa.shape[1]), qt_ref.dtype)
    for h in range(MLA_HEADS):
        b0 = h * MLA_HEAD_PAD
        b1, b2, b3 = b0 + MLA_NOPE, b0 + MLA_NOPE + MLA_FREQS, b0 + MLA_NOPE + MLA_ROPE
        x1, x2 = qa[b1:b2], qa[b2:b3]
        qt_ref[0, b0:b1, :] = qa[b0:b1].astype(qt_ref.dtype)
        qt_ref[0, b1:b2, :] = (x1 * cos - x2 * sin).astype(qt_ref.dtype)
        qt_ref[0, b2:b3, :] = (x2 * cos + x1 * sin).astype(qt_ref.dtype)
        qt_ref[0, b3:b0 + MLA_HEAD_PAD, :] = zeros
    k_ref[...] = (_dot(nkv, wk_ref[...]) + _dot(kr, place_ref[...])).astype(k_ref.dtype)
    vt = _dot_nt(wvt_ref[...], nkv).astype(vt_ref.dtype)
    ones = jnp.ones((MLA_VT_ROWS - MLA_V, vt.shape[1]), vt_ref.dtype)
    for h in range(MLA_HEADS):
        vt_ref[0, h * MLA_VT_ROWS:h * MLA_VT_ROWS + MLA_V, :] = vt[h * MLA_V:(h + 1) * MLA_V, :]
        vt_ref[0, h * MLA_VT_ROWS + MLA_V:(h + 1) * MLA_VT_ROWS, :] = ones


def _mla_proj(p_c, tab, qn, kn, wqt, wk, wvt, layer, tm):
    t = p_c.shape[0]
    bsz, s, _ = tab.shape
    per = s // tm
    row = lambda n: pl.BlockSpec((tm, n), lambda i: (i, 0))
    cols = lambda n: pl.BlockSpec((1, n, tm), lambda i: (i // per, 0, i % per))
    pk, pq, place = _mla_tables()
    hp = MLA_HEADS * MLA_HEAD_PAD
    lay = lambda a: _layer_spec(a.shape[1:], layer, 1)
    return pl.pallas_call(
        _mla_proj_kernel,
        grid=(t // tm,),
        in_specs=[row(MLA_C_PAD), pl.BlockSpec((1, tm, LANES), lambda i: (i // per, i % per, 0)),
                  lay(qn), lay(kn), lay(wqt), lay(wk), lay(wvt),
                  _const_spec(pk, 1), _const_spec(pq, 1), _const_spec(place, 1)],
        out_specs=[cols(hp), row(hp), cols(MLA_HEADS * MLA_VT_ROWS)],
        out_shape=[jax.ShapeDtypeStruct((bsz, hp, s), BF16), jax.ShapeDtypeStruct((t, hp), BF16),
                   jax.ShapeDtypeStruct((bsz, MLA_HEADS * MLA_VT_ROWS, s), BF16)],
        compiler_params=_cparams(("parallel",)),
        name="mla_proj",
    )(p_c, tab, qn, kn, wqt, wk, wvt, pk, pq, place)


_NEG = -1e30
FLASH_ISSUE_AHEAD = 4
FLASH_KV_SUB = 256
FLASH_Q_SUB = 512


def _flash_units(t, diagonal):
    units = []
    for q0 in range(0, t, FLASH_Q_SUB):
        for kv0 in range(0, t, FLASH_KV_SUB):
            for h in range(MLA_HEADS):
                if not diagonal:
                    units.append((h, kv0, q0, q0 + FLASH_Q_SUB, False))
                    continue
                q_lo = max(q0, kv0)
                if q_lo >= q0 + FLASH_Q_SUB:
                    continue
                units.append((h, kv0, q_lo, q0 + FLASH_Q_SUB, kv0 + FLASH_KV_SUB - 1 > q_lo))
    return units


def _flash_kernel(qi_ref, ki_ref, qt_ref, k_ref, vt_ref, o_ref, m_ref, l_ref, acc_ref, *, t):
    pair = pl.program_id(1)
    qi = qi_ref[pair]
    ki = ki_ref[pair]

    @pl.when(ki == 0)
    def _():
        m_ref[...] = jnp.full_like(m_ref, _NEG)
        l_ref[...] = jnp.zeros_like(l_ref)
        acc_ref[...] = jnp.zeros_like(acc_ref)

    def step(diagonal):
        units = _flash_units(t, diagonal)

        def scores(unit):
            h, kv0, q_lo, q_hi, _ = unit
            hs = slice(h * MLA_HEAD_PAD, (h + 1) * MLA_HEAD_PAD)
            return _dot(k_ref[0, kv0:kv0 + FLASH_KV_SUB, hs], qt_ref[0, hs, q_lo:q_hi])

        ahead = [scores(u) for u in units[:FLASH_ISSUE_AHEAD]]
        for i, (h, kv0, q_lo, q_hi, masked) in enumerate(units):
            vs = slice(h * MLA_V, (h + 1) * MLA_V)
            qs = slice(q_lo, q_hi)
            st = ahead.pop(0)
            if i + FLASH_ISSUE_AHEAD < len(units):
                ahead.append(scores(units[i + FLASH_ISSUE_AHEAD]))
            if masked:
                kv_i = lax.broadcasted_iota(jnp.int32, st.shape, 0) + kv0
                q_i = lax.broadcasted_iota(jnp.int32, st.shape, 1) + q_lo
                st = jnp.where(kv_i <= q_i, st, _NEG)
            m_old = m_ref[h:h + 1, qs]
            m_new = jnp.maximum(m_old, jnp.max(st, axis=0, keepdims=True))
            alpha = jnp.exp2(m_old - m_new)
            p = jnp.exp2(st - m_new).astype(BF16)
            pv = _dot(vt_ref[0, h * MLA_VT_ROWS:(h + 1) * MLA_VT_ROWS, kv0:kv0 + FLASH_KV_SUB], p)
            l_ref[h:h + 1, qs] = alpha * l_ref[h:h + 1, qs] + pv[MLA_V:MLA_V + 1, :]
            m_ref[h:h + 1, qs] = m_new
            acc_ref[vs, qs] = acc_ref[vs, qs] * alpha + pv[0:MLA_V, :]

    @pl.when(ki < qi)
    def _():
        step(False)

    @pl.when(ki == qi)
    def _():
        step(True)
        for h in range(MLA_HEADS):
            vs = slice(h * MLA_V, (h + 1) * MLA_V)
            o_ref[0, vs, :] = (acc_ref[vs, :] / l_ref[h:h + 1, :]).astype(o_ref.dtype)


def _flash(qt, k, vt, t):
    bsz, s, hp = k.shape
    tq = tk = t
    nq = s // t
    pairs = [(i, j) for i in range(nq) for j in range(i + 1)]
    qi_tab = jnp.asarray(np.asarray([p[0] for p in pairs], np.int32))
    ki_tab = jnp.asarray(np.asarray([p[1] for p in pairs], np.int32))
    grid_spec = pltpu.PrefetchScalarGridSpec(
        num_scalar_prefetch=2,
        grid=(bsz, len(pairs)),
        in_specs=[pl.BlockSpec((1, hp, tq), lambda b, p, qi, ki: (b, 0, qi[p])),
                  pl.BlockSpec((1, tk, hp), lambda b, p, qi, ki: (b, ki[p], 0)),
                  pl.BlockSpec((1, MLA_HEADS * MLA_VT_ROWS, tk), lambda b, p, qi, ki: (b, 0, ki[p]))],
        out_specs=pl.BlockSpec((1, MLA_DIM, tq), lambda b, p, qi, ki: (b, 0, qi[p])),
        scratch_shapes=[pltpu.VMEM((MLA_HEADS, tq), F32), pltpu.VMEM((MLA_HEADS, tq), F32),
                        pltpu.VMEM((MLA_DIM, tq), F32)],
    )
    return pl.pallas_call(
        functools.partial(_flash_kernel, t=t),
        grid_spec=grid_spec,
        out_shape=jax.ShapeDtypeStruct((bsz, MLA_DIM, s), BF16),
        compiler_params=_cparams(("parallel", "arbitrary")),
        name="mla_flash",
    )(qi_tab, ki_tab, qt, k, vt)


def _ffn_kernel(h_ref, ya_ref, yb_ref, yct_ref, woa_ref, wob_ref, woc_ref, fg_ref, wg_ref, wu_ref, wd_ref, *rest,
                final, tm):
    if final:
        fin_ref, o_ref, h1_ref, hn_ref = rest
    else:
        o_ref, h1_ref, hn_ref = rest
    h1 = (h_ref[...] + _dot(ya_ref[...], woa_ref[...]) + _dot(yb_ref[...], wob_ref[...])
          + _dot_tn(yct_ref[0], woc_ref[...]))
    h1_ref[...] = h1
    hn_ref[...] = _rms(h1, fg_ref[...]).astype(BF16)

    def gate_up(r):
        x = hn_ref[r * MM_SUB_ROWS:(r + 1) * MM_SUB_ROWS, :]
        return _dot(x, wg_ref[...]), _dot(x, wu_ref[...])

    nsub = tm // MM_SUB_ROWS
    nxt = gate_up(0)
    for r in range(nsub):
        rows = slice(r * MM_SUB_ROWS, (r + 1) * MM_SUB_ROWS)
        gate, up = nxt
        if r + 1 < nsub:
            nxt = gate_up(r + 1)
        act = (gate * jax.nn.sigmoid(gate) * up).astype(BF16)
        out = h1_ref[rows, :] + _dot(act, wd_ref[...])
        if final:
            out = _rms(out, fin_ref[...])
        o_ref[rows, :] = out


def _ffn(h, ya, yb, yct, wo, fg, wgu, wd, fin, layer, tm):
    t, d = h.shape
    dff = wd.shape[1]
    per = yct.shape[2] // tm
    row = lambda n: pl.BlockSpec((tm, n), lambda i: (i, 0))
    once = pl.Buffered(1)
    final = fin is not None
    in_specs = [row(d), row(RW_DIM), row(RET_DIM),
                pl.BlockSpec((1, MLA_DIM, tm), lambda i: (i // per, 0, i % per)),
                pl.BlockSpec((None, RW_DIM, d), lambda i: (layer, 0, 0), pipeline_mode=once),
                pl.BlockSpec((None, RET_DIM, d), lambda i: (layer, 1, 0), pipeline_mode=once),
                pl.BlockSpec((None, MLA_DIM, d), lambda i: (layer, 1, 0), pipeline_mode=once),
                _layer_spec(fg.shape[1:], layer, 1),
                pl.BlockSpec((None, d, dff), lambda i: (layer, 0, 0), pipeline_mode=once),
                pl.BlockSpec((None, d, dff), lambda i: (layer, 0, 1), pipeline_mode=once),
                pl.BlockSpec((None, dff, d), lambda i: (layer, 0, 0), pipeline_mode=once)]
    args = [h, ya, yb, yct, wo, wo, wo, fg, wgu, wgu, wd]
    if final:
        in_specs.append(_const_spec(fin, 1))
        args.append(fin)
    return pl.pallas_call(
        functools.partial(_ffn_kernel, final=final, tm=tm),
        grid=(t // tm,),
        in_specs=in_specs,
        out_specs=row(d),
        out_shape=jax.ShapeDtypeStruct((t, d), F32),
        scratch_shapes=[pltpu.VMEM((tm, d), F32), pltpu.VMEM((tm, d), BF16)],
        compiler_params=_cparams(("parallel",)),
        name="outproj_ffn",
    )(*args)


def _relayout(w, m, transpose=False):
    out = "lki" if transpose else "lik"
    return jnp.einsum("lij,jk->" + out, w.astype(BF16), jnp.asarray(m, BF16), preferred_element_type=BF16)


def _rot_half_into(m, src, dst, n, sign=1.0):
    for j in range(n // 2):
        m[src + n // 2 + j, dst + j] = -sign
        m[src + j, dst + n // 2 + j] = sign


def _prep_in_weights(w_in):
    perm = np.zeros((2 * RET_DIM, 2 * RET_DIM), np.float32)
    for part in range(2):
        for hh in range(RET_HEADS):
            for half in range(2):
                for j in range(RET_FREQS):
                    src = part * RET_DIM + hh * RET_HEAD_DIM + half * RET_FREQS + j
                    dst = part * RET_DIM + half * (RET_DIM // 2) + hh * RET_FREQS + j
                    perm[src, dst] = 1.0
    mc = np.zeros((MLA_IN, MLA_C_PAD), np.float32)
    mc[np.arange(MLA_IN), np.arange(MLA_IN)] = 1.0
    _rot_half_into(mc, MLA_Q_RANK + MLA_KV_RANK, MLA_IN, MLA_ROPE)
    b0 = RW_IN
    c0 = RW_IN + RET_IN
    wa = w_in[..., :b0].astype(BF16)
    wb = jnp.concatenate([_relayout(w_in[..., b0:b0 + 2 * RET_DIM], perm), w_in[..., b0 + 2 * RET_DIM:c0].astype(BF16)],
                         axis=-1)
    wc = _relayout(w_in[..., c0:], mc)
    return wa, wb, wc


def _prep_mla_weights(w_q_up, w_kv_up):
    qd = MLA_NOPE + MLA_ROPE
    kvd = MLA_NOPE + MLA_V
    hp = MLA_HEADS * MLA_HEAD_PAD
    mq = np.zeros((MLA_HEADS * qd, hp), np.float32)
    mk = np.zeros((MLA_HEADS * kvd, hp), np.float32)
    mv = np.zeros((MLA_HEADS * kvd, MLA_DIM), np.float32)
    for h in range(MLA_HEADS):
        for c in range(qd):
            mq[h * qd + c, h * MLA_HEAD_PAD + c] = 1.0
        for c in range(MLA_NOPE):
            mk[h * kvd + c, h * MLA_HEAD_PAD + c] = 1.0
        for c in range(MLA_V):
            mv[h * kvd + MLA_NOPE + c, h * MLA_V + c] = 1.0
    return (_relayout(w_q_up, mq, transpose=True), _relayout(w_kv_up, mk),
            _relayout(w_kv_up, mv, transpose=True))


def _pad_rows(w, start, total):
    return jnp.pad(w, ((0, 0), (start, total - start - w.shape[1]), (0, 0)))


def _rope_table(positions):
    inv_ret = ROPE_BASE ** (-jnp.arange(0, RET_HEAD_DIM, 2, dtype=F32) / RET_HEAD_DIM)
    inv_mla = ROPE_BASE ** (-jnp.arange(0, MLA_ROPE, 2, dtype=F32) / MLA_ROPE)
    inv = jnp.concatenate([inv_ret, inv_ret, inv_mla, inv_mla, jnp.zeros((LANES - TAB_MLA_SIN - MLA_FREQS,), F32)])
    lane = np.arange(LANES)
    is_cos = (lane < TAB_RET_SIN) | ((lane >= TAB_MLA_COS) & (lane < TAB_MLA_SIN))
    ang = positions.astype(F32)[..., None] * inv
    return jnp.where(jnp.asarray(is_cos), jnp.cos(ang), jnp.sin(ang))


def kernel(x, positions, attn_norm, w_in, w_out, rw_mu, rw_w0, rw_w2, rw_a0, rw_a2, rw_g2, rw_k_k, rw_k_a, rw_r_k, rw_gn_w, rw_gn_b, rw_v0, rw_v1, rw_v2, mla_q_norm, mla_kv_norm, mla_w_q_up, mla_w_kv_up, ffn_norm, w_gate_up, w_down, final_norm):
    bsz, s, d = x.shape
    depth = w_in.shape[0]
    t = bsz * s
    tm = min(512, s)
    tm_proj = min(1024, s)
    hp = MLA_HEADS * MLA_HEAD_PAD

    tab = _rope_table(positions)
    wa, wb, wc = _prep_in_weights(w_in)
    wqt, wk, wvt = _prep_mla_weights(mla_w_q_up, mla_w_kv_up)
    wo, wgu, wd = w_out.astype(BF16), w_gate_up.astype(BF16), w_down.astype(BF16)
    v0 = jnp.concatenate([jnp.zeros((1, RW_DIM), F32), rw_v0], axis=0)
    vec = jnp.stack([rw_w0, rw_a0, rw_k_k, rw_k_a, rw_r_k.reshape(depth, RW_DIM), rw_gn_w, rw_gn_b, v0], axis=1)
    vec = jnp.pad(vec, ((0, 0), (0, _VEC_ROWS - vec.shape[1]), (0, 0)))
    w2p = _pad_rows(rw_w2, 0, RW_LOWRANK).astype(BF16)
    a2p = _pad_rows(rw_a2, RW_DECAY_RANK, RW_LOWRANK).astype(BF16)
    g2p = _pad_rows(rw_g2, RW_DECAY_RANK + RW_A_RANK, RW_LOWRANK).astype(BF16)
    v1p = jnp.pad(rw_v1, ((0, 0), (0, 0), (0, LANES - RW_V_RANK))).astype(BF16)
    v2p = _pad_rows(rw_v2, 0, LANES).astype(BF16)
    row3 = lambda a: a[:, None, :]
    mu, an, qn, kn, fn = row3(rw_mu), row3(attn_norm), row3(mla_q_norm), row3(mla_kv_norm), row3(ffn_norm)

    h = x.reshape(t, d)
    v_first = None
    for l in range(depth):
        p_a, p_b, p_c = _inproj(h, an, wa, wb, wc, l, tm_proj)
        res = _rwkv(p_a.reshape(bsz, s, RW_IN), v_first, mu, vec, w2p, a2p, g2p, v1p, v2p, l, bsz, min(256, s))
        if l == 0:
            y_a, v_first = res
        else:
            y_a = res
        y_b = _retention(p_b.reshape(bsz, s, RET_IN), tab, bsz)
        qt, k, vt = _mla_proj(p_c, tab, qn, kn, wqt, wk, wvt, l, tm_proj)
        y_ct = _flash(qt, k.reshape(bsz, s, hp), vt, min(1024, s))
        fin = final_norm[None, :] if l == depth - 1 else None
        h = _ffn(h, y_a.reshape(t, RW_DIM), y_b.reshape(t, RET_DIM), y_ct, wo, fn, wgu, wd, fin, l, tm)
    return h.reshape(bsz, s, d)
```

```python
import functools
import math

import jax
import jax.numpy as jnp
import numpy as np
from jax import lax
from jax.experimental import pallas as pl
from jax.experimental.pallas import tpu as pltpu

F32 = jnp.float32
BF16 = jnp.bfloat16

NORM_EPS = 1e-6
ROPE_BASE = 10000.0
LANES = 128

RW_HEADS = 4
RW_HEAD_DIM = 64
RW_DIM = RW_HEADS * RW_HEAD_DIM
RW_DECAY_RANK = 32
RW_A_RANK = 32
RW_V_RANK = 32
RW_GATE_RANK = 64
RW_GN_EPS = 64e-5
RW_IN = 3 * RW_DIM + RW_DECAY_RANK + RW_A_RANK + RW_GATE_RANK
RW_LOWRANK = RW_DECAY_RANK + RW_A_RANK + RW_GATE_RANK
RW_CHUNK = 64

RET_HEADS = 4
RET_HEAD_DIM = 64
RET_DIM = RET_HEADS * RET_HEAD_DIM
RET_CHUNK = 128
RET_IN = 4 * RET_DIM
RET_FREQS = RET_HEAD_DIM // 2

MLA_HEADS = 8
MLA_NOPE = 64
MLA_ROPE = 32
MLA_V = 64
MLA_Q_RANK = 384
MLA_KV_RANK = 256
MLA_DIM = MLA_HEADS * MLA_V
MLA_IN = MLA_Q_RANK + MLA_KV_RANK + MLA_ROPE
MLA_HEAD_PAD = 128
MLA_C_PAD = 768
MLA_FREQS = MLA_ROPE // 2
MLA_VT_ROWS = MLA_V + 16

TAB_RET_COS = 0
TAB_RET_SIN = RET_FREQS
TAB_MLA_COS = 2 * RET_FREQS
TAB_MLA_SIN = 2 * RET_FREQS + MLA_FREQS

V7X_VMEM_LIMIT_BYTES = 56 * 1024 * 1024


MM_SUB_ROWS = 256


def _cparams(semantics):
    return pltpu.CompilerParams(dimension_semantics=semantics, vmem_limit_bytes=V7X_VMEM_LIMIT_BYTES)


def _dot(a, b):
    return jnp.dot(a, b, preferred_element_type=F32)


def _dot_nt(a, b):
    return lax.dot_general(a, b, (((1,), (1,)), ((), ())), preferred_element_type=F32)


def _dot_tn(a, b):
    return lax.dot_general(a, b, (((0,), (0,)), ((), ())), preferred_element_type=F32)


def _split_dot(x, m_bf16, passes):
    acc = None
    rem = x
    for _ in range(passes):
        piece = rem.astype(BF16)
        part = _dot(piece, m_bf16)
        acc = part if acc is None else acc + part
        rem = rem - piece.astype(F32)
    return acc


def _split_dot_left(m_bf16, x):
    acc = None
    rem = x
    for _ in range(3):
        piece = rem.astype(BF16)
        part = _dot(m_bf16, piece)
        acc = part if acc is None else acc + part
        rem = rem - piece.astype(F32)
    return acc


def _lane_groups(x, width, picks):
    lane = lax.broadcasted_iota(jnp.int32, x.shape, 1)
    rolled = {0: x}
    out = jnp.zeros_like(x)
    for g, src in enumerate(picks):
        if src is None:
            continue
        shift = (g * width - src) % LANES
        if shift not in rolled:
            rolled[shift] = pltpu.roll(x, shift, 1)
        out = jnp.where((lane >= g * width) & (lane < (g + 1) * width), rolled[shift], out)
    return out


def _rms(x, g):
    return x * lax.rsqrt(jnp.mean(x * x, axis=-1, keepdims=True) + NORM_EPS) * g


def _layer_spec(shape, layer, ngrid):
    zeros = (0,) * len(shape)
    if ngrid == 1:
        return pl.BlockSpec((None,) + tuple(shape), lambda i: (layer,) + zeros)
    return pl.BlockSpec((None,) + tuple(shape), lambda i, j: (layer,) + zeros)


def _const_spec(a, ngrid):
    zeros = (0,) * a.ndim
    if ngrid == 1:
        return pl.BlockSpec(a.shape, lambda i: zeros)
    return pl.BlockSpec(a.shape, lambda i, j: zeros)


def _inproj_kernel(x_ref, g_ref, wa_ref, wb_ref, wc_ref, pa_ref, pb_ref, pc_ref):
    for r in range(x_ref.shape[0] // MM_SUB_ROWS):
        rows = slice(r * MM_SUB_ROWS, (r + 1) * MM_SUB_ROWS)
        hn = _rms(x_ref[rows, :], g_ref[...]).astype(BF16)
        pa_ref[rows, :] = _dot(hn, wa_ref[...])
        pb_ref[rows, :] = _dot(hn, wb_ref[...])
        pc_ref[rows, :] = _dot(hn, wc_ref[...])


def _inproj(x, g, wa, wb, wc, layer, tm):
    t, d = x.shape
    row = lambda n: pl.BlockSpec((tm, n), lambda i: (i, 0))
    ws = (wa, wb, wc)
    return pl.pallas_call(
        _inproj_kernel,
        grid=(t // tm,),
        in_specs=[row(d), _layer_spec(g.shape[1:], layer, 1)] + [_layer_spec(w.shape[1:], layer, 1) for w in ws],
        out_specs=[row(w.shape[2]) for w in ws],
        out_shape=[jax.ShapeDtypeStruct((t, w.shape[2]), F32) for w in ws],
        compiler_params=_cparams(("parallel",)),
        name="inproj",
    )(x, g, wa, wb, wc)


_VEC_ROWS = 16
(_V_W0, _V_A0, _V_KK, _V_KA, _V_RK, _V_GNW, _V_GNB, _V_V0) = range(8)


def _rwkv_kernel(*refs, has_vres, nb, tb):
    if has_vres:
        (p_ref, vf_ref, mu_ref, vec_ref, w2_ref, a2_ref, g2_ref, v1_ref, v2_ref, bd_ref, tri_ref,
         o_ref, carry_ref, s_ref, r_s, k_s, v_s, lw_s, kk_s, ka_s, g_s, bon_s, y_s) = refs
    else:
        (p_ref, mu_ref, vec_ref, w2_ref, a2_ref, g2_ref, bd_ref, tri_ref,
         o_ref, vfo_ref, carry_ref, s_ref, r_s, k_s, v_s, lw_s, kk_s, ka_s, g_s, bon_s, y_s) = refs

    L = RW_CHUNK
    C = RW_DIM
    tstep = pl.program_id(1)

    @pl.when(tstep == 0)
    def _():
        carry_ref[...] = jnp.zeros_like(carry_ref)
        s_ref[...] = jnp.zeros_like(s_ref)

    vec = vec_ref[...]
    row = lambda i: vec[i:i + 1, :]
    bd = bd_ref[...]
    mu = mu_ref[...]

    def bdsum(x):
        return _split_dot(x, bd, 2)

    def prep(b, _):
        p = p_ref[b]
        shifted = pltpu.roll(p, 1, 0)
        first = lax.broadcasted_iota(jnp.int32, p.shape, 0) == 0
        p_prev = jnp.where(first, carry_ref[b], shifted)
        carry_ref[b] = p[tb - 1:tb, :]
        ps = p + (p_prev - p) * mu
        r = ps[:, 0:C]
        k = ps[:, C:2 * C]
        v = ps[:, 2 * C:3 * C]
        lr = ps[:, 3 * C:3 * C + RW_LOWRANK]
        z = row(_V_W0) + _dot(jnp.tanh(lr).astype(BF16), w2_ref[...])
        lw = -math.exp(-0.5) * jax.nn.sigmoid(z)
        a = jax.nn.sigmoid(row(_V_A0) + _dot(lr.astype(BF16), a2_ref[...]))
        g = _dot(jax.nn.sigmoid(lr).astype(BF16), g2_ref[...])
        if has_vres:
            lat = _dot(v.astype(BF16), v1_ref[...]).astype(BF16)
            v = v + (vf_ref[b] - v) * jax.nn.sigmoid(row(_V_V0) + _dot(lat, v2_ref[...]))
        else:
            vfo_ref[b] = v
        kk = k * row(_V_KK)
        kk = kk * lax.rsqrt(jnp.maximum(bdsum(kk * kk), 1e-24))
        k2 = k * (1.0 + (a - 1.0) * row(_V_KA))
        r_s[b] = r
        k_s[b] = k2
        v_s[b] = v
        lw_s[b] = lw
        kk_s[b] = kk
        ka_s[b] = kk * a
        g_s[b] = g
        bon_s[b] = bdsum(r * k2 * row(_V_RK)) * v
        return 0

    for b in range(nb):
        prep(b, 0)

    hl = RW_HEADS * L
    r_i = lax.broadcasted_iota(jnp.int32, (hl, C), 0)
    c_i = lax.broadcasted_iota(jnp.int32, (hl, C), 1)
    headmask = (r_i // L) == (c_i // RW_HEAD_DIM)
    t_i = lax.broadcasted_iota(jnp.int32, (L, hl), 0)
    s_i = lax.broadcasted_iota(jnp.int32, (L, hl), 1) % L
    strict = t_i > s_i
    incl = t_i >= s_i
    tri = tri_ref[...]

    def masked4(x):
        return jnp.where(headmask, jnp.concatenate([x] * RW_HEADS, axis=0), 0.0).astype(BF16)

    def chunk(j, _):
        sl = pl.ds(pl.multiple_of(j * L, L), L)
        bs = range(nb)
        r = [r_s[b, sl, :] for b in bs]
        k2 = [k_s[b, sl, :] for b in bs]
        v = [v_s[b, sl, :] for b in bs]
        lw = [lw_s[b, sl, :] for b in bs]
        kk = [kk_s[b, sl, :] for b in bs]
        ka = [ka_s[b, sl, :] for b in bs]
        c = [_split_dot_left(tri, lw[b]) for b in bs]
        c_last = [c[b][L - 1:L, :] for b in bs]
        sig = [0.5 * c_last[b] for b in bs]
        e_neg = [jnp.exp(sig[b] - c[b]) for b in bs]
        kt = [k2[b] * e_neg[b] for b in bs]
        bt = [ka[b] * e_neg[b] for b in bs]
        x = [jnp.concatenate([-(kk[b] * jnp.exp(c[b] - lw[b] - sig[b])), r[b] * jnp.exp(c[b] - sig[b])],
                             axis=0).astype(BF16) for b in bs]
        s_old = [s_ref[b] for b in bs]
        kbs_w = [jnp.concatenate([masked4(kt[b]), masked4(bt[b]), (s_old[b] * jnp.exp(sig[b])).astype(BF16)], axis=0)
                 for b in bs]
        abx = [_dot_nt(x[b], kbs_w[b]) for b in bs]
        ab = [abx[b][:, 0:2 * hl] for b in bs]
        xs = [abx[b][:, 2 * hl:2 * hl + C] for b in bs]
        a_k = [jnp.concatenate([jnp.where(strict, ab[b][0:L, 0:hl], 0.0), jnp.where(incl, ab[b][L:2 * L, 0:hl], 0.0)],
                               axis=0).astype(BF16) for b in bs]
        n_w = [jnp.where(strict, ab[b][0:L, hl:2 * hl], 0.0) for b in bs]
        a_rb = [jnp.where(incl, ab[b][L:2 * L, hl:2 * hl], 0.0).astype(BF16) for b in bs]
        av = [_dot(a_k[b], masked4(v[b])) for b in bs]
        u = [av[b][0:L] + xs[b][0:L] for b in bs]
        for i in range(6):
            n_b = [n_w[b].astype(BF16) for b in bs]
            if i < 5:
                nu = [_dot(n_b[b], jnp.concatenate([masked4(u[b]), masked4(n_w[b])], axis=1)) for b in bs]
                u = [u[b] + nu[b][:, 0:C] for b in bs]
                n_w = [nu[b][:, C:2 * C] for b in bs]
            else:
                u = [u[b] + _dot(n_b[b], masked4(u[b])) for b in bs]
        y = [av[b][L:2 * L] + xs[b][L:2 * L] + _dot(a_rb[b], masked4(u[b])) for b in bs]
        for b in bs:
            y_s[b, sl, :] = y[b]
        scale = [jnp.exp(c_last[b] - sig[b]) for b in bs]
        kbs = [jnp.concatenate([kt[b] * scale[b], bt[b] * scale[b]], axis=0).astype(BF16) for b in bs]
        vu = [jnp.concatenate([v[b], u[b]], axis=0).astype(BF16) for b in bs]
        upd = [_dot_tn(vu[b], kbs[b]) for b in bs]
        for b in bs:
            s_ref[b] = s_old[b] * jnp.exp(c_last[b]) + jnp.where(headmask, upd[b], 0.0)
        return 0

    lax.fori_loop(0, tb // L, chunk, 0)

    bs = range(nb)
    mean = [bdsum(y_s[b]) * (1.0 / RW_HEAD_DIM) for b in bs]
    d = [y_s[b] - mean[b] for b in bs]
    var = [bdsum(d[b] * d[b]) * (1.0 / RW_HEAD_DIM) for b in bs]
    for b in bs:
        yn = d[b] * lax.rsqrt(var[b] + RW_GN_EPS) * row(_V_GNW) + row(_V_GNB)
        o_ref[b] = ((yn + bon_s[b]) * g_s[b]).astype(o_ref.dtype)


def _block_ones(n, blk):
    i = np.arange(n)
    return (i[:, None] // blk == i[None, :] // blk).astype(np.float32)


def _rwkv(p_a, v_first, mu, vec, w2p, a2p, g2p, v1p, v2p, layer, nb, tb):
    bsz, s, _ = p_a.shape
    has_vres = v_first is not None
    bd = jnp.asarray(_block_ones(RW_DIM, RW_HEAD_DIM), BF16)
    tri = jnp.asarray(np.tril(np.ones((RW_CHUNK, RW_CHUNK), np.float32)), BF16)
    blk = lambda n: pl.BlockSpec((nb, tb, n), lambda i, j: (i, j, 0))
    lay = lambda a, l: _layer_spec(a.shape[1:], l, 2)
    common = [lay(mu, layer), lay(vec, layer), lay(w2p, layer), lay(a2p, layer), lay(g2p, layer)]
    consts = [_const_spec(bd, 2), _const_spec(tri, 2)]
    if has_vres:
        args = (p_a, v_first, mu, vec, w2p, a2p, g2p, v1p, v2p, bd, tri)
        in_specs = [blk(RW_IN), blk(RW_DIM)] + common + [lay(v1p, layer - 1), lay(v2p, layer - 1)] + consts
        out_specs = blk(RW_DIM)
        out_shape = jax.ShapeDtypeStruct((bsz, s, RW_DIM), BF16)
    else:
        args = (p_a, mu, vec, w2p, a2p, g2p, bd, tri)
        in_specs = [blk(RW_IN)] + common + consts
        out_specs = [blk(RW_DIM), blk(RW_DIM)]
        out_shape = [jax.ShapeDtypeStruct((bsz, s, RW_DIM), BF16), jax.ShapeDtypeStruct((bsz, s, RW_DIM), F32)]
    big = pltpu.VMEM((nb, tb, RW_DIM), F32)
    scratch = [pltpu.VMEM((nb, 1, RW_IN), F32), pltpu.VMEM((nb, RW_DIM, RW_DIM), F32)] + [big] * 9
    return pl.pallas_call(
        functools.partial(_rwkv_kernel, has_vres=has_vres, nb=nb, tb=tb),
        grid=(bsz // nb, s // tb),
        in_specs=in_specs,
        out_specs=out_specs,
        out_shape=out_shape,
        scratch_shapes=scratch,
        compiler_params=_cparams(("parallel", "arbitrary")),
        name="rwkv7",
    )(*args)


def _ret_tables():
    h, d, c = RET_HEADS, RET_HEAD_DIM, RET_CHUNK
    log_gamma = np.log(1.0 - 2.0 ** (-5.0 - np.arange(h, dtype=np.float64)))
    idx = np.arange(c, dtype=np.float64)
    diff = idx[:, None] - idx[None, :]
    dmask = np.where(diff >= 0, np.exp(log_gamma[:, None, None] * np.maximum(diff, 0.0)), 0.0)
    lane = np.arange(RET_DIM)
    head_qk = (lane % (RET_DIM // 2)) // (d // 2)
    head_v = lane // d
    qdec = np.exp(log_gamma[head_qk][None, :] * (idx[:, None] + 1.0))
    kdec = np.exp(log_gamma[head_qk][None, :] * (c - 1.0 - idx[:, None])) * d ** -0.5
    hm_qk = (np.arange(h)[:, None, None] == head_qk[None, None, :]) * np.ones((1, c, 1))
    hm_v = (np.arange(h)[:, None, None] == head_v[None, None, :]) * np.ones((1, c, 1))
    block = head_qk[:, None] == head_v[None, :]
    rdec = np.where(block, np.exp(log_gamma[head_qk] * c)[:, None], 0.0)
    f = lambda a: jnp.asarray(a, F32)
    return (f(dmask.reshape(h * c, c) * d ** -0.5), f(qdec), f(kdec), f(hm_qk.reshape(h * c, RET_DIM)),
            f(hm_v.reshape(h * c, RET_DIM)), f(block), f(rdec))


def _ret_kernel(p_ref, tab_ref, dmask_ref, qdec_ref, kdec_ref, hmqk_ref, hmv_ref, block_ref, rdec_ref,
                bd_ref, o_ref, r_ref, *, nb):
    c, dm = RET_CHUNK, RET_DIM
    half = dm // 2

    @pl.when(pl.program_id(0) == 0)
    def _():
        r_ref[...] = jnp.zeros_like(r_ref)

    bs = range(nb)
    cs = [(_lane_groups(tab_ref[b], RET_FREQS, [TAB_RET_COS] * RET_HEADS),
           _lane_groups(tab_ref[b], RET_FREQS, [TAB_RET_SIN] * RET_HEADS)) for b in bs]
    p = [p_ref[b] for b in bs]

    def rope(x, t):
        cos, sin = t
        x1, x2 = x[:, :half], x[:, half:]
        return jnp.concatenate([x1 * cos - x2 * sin, x2 * cos + x1 * sin], axis=1)

    q = [rope(p[b][:, 0:dm], cs[b]) for b in bs]
    k = [rope(p[b][:, dm:2 * dm], cs[b]) for b in bs]
    vb = [p[b][:, 2 * dm:3 * dm].astype(BF16) for b in bs]
    q_bd = [(jnp.concatenate([q[b]] * RET_HEADS, axis=0) * hmqk_ref[...]).astype(BF16) for b in bs]
    scores = [(_dot_nt(q_bd[b], k[b].astype(BF16)) * dmask_ref[...]).astype(BF16) for b in bs]
    o = [_dot(scores[b], vb[b]) * hmv_ref[...] for b in bs]
    r_old = [r_ref[b] for b in bs]
    cross = [_dot((q[b] * qdec_ref[...]).astype(BF16), r_old[b].astype(BF16)) for b in bs]
    upd = [_dot_tn((k[b] * kdec_ref[...]).astype(BF16), vb[b]) for b in bs]
    for b in bs:
        r_ref[b] = r_old[b] * rdec_ref[...] + upd[b] * block_ref[...]
    y = [o[b][0:c] + o[b][c:2 * c] + o[b][2 * c:3 * c] + o[b][3 * c:4 * c] + cross[b] for b in bs]
    ms = [_split_dot(y[b] * y[b], bd_ref[...], 2) * (1.0 / RET_HEAD_DIM) for b in bs]
    for b in bs:
        g = p[b][:, 3 * dm:4 * dm]
        o_ref[b] = (g * jax.nn.sigmoid(g) * y[b] * lax.rsqrt(ms[b] + NORM_EPS)).astype(o_ref.dtype)


def _retention(p_b, tab, nb):
    bsz, s, _ = p_b.shape
    consts = _ret_tables() + (jnp.asarray(_block_ones(RET_DIM, RET_HEAD_DIM), BF16),)
    blk = lambda n: pl.BlockSpec((nb, RET_CHUNK, n), lambda j, i: (i, j, 0))
    full = lambda a: pl.BlockSpec(a.shape, lambda j, i: (0,) * a.ndim)
    assert bsz == nb, "one batch group per time block keeps the state scratch simple"
    return pl.pallas_call(
        functools.partial(_ret_kernel, nb=nb),
        grid=(s // RET_CHUNK, bsz // nb),
        in_specs=[blk(RET_IN), blk(LANES)] + [full(a) for a in consts],
        out_specs=blk(RET_DIM),
        out_shape=jax.ShapeDtypeStruct((bsz, s, RET_DIM), BF16),
        scratch_shapes=[pltpu.VMEM((nb, RET_DIM, RET_DIM), F32)],
        compiler_params=_cparams(("arbitrary", "arbitrary")),
        name="retention",
    )(p_b, tab, *consts)


def _mla_tables():
    pq = np.zeros((2 * MLA_FREQS, LANES), np.float32)
    for j in range(MLA_FREQS):
        pq[j, TAB_MLA_COS + j] = 1.0
        pq[MLA_FREQS + j, TAB_MLA_SIN + j] = 1.0
    return jnp.asarray(pq, BF16)


def _split_dot_nt(m_bf16, x, passes):
    acc = None
    rem = x
    for _ in range(passes):
        piece = rem.astype(BF16)
        part = _dot_nt(m_bf16, piece)
        acc = part if acc is None else acc + part
        rem = rem - piece.astype(F32)
    return acc


def _mla_proj_kernel(pc_ref, tab_ref, qn_ref, kn_ref, wqt_ref, wk_ref, wvt_ref, pq_ref,
                     qt_ref, k_ref, vt_ref):
    pc = pc_ref[...]
    tab = tab_ref[0]
    ck = _lane_groups(tab, MLA_FREQS, [None] * 4 + [TAB_MLA_COS] * 2 + [TAB_MLA_SIN] * 2)
    cs = _split_dot_nt(pq_ref[...], tab, 3)
    cos, sin = cs[0:MLA_FREQS], cs[MLA_FREQS:2 * MLA_FREQS]
    nq = _rms(pc[:, 0:MLA_Q_RANK], qn_ref[...]).astype(BF16)
    nkv = _rms(pc[:, MLA_Q_RANK:MLA_Q_RANK + MLA_KV_RANK], kn_ref[...]).astype(BF16)
    kr = pc[:, MLA_Q_RANK + MLA_KV_RANK:MLA_C_PAD] * ck
    kr = kr + pltpu.roll(kr, LANES - MLA_ROPE, 1)
    lane = lax.broadcasted_iota(jnp.int32, kr.shape, 1)
    kr = jnp.where((lane >= MLA_NOPE) & (lane < MLA_NOPE + MLA_ROPE), kr, 0.0)
    qa = _dot_nt(wqt_ref[...], nq) * ((MLA_NOPE + MLA_ROPE) ** -0.5 * math.log2(math.e))
    zeros = jnp.zeros((MLA_HEAD_PAD - MLA_NOPE - MLA_ROPE, qa.shape[1]), qt_ref.dtype)
    for h in range(MLA_HEADS):
        b0 = h * MLA_HEAD_PAD
        b1, b2, b3 = b0 + MLA_NOPE, b0 + MLA_NOPE + MLA_FREQS, b0 + MLA_NOPE + MLA_ROPE
        x1, x2 = qa[b1:b2], qa[b2:b3]
        qt_ref[0, b0:b1, :] = qa[b0:b1].astype(qt_ref.dtype)
        qt_ref[0, b1:b2, :] = (x1 * cos - x2 * sin).astype(qt_ref.dtype)
        qt_ref[0, b2:b3, :] = (x2 * cos + x1 * sin).astype(qt_ref.dtype)
        qt_ref[0, b3:b0 + MLA_HEAD_PAD, :] = zeros
    k_nope = _dot(nkv, wk_ref[...])
    for h in range(MLA_HEADS):
        hs = slice(h * MLA_HEAD_PAD, (h + 1) * MLA_HEAD_PAD)
        k_ref[:, hs] = (k_nope[:, hs] + kr).astype(k_ref.dtype)
    vt = _dot_nt(wvt_ref[...], nkv).astype(vt_ref.dtype)
    ones = jnp.ones((MLA_VT_ROWS - MLA_V, vt.shape[1]), vt_ref.dtype)
    for h in range(MLA_HEADS):
        vt_ref[0, h * MLA_VT_ROWS:h * MLA_VT_ROWS + MLA_V, :] = vt[h * MLA_V:(h + 1) * MLA_V, :]
        vt_ref[0, h * MLA_VT_ROWS + MLA_V:(h + 1) * MLA_VT_ROWS, :] = ones


def _mla_proj(p_c, tab, qn, kn, wqt, wk, wvt, layer, tm):
    t = p_c.shape[0]
    bsz, s, _ = tab.shape
    per = s // tm
    row = lambda n: pl.BlockSpec((tm, n), lambda i: (i, 0))
    cols = lambda n: pl.BlockSpec((1, n, tm), lambda i: (i // per, 0, i % per))
    pq = _mla_tables()
    hp = MLA_HEADS * MLA_HEAD_PAD
    lay = lambda a: _layer_spec(a.shape[1:], layer, 1)
    return pl.pallas_call(
        _mla_proj_kernel,
        grid=(t // tm,),
        in_specs=[row(MLA_C_PAD), pl.BlockSpec((1, tm, LANES), lambda i: (i // per, i % per, 0)),
                  lay(qn), lay(kn), lay(wqt), lay(wk), lay(wvt),
                  _const_spec(pq, 1)],
        out_specs=[cols(hp), row(hp), cols(MLA_HEADS * MLA_VT_ROWS)],
        out_shape=[jax.ShapeDtypeStruct((bsz, hp, s), BF16), jax.ShapeDtypeStruct((t, hp), BF16),
                   jax.ShapeDtypeStruct((bsz, MLA_HEADS * MLA_VT_ROWS, s), BF16)],
        compiler_params=_cparams(("parallel",)),
        name="mla_proj",
    )(p_c, tab, qn, kn, wqt, wk, wvt, pq)


_NEG = -1e30
FLASH_ISSUE_AHEAD = 4
FLASH_KV_SUB = 256
FLASH_Q_SUB = 512


def _flash_units(t, diagonal):
    units = []
    for q0 in range(0, t, FLASH_Q_SUB):
        for kv0 in range(0, t, FLASH_KV_SUB):
            for h in range(MLA_HEADS):
                if not diagonal:
                    units.append((h, kv0, q0, q0 + FLASH_Q_SUB, False))
                    continue
                q_lo = max(q0, kv0)
                if q_lo >= q0 + FLASH_Q_SUB:
                    continue
                units.append((h, kv0, q_lo, q0 + FLASH_Q_SUB, kv0 + FLASH_KV_SUB - 1 > q_lo))
    return units


def _flash_kernel(qi_ref, ki_ref, qt_ref, k_ref, vt_ref, o_ref, m_ref, l_ref, acc_ref, *, t):
    pair = pl.program_id(1)
    qi = qi_ref[pair]
    ki = ki_ref[pair]

    @pl.when(ki == 0)
    def _():
        m_ref[...] = jnp.full_like(m_ref, _NEG)
        l_ref[...] = jnp.zeros_like(l_ref)
        acc_ref[...] = jnp.zeros_like(acc_ref)

    def step(diagonal):
        units = _flash_units(t, diagonal)

        def scores(unit):
            h, kv0, q_lo, q_hi, _ = unit
            hs = slice(h * MLA_HEAD_PAD, (h + 1) * MLA_HEAD_PAD)
            return _dot(k_ref[0, kv0:kv0 + FLASH_KV_SUB, hs], qt_ref[0, hs, q_lo:q_hi])

        ahead = [scores(u) for u in units[:FLASH_ISSUE_AHEAD]]
        for i, (h, kv0, q_lo, q_hi, masked) in enumerate(units):
            vs = slice(h * MLA_V, (h + 1) * MLA_V)
            qs = slice(q_lo, q_hi)
            st = ahead.pop(0)
            if i + FLASH_ISSUE_AHEAD < len(units):
                ahead.append(scores(units[i + FLASH_ISSUE_AHEAD]))
            if masked:
                kv_i = lax.broadcasted_iota(jnp.int32, st.shape, 0) + kv0
                q_i = lax.broadcasted_iota(jnp.int32, st.shape, 1) + q_lo
                st = jnp.where(kv_i <= q_i, st, _NEG)
            m_old = m_ref[h:h + 1, qs]
            m_new = jnp.maximum(m_old, jnp.max(st, axis=0, keepdims=True))
            alpha = jnp.exp2(m_old - m_new)
            p = jnp.exp2(st - m_new).astype(BF16)
            pv = _dot(vt_ref[0, h * MLA_VT_ROWS:(h + 1) * MLA_VT_ROWS, kv0:kv0 + FLASH_KV_SUB], p)
            l_ref[h:h + 1, qs] = alpha * l_ref[h:h + 1, qs] + pv[MLA_V:MLA_V + 1, :]
            m_ref[h:h + 1, qs] = m_new
            acc_ref[vs, qs] = acc_ref[vs, qs] * alpha + pv[0:MLA_V, :]

    @pl.when(ki < qi)
    def _():
        step(False)

    @pl.when(ki == qi)
    def _():
        step(True)
        for h in range(MLA_HEADS):
            vs = slice(h * MLA_V, (h + 1) * MLA_V)
            o_ref[0, vs, :] = (acc_ref[vs, :] / l_ref[h:h + 1, :]).astype(o_ref.dtype)


def _flash(qt, k, vt, t):
    bsz, s, hp = k.shape
    tq = tk = t
    nq = s // t
    pairs = [(i, j) for i in range(nq) for j in range(i + 1)]
    qi_tab = jnp.asarray(np.asarray([p[0] for p in pairs], np.int32))
    ki_tab = jnp.asarray(np.asarray([p[1] for p in pairs], np.int32))
    grid_spec = pltpu.PrefetchScalarGridSpec(
        num_scalar_prefetch=2,
        grid=(bsz, len(pairs)),
        in_specs=[pl.BlockSpec((1, hp, tq), lambda b, p, qi, ki: (b, 0, qi[p])),
                  pl.BlockSpec((1, tk, hp), lambda b, p, qi, ki: (b, ki[p], 0)),
                  pl.BlockSpec((1, MLA_HEADS * MLA_VT_ROWS, tk), lambda b, p, qi, ki: (b, 0, ki[p]))],
        out_specs=pl.BlockSpec((1, MLA_DIM, tq), lambda b, p, qi, ki: (b, 0, qi[p])),
        scratch_shapes=[pltpu.VMEM((MLA_HEADS, tq), F32), pltpu.VMEM((MLA_HEADS, tq), F32),
                        pltpu.VMEM((MLA_DIM, tq), F32)],
    )
    return pl.pallas_call(
        functools.partial(_flash_kernel, t=t),
        grid_spec=grid_spec,
        out_shape=jax.ShapeDtypeStruct((bsz, MLA_DIM, s), BF16),
        compiler_params=_cparams(("parallel", "arbitrary")),
        name="mla_flash",
    )(qi_tab, ki_tab, qt, k, vt)


def _ffn_kernel(h_ref, ya_ref, yb_ref, yct_ref, woa_ref, wob_ref, woc_ref, fg_ref, wg_ref, wu_ref, wd_ref, *rest,
                final, tm):
    if final:
        fin_ref, o_ref, h1_ref, hn_ref = rest
    else:
        o_ref, h1_ref, hn_ref = rest
    h1 = (h_ref[...] + _dot(ya_ref[...], woa_ref[...]) + _dot(yb_ref[...], wob_ref[...])
          + _dot_tn(yct_ref[0], woc_ref[...]))
    h1_ref[...] = h1
    hn_ref[...] = _rms(h1, fg_ref[...]).astype(BF16)

    def gate_up(r):
        x = hn_ref[r * MM_SUB_ROWS:(r + 1) * MM_SUB_ROWS, :]
        return _dot(x, wg_ref[...]), _dot(x, wu_ref[...])

    nsub = tm // MM_SUB_ROWS
    nxt = gate_up(0)
    for r in range(nsub):
        rows = slice(r * MM_SUB_ROWS, (r + 1) * MM_SUB_ROWS)
        gate, up = nxt
        if r + 1 < nsub:
            nxt = gate_up(r + 1)
        act = (gate * jax.nn.sigmoid(gate) * up).astype(BF16)
        out = h1_ref[rows, :] + _dot(act, wd_ref[...])
        if final:
            out = _rms(out, fin_ref[...])
        o_ref[rows, :] = out


def _ffn(h, ya, yb, yct, wo, fg, wgu, wd, fin, layer, tm):
    t, d = h.shape
    dff = wd.shape[1]
    per = yct.shape[2] // tm
    row = lambda n: pl.BlockSpec((tm, n), lambda i: (i, 0))
    once = pl.Buffered(1)
    final = fin is not None
    in_specs = [row(d), row(RW_DIM), row(RET_DIM),
                pl.BlockSpec((1, MLA_DIM, tm), lambda i: (i // per, 0, i % per)),
                pl.BlockSpec((None, RW_DIM, d), lambda i: (layer, 0, 0), pipeline_mode=once),
                pl.BlockSpec((None, RET_DIM, d), lambda i: (layer, 1, 0), pipeline_mode=once),
                pl.BlockSpec((None, MLA_DIM, d), lambda i: (layer, 1, 0), pipeline_mode=once),
                _layer_spec(fg.shape[1:], layer, 1),
                pl.BlockSpec((None, d, dff), lambda i: (layer, 0, 0), pipeline_mode=once),
                pl.BlockSpec((None, d, dff), lambda i: (layer, 0, 1), pipeline_mode=once),
                pl.BlockSpec((None, dff, d), lambda i: (layer, 0, 0), pipeline_mode=once)]
    args = [h, ya, yb, yct, wo, wo, wo, fg, wgu, wgu, wd]
    if final:
        in_specs.append(_const_spec(fin, 1))
        args.append(fin)
    return pl.pallas_call(
        functools.partial(_ffn_kernel, final=final, tm=tm),
        grid=(t // tm,),
        in_specs=in_specs,
        out_specs=row(d),
        out_shape=jax.ShapeDtypeStruct((t, d), F32),
        scratch_shapes=[pltpu.VMEM((tm, d), F32), pltpu.VMEM((tm, d), BF16)],
        compiler_params=_cparams(("parallel",)),
        name="outproj_ffn",
    )(*args)


def _relayout(w, m, transpose=False):
    out = "lki" if transpose else "lik"
    return jnp.einsum("lij,jk->" + out, w.astype(BF16), jnp.asarray(m, BF16), preferred_element_type=BF16)


def _rot_half_into(m, src, dst, n, sign=1.0):
    for j in range(n // 2):
        m[src + n // 2 + j, dst + j] = -sign
        m[src + j, dst + n // 2 + j] = sign


def _prep_in_weights(w_in):
    perm = np.zeros((2 * RET_DIM, 2 * RET_DIM), np.float32)
    for part in range(2):
        for hh in range(RET_HEADS):
            for half in range(2):
                for j in range(RET_FREQS):
                    src = part * RET_DIM + hh * RET_HEAD_DIM + half * RET_FREQS + j
                    dst = part * RET_DIM + half * (RET_DIM // 2) + hh * RET_FREQS + j
                    perm[src, dst] = 1.0
    lat = MLA_Q_RANK + MLA_KV_RANK
    mc = np.zeros((MLA_IN, MLA_C_PAD), np.float32)
    mc[np.arange(lat), np.arange(lat)] = 1.0
    mc[lat + np.arange(MLA_ROPE), lat + MLA_NOPE + np.arange(MLA_ROPE)] = 1.0
    _rot_half_into(mc, lat, lat + MLA_NOPE + MLA_ROPE, MLA_ROPE)
    b0 = RW_IN
    c0 = RW_IN + RET_IN
    wa = w_in[..., :b0].astype(BF16)
    wb = jnp.concatenate([_relayout(w_in[..., b0:b0 + 2 * RET_DIM], perm), w_in[..., b0 + 2 * RET_DIM:c0].astype(BF16)],
                         axis=-1)
    wc = _relayout(w_in[..., c0:], mc)
    return wa, wb, wc


def _prep_mla_weights(w_q_up, w_kv_up):
    qd = MLA_NOPE + MLA_ROPE
    kvd = MLA_NOPE + MLA_V
    hp = MLA_HEADS * MLA_HEAD_PAD
    mq = np.zeros((MLA_HEADS * qd, hp), np.float32)
    mk = np.zeros((MLA_HEADS * kvd, hp), np.float32)
    mv = np.zeros((MLA_HEADS * kvd, MLA_DIM), np.float32)
    for h in range(MLA_HEADS):
        for c in range(qd):
            mq[h * qd + c, h * MLA_HEAD_PAD + c] = 1.0
        for c in range(MLA_NOPE):
            mk[h * kvd + c, h * MLA_HEAD_PAD + c] = 1.0
        for c in range(MLA_V):
            mv[h * kvd + MLA_NOPE + c, h * MLA_V + c] = 1.0
    return (_relayout(w_q_up, mq, transpose=True), _relayout(w_kv_up, mk),
            _relayout(w_kv_up, mv, transpose=True))


def _pad_rows(w, start, total):
    return jnp.pad(w, ((0, 0), (start, total - start - w.shape[1]), (0, 0)))


def _rope_table(positions):
    inv_ret = ROPE_BASE ** (-jnp.arange(0, RET_HEAD_DIM, 2, dtype=F32) / RET_HEAD_DIM)
    inv_mla = ROPE_BASE ** (-jnp.arange(0, MLA_ROPE, 2, dtype=F32) / MLA_ROPE)
    inv = jnp.concatenate([inv_ret, inv_ret, inv_mla, inv_mla, jnp.zeros((LANES - TAB_MLA_SIN - MLA_FREQS,), F32)])
    lane = np.arange(LANES)
    is_cos = (lane < TAB_RET_SIN) | ((lane >= TAB_MLA_COS) & (lane < TAB_MLA_SIN))
    ang = positions.astype(F32)[..., None] * inv
    return jnp.where(jnp.asarray(is_cos), jnp.cos(ang), jnp.sin(ang))


def kernel(x, positions, attn_norm, w_in, w_out, rw_mu, rw_w0, rw_w2, rw_a0, rw_a2, rw_g2, rw_k_k, rw_k_a, rw_r_k, rw_gn_w, rw_gn_b, rw_v0, rw_v1, rw_v2, mla_q_norm, mla_kv_norm, mla_w_q_up, mla_w_kv_up, ffn_norm, w_gate_up, w_down, final_norm):
    bsz, s, d = x.shape
    depth = w_in.shape[0]
    t = bsz * s
    tm = min(512, s)
    tm_proj = min(1024, s)
    hp = MLA_HEADS * MLA_HEAD_PAD

    tab = _rope_table(positions)
    wa, wb, wc = _prep_in_weights(w_in)
    wqt, wk, wvt = _prep_mla_weights(mla_w_q_up, mla_w_kv_up)
    wo, wgu, wd = w_out.astype(BF16), w_gate_up.astype(BF16), w_down.astype(BF16)
    v0 = jnp.concatenate([jnp.zeros((1, RW_DIM), F32), rw_v0], axis=0)
    vec = jnp.stack([rw_w0, rw_a0, rw_k_k, rw_k_a, rw_r_k.reshape(depth, RW_DIM), rw_gn_w, rw_gn_b, v0], axis=1)
    vec = jnp.pad(vec, ((0, 0), (0, _VEC_ROWS - vec.shape[1]), (0, 0)))
    w2p = _pad_rows(rw_w2, 0, RW_LOWRANK).astype(BF16)
    a2p = _pad_rows(rw_a2, RW_DECAY_RANK, RW_LOWRANK).astype(BF16)
    g2p = _pad_rows(rw_g2, RW_DECAY_RANK + RW_A_RANK, RW_LOWRANK).astype(BF16)
    v1p = jnp.pad(rw_v1, ((0, 0), (0, 0), (0, LANES - RW_V_RANK))).astype(BF16)
    v2p = _pad_rows(rw_v2, 0, LANES).astype(BF16)
    row3 = lambda a: a[:, None, :]
    mu, an, qn, kn, fn = row3(rw_mu), row3(attn_norm), row3(mla_q_norm), row3(mla_kv_norm), row3(ffn_norm)

    h = x.reshape(t, d)
    v_first = None
    for l in range(depth):
        p_a, p_b, p_c = _inproj(h, an, wa, wb, wc, l, tm_proj)
        res = _rwkv(p_a.reshape(bsz, s, RW_IN), v_first, mu, vec, w2p, a2p, g2p, v1p, v2p, l, bsz, min(256, s))
        if l == 0:
            y_a, v_first = res
        else:
            y_a = res
        y_b = _retention(p_b.reshape(bsz, s, RET_IN), tab, bsz)
        qt, k, vt = _mla_proj(p_c, tab, qn, kn, wqt, wk, wvt, l, tm_proj)
        y_ct = _flash(qt, k.reshape(bsz, s, hp), vt, min(1024, s))
        fin = final_norm[None, :] if l == depth - 1 else None
        h = _ffn(h, y_a.reshape(t, RW_DIM), y_b.reshape(t, RET_DIM), y_ct, wo, fn, wgu, wd, fin, l, tm)
    return h.reshape(bsz, s, d)
```

```python
import functools
import math

import jax
import jax.numpy as jnp
import numpy as np
from jax import lax
from jax.experimental import pallas as pl
from jax.experimental.pallas import tpu as pltpu

F32 = jnp.float32
BF16 = jnp.bfloat16

NORM_EPS = 1e-6
ROPE_BASE = 10000.0
LANES = 128

RW_HEADS = 4
RW_HEAD_DIM = 64
RW_DIM = RW_HEADS * RW_HEAD_DIM
RW_DECAY_RANK = 32
RW_A_RANK = 32
RW_V_RANK = 32
RW_GATE_RANK = 64
RW_GN_EPS = 64e-5
RW_IN = 3 * RW_DIM + RW_DECAY_RANK + RW_A_RANK + RW_GATE_RANK
RW_LOWRANK = RW_DECAY_RANK + RW_A_RANK + RW_GATE_RANK
RW_CHUNK = 64

RET_HEADS = 4
RET_HEAD_DIM = 64
RET_DIM = RET_HEADS * RET_HEAD_DIM
RET_CHUNK = 128
RET_IN = 4 * RET_DIM
RET_FREQS = RET_HEAD_DIM // 2

MLA_HEADS = 8
MLA_NOPE = 64
MLA_ROPE = 32
MLA_V = 64
MLA_Q_RANK = 384
MLA_KV_RANK = 256
MLA_DIM = MLA_HEADS * MLA_V
MLA_IN = MLA_Q_RANK + MLA_KV_RANK + MLA_ROPE
MLA_HEAD_PAD = 128
MLA_C_PAD = 768
MLA_FREQS = MLA_ROPE // 2
MLA_VT_ROWS = MLA_V + 16

TAB_RET_COS = 0
TAB_RET_SIN = RET_FREQS
TAB_MLA_COS = 2 * RET_FREQS
TAB_MLA_SIN = 2 * RET_FREQS + MLA_FREQS

V7X_VMEM_LIMIT_BYTES = 56 * 1024 * 1024


MM_SUB_ROWS = 256


def _cparams(semantics):
    return pltpu.CompilerParams(dimension_semantics=semantics, vmem_limit_bytes=V7X_VMEM_LIMIT_BYTES)


def _dot(a, b):
    return jnp.dot(a, b, preferred_element_type=F32)


def _dot_nt(a, b):
    return lax.dot_general(a, b, (((1,), (1,)), ((), ())), preferred_element_type=F32)


def _dot_tn(a, b):
    return lax.dot_general(a, b, (((0,), (0,)), ((), ())), preferred_element_type=F32)


def _split_dot(x, m_bf16, passes):
    acc = None
    rem = x
    for _ in range(passes):
        piece = rem.astype(BF16)
        part = _dot(piece, m_bf16)
        acc = part if acc is None else acc + part
        rem = rem - piece.astype(F32)
    return acc


def _split_dot_left(m_bf16, x):
    acc = None
    rem = x
    for _ in range(3):
        piece = rem.astype(BF16)
        part = _dot(m_bf16, piece)
        acc = part if acc is None else acc + part
        rem = rem - piece.astype(F32)
    return acc


def _lane_groups(x, width, picks):
    lane = lax.broadcasted_iota(jnp.int32, x.shape, 1)
    rolled = {0: x}
    out = jnp.zeros_like(x)
    for g, src in enumerate(picks):
        if src is None:
            continue
        shift = (g * width - src) % LANES
        if shift not in rolled:
            rolled[shift] = pltpu.roll(x, shift, 1)
        out = jnp.where((lane >= g * width) & (lane < (g + 1) * width), rolled[shift], out)
    return out


def _rms(x, g):
    return x * lax.rsqrt(jnp.mean(x * x, axis=-1, keepdims=True) + NORM_EPS) * g


def _layer_spec(shape, layer, ngrid):
    zeros = (0,) * len(shape)
    if ngrid == 1:
        return pl.BlockSpec((None,) + tuple(shape), lambda i: (layer,) + zeros)
    return pl.BlockSpec((None,) + tuple(shape), lambda i, j: (layer,) + zeros)


def _const_spec(a, ngrid):
    zeros = (0,) * a.ndim
    if ngrid == 1:
        return pl.BlockSpec(a.shape, lambda i: zeros)
    return pl.BlockSpec(a.shape, lambda i, j: zeros)


def _inproj_kernel(x_ref, g_ref, wa_ref, wb_ref, wc_ref, pa_ref, pb_ref, pc_ref):
    for r in range(x_ref.shape[0] // MM_SUB_ROWS):
        rows = slice(r * MM_SUB_ROWS, (r + 1) * MM_SUB_ROWS)
        hn = _rms(x_ref[rows, :], g_ref[...]).astype(BF16)
        pa_ref[rows, :] = _dot(hn, wa_ref[...])
        pb_ref[rows, :] = _dot(hn, wb_ref[...])
        pc_ref[rows, :] = _dot(hn, wc_ref[...])


def _inproj(x, g, wa, wb, wc, layer, tm):
    t, d = x.shape
    row = lambda n: pl.BlockSpec((tm, n), lambda i: (i, 0))
    ws = (wa, wb, wc)
    return pl.pallas_call(
        _inproj_kernel,
        grid=(t // tm,),
        in_specs=[row(d), _layer_spec(g.shape[1:], layer, 1)] + [_layer_spec(w.shape[1:], layer, 1) for w in ws],
        out_specs=[row(w.shape[2]) for w in ws],
        out_shape=[jax.ShapeDtypeStruct((t, w.shape[2]), F32) for w in ws],
        compiler_params=_cparams(("parallel",)),
        name="inproj",
    )(x, g, wa, wb, wc)


_VEC_ROWS = 16
(_V_W0, _V_A0, _V_KK, _V_KA, _V_RK, _V_GNW, _V_GNB, _V_V0) = range(8)


def _rwkv_kernel(*refs, has_vres, nb, tb):
    if has_vres:
        (p_ref, vf_ref, mu_ref, vec_ref, w2_ref, a2_ref, g2_ref, v1_ref, v2_ref, bd_ref, tri_ref,
         o_ref, carry_ref, s_ref, r_s, k_s, v_s, lw_s, kk_s, ka_s, g_s, bon_s, y_s) = refs
    else:
        (p_ref, mu_ref, vec_ref, w2_ref, a2_ref, g2_ref, bd_ref, tri_ref,
         o_ref, vfo_ref, carry_ref, s_ref, r_s, k_s, v_s, lw_s, kk_s, ka_s, g_s, bon_s, y_s) = refs

    L = RW_CHUNK
    C = RW_DIM
    tstep = pl.program_id(1)

    @pl.when(tstep == 0)
    def _():
        carry_ref[...] = jnp.zeros_like(carry_ref)
        s_ref[...] = jnp.zeros_like(s_ref)

    vec = vec_ref[...]
    row = lambda i: vec[i:i + 1, :]
    bd = bd_ref[...]
    mu = mu_ref[...]

    def bdsum(x):
        return _split_dot(x, bd, 2)

    def prep(b, _):
        p = p_ref[b]
        shifted = pltpu.roll(p, 1, 0)
        first = lax.broadcasted_iota(jnp.int32, p.shape, 0) == 0
        p_prev = jnp.where(first, carry_ref[b], shifted)
        carry_ref[b] = p[tb - 1:tb, :]
        ps = p + (p_prev - p) * mu
        r = ps[:, 0:C]
        k = ps[:, C:2 * C]
        v = ps[:, 2 * C:3 * C]
        lr = ps[:, 3 * C:3 * C + RW_LOWRANK]
        z = row(_V_W0) + _dot(jnp.tanh(lr).astype(BF16), w2_ref[...])
        lw = -math.exp(-0.5) * jax.nn.sigmoid(z)
        a = jax.nn.sigmoid(row(_V_A0) + _dot(lr.astype(BF16), a2_ref[...]))
        g = _dot(jax.nn.sigmoid(lr).astype(BF16), g2_ref[...])
        if has_vres:
            lat = _dot(v.astype(BF16), v1_ref[...]).astype(BF16)
            v = v + (vf_ref[b] - v) * jax.nn.sigmoid(row(_V_V0) + _dot(lat, v2_ref[...]))
        else:
            vfo_ref[b] = v
        kk = k * row(_V_KK)
        kk = kk * lax.rsqrt(jnp.maximum(bdsum(kk * kk), 1e-24))
        k2 = k * (1.0 + (a - 1.0) * row(_V_KA))
        r_s[b] = r
        k_s[b] = k2
        v_s[b] = v
        lw_s[b] = lw
        kk_s[b] = kk
        ka_s[b] = kk * a
        g_s[b] = g
        bon_s[b] = bdsum(r * k2 * row(_V_RK)) * v
        return 0

    for b in range(nb):
        prep(b, 0)

    hl = RW_HEADS * L
    r_i = lax.broadcasted_iota(jnp.int32, (hl, C), 0)
    c_i = lax.broadcasted_iota(jnp.int32, (hl, C), 1)
    headmask = (r_i // L) == (c_i // RW_HEAD_DIM)
    t_i = lax.broadcasted_iota(jnp.int32, (L, hl), 0)
    s_i = lax.broadcasted_iota(jnp.int32, (L, hl), 1) % L
    strict = t_i > s_i
    incl = t_i >= s_i
    tri = tri_ref[...]

    def masked4(x):
        return jnp.where(headmask, jnp.concatenate([x] * RW_HEADS, axis=0), 0.0).astype(BF16)

    def chunk(j, _):
        sl = pl.ds(pl.multiple_of(j * L, L), L)
        bs = range(nb)
        r = [r_s[b, sl, :] for b in bs]
        k2 = [k_s[b, sl, :] for b in bs]
        v = [v_s[b, sl, :] for b in bs]
        lw = [lw_s[b, sl, :] for b in bs]
        kk = [kk_s[b, sl, :] for b in bs]
        ka = [ka_s[b, sl, :] for b in bs]
        c = [_split_dot_left(tri, lw[b]) for b in bs]
        c_last = [c[b][L - 1:L, :] for b in bs]
        sig = [0.5 * c_last[b] for b in bs]
        e_neg = [jnp.exp(sig[b] - c[b]) for b in bs]
        kt = [k2[b] * e_neg[b] for b in bs]
        bt = [ka[b] * e_neg[b] for b in bs]
        x = [jnp.concatenate([-(kk[b] * jnp.exp(c[b] - lw[b] - sig[b])), r[b] * jnp.exp(c[b] - sig[b])],
                             axis=0).astype(BF16) for b in bs]
        s_old = [s_ref[b] for b in bs]
        kbs_w = [jnp.concatenate([masked4(kt[b]), masked4(bt[b]), (s_old[b] * jnp.exp(sig[b])).astype(BF16)], axis=0)
                 for b in bs]
        abx = [_dot_nt(x[b], kbs_w[b]) for b in bs]
        ab = [abx[b][:, 0:2 * hl] for b in bs]
        xs = [abx[b][:, 2 * hl:2 * hl + C] for b in bs]
        a_k = [jnp.concatenate([jnp.where(strict, ab[b][0:L, 0:hl], 0.0), jnp.where(incl, ab[b][L:2 * L, 0:hl], 0.0)],
                               axis=0).astype(BF16) for b in bs]
        n_w = [jnp.where(strict, ab[b][0:L, hl:2 * hl], 0.0) for b in bs]
        a_rb = [jnp.where(incl, ab[b][L:2 * L, hl:2 * hl], 0.0).astype(BF16) for b in bs]
        av = [_dot(a_k[b], masked4(v[b])) for b in bs]
        u = [av[b][0:L] + xs[b][0:L] for b in bs]
        for i in range(6):
            n_b = [n_w[b].astype(BF16) for b in bs]
            if i < 5:
                nu = [_dot(n_b[b], jnp.concatenate([masked4(u[b]), masked4(n_w[b])], axis=1)) for b in bs]
                u = [u[b] + nu[b][:, 0:C] for b in bs]
                n_w = [nu[b][:, C:2 * C] for b in bs]
            else:
                u = [u[b] + _dot(n_b[b], masked4(u[b])) for b in bs]
        y = [av[b][L:2 * L] + xs[b][L:2 * L] + _dot(a_rb[b], masked4(u[b])) for b in bs]
        for b in bs:
            y_s[b, sl, :] = y[b]
        scale = [jnp.exp(c_last[b] - sig[b]) for b in bs]
        kbs = [jnp.concatenate([kt[b] * scale[b], bt[b] * scale[b]], axis=0).astype(BF16) for b in bs]
        vu = [jnp.concatenate([v[b], u[b]], axis=0).astype(BF16) for b in bs]
        upd = [_dot_tn(vu[b], kbs[b]) for b in bs]
        for b in bs:
            s_ref[b] = s_old[b] * jnp.exp(c_last[b]) + jnp.where(headmask, upd[b], 0.0)
        return 0

    lax.fori_loop(0, tb // L, chunk, 0)

    bs = range(nb)
    mean = [bdsum(y_s[b]) * (1.0 / RW_HEAD_DIM) for b in bs]
    d = [y_s[b] - mean[b] for b in bs]
    var = [bdsum(d[b] * d[b]) * (1.0 / RW_HEAD_DIM) for b in bs]
    for b in bs:
        yn = d[b] * lax.rsqrt(var[b] + RW_GN_EPS) * row(_V_GNW) + row(_V_GNB)
        o_ref[b] = ((yn + bon_s[b]) * g_s[b]).astype(o_ref.dtype)


def _block_ones(n, blk):
    i = np.arange(n)
    return (i[:, None] // blk == i[None, :] // blk).astype(np.float32)


def _rwkv(p_a, v_first, mu, vec, w2p, a2p, g2p, v1p, v2p, layer, nb, tb):
    bsz, s, _ = p_a.shape
    has_vres = v_first is not None
    bd = jnp.asarray(_block_ones(RW_DIM, RW_HEAD_DIM), BF16)
    tri = jnp.asarray(np.tril(np.ones((RW_CHUNK, RW_CHUNK), np.float32)), BF16)
    blk = lambda n: pl.BlockSpec((nb, tb, n), lambda i, j: (i, j, 0))
    lay = lambda a, l: _layer_spec(a.shape[1:], l, 2)
    common = [lay(mu, layer), lay(vec, layer), lay(w2p, layer), lay(a2p, layer), lay(g2p, layer)]
    consts = [_const_spec(bd, 2), _const_spec(tri, 2)]
    if has_vres:
        args = (p_a, v_first, mu, vec, w2p, a2p, g2p, v1p, v2p, bd, tri)
        in_specs = [blk(RW_IN), blk(RW_DIM)] + common + [lay(v1p, layer - 1), lay(v2p, layer - 1)] + consts
        out_specs = blk(RW_DIM)
        out_shape = jax.ShapeDtypeStruct((bsz, s, RW_DIM), BF16)
    else:
        args = (p_a, mu, vec, w2p, a2p, g2p, bd, tri)
        in_specs = [blk(RW_IN)] + common + consts
        out_specs = [blk(RW_DIM), blk(RW_DIM)]
        out_shape = [jax.ShapeDtypeStruct((bsz, s, RW_DIM), BF16), jax.ShapeDtypeStruct((bsz, s, RW_DIM), F32)]
    big = pltpu.VMEM((nb, tb, RW_DIM), F32)
    scratch = [pltpu.VMEM((nb, 1, RW_IN), F32), pltpu.VMEM((nb, RW_DIM, RW_DIM), F32)] + [big] * 9
    return pl.pallas_call(
        functools.partial(_rwkv_kernel, has_vres=has_vres, nb=nb, tb=tb),
        grid=(bsz // nb, s // tb),
        in_specs=in_specs,
        out_specs=out_specs,
        out_shape=out_shape,
        scratch_shapes=scratch,
        compiler_params=_cparams(("parallel", "arbitrary")),
        name="rwkv7",
    )(*args)


def _ret_tables():
    h, d, c = RET_HEADS, RET_HEAD_DIM, RET_CHUNK
    log_gamma = np.log(1.0 - 2.0 ** (-5.0 - np.arange(h, dtype=np.float64)))
    idx = np.arange(c, dtype=np.float64)
    diff = idx[:, None] - idx[None, :]
    dmask = np.where(diff >= 0, np.exp(log_gamma[:, None, None] * np.maximum(diff, 0.0)), 0.0)
    lane = np.arange(RET_DIM)
    head_qk = (lane % (RET_DIM // 2)) // (d // 2)
    head_v = lane // d
    qdec = np.exp(log_gamma[head_qk][None, :] * (idx[:, None] + 1.0))
    kdec = np.exp(log_gamma[head_qk][None, :] * (c - 1.0 - idx[:, None])) * d ** -0.5
    hm_qk = (np.arange(h)[:, None, None] == head_qk[None, None, :]) * np.ones((1, c, 1))
    hm_v = (np.arange(h)[:, None, None] == head_v[None, None, :]) * np.ones((1, c, 1))
    block = head_qk[:, None] == head_v[None, :]
    rdec = np.where(block, np.exp(log_gamma[head_qk] * c)[:, None], 0.0)
    f = lambda a: jnp.asarray(a, F32)
    return (f(dmask.reshape(h * c, c) * d ** -0.5), f(qdec), f(kdec), f(hm_qk.reshape(h * c, RET_DIM)),
            f(hm_v.reshape(h * c, RET_DIM)), f(block), f(rdec))


def _ret_kernel(p_ref, tab_ref, dmask_ref, qdec_ref, kdec_ref, hmqk_ref, hmv_ref, block_ref, rdec_ref,
                bd_ref, o_ref, r_ref, *, nb):
    c, dm = RET_CHUNK, RET_DIM
    half = dm // 2

    @pl.when(pl.program_id(0) == 0)
    def _():
        r_ref[...] = jnp.zeros_like(r_ref)

    bs = range(nb)
    cs = [(_lane_groups(tab_ref[b], RET_FREQS, [TAB_RET_COS] * RET_HEADS),
           _lane_groups(tab_ref[b], RET_FREQS, [TAB_RET_SIN] * RET_HEADS)) for b in bs]
    p = [p_ref[b] for b in bs]

    def rope(x, t):
        cos, sin = t
        x1, x2 = x[:, :half], x[:, half:]
        return jnp.concatenate([x1 * cos - x2 * sin, x2 * cos + x1 * sin], axis=1)

    q = [rope(p[b][:, 0:dm], cs[b]) for b in bs]
    k = [rope(p[b][:, dm:2 * dm], cs[b]) for b in bs]
    vb = [p[b][:, 2 * dm:3 * dm].astype(BF16) for b in bs]
    q_bd = [(jnp.concatenate([q[b]] * RET_HEADS, axis=0) * hmqk_ref[...]).astype(BF16) for b in bs]
    scores = [(_dot_nt(q_bd[b], k[b].astype(BF16)) * dmask_ref[...]).astype(BF16) for b in bs]
    o = [_dot(scores[b], vb[b]) * hmv_ref[...] for b in bs]
    r_old = [r_ref[b] for b in bs]
    cross = [_dot((q[b] * qdec_ref[...]).astype(BF16), r_old[b].astype(BF16)) for b in bs]
    upd = [_dot_tn((k[b] * kdec_ref[...]).astype(BF16), vb[b]) for b in bs]
    for b in bs:
        r_ref[b] = r_old[b] * rdec_ref[...] + upd[b] * block_ref[...]
    y = [o[b][0:c] + o[b][c:2 * c] + o[b][2 * c:3 * c] + o[b][3 * c:4 * c] + cross[b] for b in bs]
    ms = [_split_dot(y[b] * y[b], bd_ref[...], 2) * (1.0 / RET_HEAD_DIM) for b in bs]
    for b in bs:
        g = p[b][:, 3 * dm:4 * dm]
        o_ref[b] = (g * jax.nn.sigmoid(g) * y[b] * lax.rsqrt(ms[b] + NORM_EPS)).astype(o_ref.dtype)


def _retention(p_b, tab, nb):
    bsz, s, _ = p_b.shape
    consts = _ret_tables() + (jnp.asarray(_block_ones(RET_DIM, RET_HEAD_DIM), BF16),)
    blk = lambda n: pl.BlockSpec((nb, RET_CHUNK, n), lambda j, i: (i, j, 0))
    full = lambda a: pl.BlockSpec(a.shape, lambda j, i: (0,) * a.ndim)
    assert bsz == nb, "one batch group per time block keeps the state scratch simple"
    return pl.pallas_call(
        functools.partial(_ret_kernel, nb=nb),
        grid=(s // RET_CHUNK, bsz // nb),
        in_specs=[blk(RET_IN), blk(LANES)] + [full(a) for a in consts],
        out_specs=blk(RET_DIM),
        out_shape=jax.ShapeDtypeStruct((bsz, s, RET_DIM), BF16),
        scratch_shapes=[pltpu.VMEM((nb, RET_DIM, RET_DIM), F32)],
        compiler_params=_cparams(("arbitrary", "arbitrary")),
        name="retention",
    )(p_b, tab, *consts)


def _mla_tables():
    pq = np.zeros((2 * MLA_FREQS, LANES), np.float32)
    for j in range(MLA_FREQS):
        pq[j, TAB_MLA_COS + j] = 1.0
        pq[MLA_FREQS + j, TAB_MLA_SIN + j] = 1.0
    return jnp.asarray(pq, BF16)


def _split_dot_nt(m_bf16, x, passes):
    acc = None
    rem = x
    for _ in range(passes):
        piece = rem.astype(BF16)
        part = _dot_nt(m_bf16, piece)
        acc = part if acc is None else acc + part
        rem = rem - piece.astype(F32)
    return acc


def _mla_proj_kernel(pc_ref, tab_ref, qn_ref, kn_ref, wqt_ref, wk_ref, wvt_ref, pq_ref,
                     qt_ref, k_ref, vt_ref):
    pc = pc_ref[...]
    tab = tab_ref[0]
    ck = _lane_groups(tab, MLA_FREQS, [None] * 4 + [TAB_MLA_COS] * 2 + [TAB_MLA_SIN] * 2)
    cs = _split_dot_nt(pq_ref[...], tab, 3)
    cos, sin = cs[0:MLA_FREQS], cs[MLA_FREQS:2 * MLA_FREQS]
    nq = _rms(pc[:, 0:MLA_Q_RANK], qn_ref[...]).astype(BF16)
    nkv = _rms(pc[:, MLA_Q_RANK:MLA_Q_RANK + MLA_KV_RANK], kn_ref[...]).astype(BF16)
    kr = pc[:, MLA_Q_RANK + MLA_KV_RANK:MLA_C_PAD] * ck
    kr = kr + pltpu.roll(kr, LANES - MLA_ROPE, 1)
    lane = lax.broadcasted_iota(jnp.int32, kr.shape, 1)
    kr = jnp.where((lane >= MLA_NOPE) & (lane < MLA_NOPE + MLA_ROPE), kr, 0.0)
    qa = _dot_nt(wqt_ref[...], nq) * ((MLA_NOPE + MLA_ROPE) ** -0.5 * math.log2(math.e))
    zeros = jnp.zeros((MLA_HEAD_PAD - MLA_NOPE - MLA_ROPE, qa.shape[1]), qt_ref.dtype)
    for h in range(MLA_HEADS):
        b0 = h * MLA_HEAD_PAD
        b1, b2, b3 = b0 + MLA_NOPE, b0 + MLA_NOPE + MLA_FREQS, b0 + MLA_NOPE + MLA_ROPE
        x1, x2 = qa[b1:b2], qa[b2:b3]
        qt_ref[0, b0:b1, :] = qa[b0:b1].astype(qt_ref.dtype)
        qt_ref[0, b1:b2, :] = (x1 * cos - x2 * sin).astype(qt_ref.dtype)
        qt_ref[0, b2:b3, :] = (x2 * cos + x1 * sin).astype(qt_ref.dtype)
        qt_ref[0, b3:b0 + MLA_HEAD_PAD, :] = zeros
    k_nope = _dot(nkv, wk_ref[...])
    for h in range(MLA_HEADS):
        hs = slice(h * MLA_HEAD_PAD, (h + 1) * MLA_HEAD_PAD)
        k_ref[:, hs] = (k_nope[:, hs] + kr).astype(k_ref.dtype)
    vt = _dot_nt(wvt_ref[...], nkv).astype(vt_ref.dtype)
    ones = jnp.ones((MLA_VT_ROWS - MLA_V, vt.shape[1]), vt_ref.dtype)
    for h in range(MLA_HEADS):
        vt_ref[0, h * MLA_VT_ROWS:h * MLA_VT_ROWS + MLA_V, :] = vt[h * MLA_V:(h + 1) * MLA_V, :]
        vt_ref[0, h * MLA_VT_ROWS + MLA_V:(h + 1) * MLA_VT_ROWS, :] = ones


def _mla_proj(p_c, tab, qn, kn, wqt, wk, wvt, layer, tm):
    t = p_c.shape[0]
    bsz, s, _ = tab.shape
    per = s // tm
    row = lambda n: pl.BlockSpec((tm, n), lambda i: (i, 0))
    cols = lambda n: pl.BlockSpec((1, n, tm), lambda i: (i // per, 0, i % per))
    pq = _mla_tables()
    hp = MLA_HEADS * MLA_HEAD_PAD
    lay = lambda a: _layer_spec(a.shape[1:], layer, 1)
    return pl.pallas_call(
        _mla_proj_kernel,
        grid=(t // tm,),
        in_specs=[row(MLA_C_PAD), pl.BlockSpec((1, tm, LANES), lambda i: (i // per, i % per, 0)),
                  lay(qn), lay(kn), lay(wqt), lay(wk), lay(wvt),
                  _const_spec(pq, 1)],
        out_specs=[cols(hp), row(hp), cols(MLA_HEADS * MLA_VT_ROWS)],
        out_shape=[jax.ShapeDtypeStruct((bsz, hp, s), BF16), jax.ShapeDtypeStruct((t, hp), BF16),
                   jax.ShapeDtypeStruct((bsz, MLA_HEADS * MLA_VT_ROWS, s), BF16)],
        compiler_params=_cparams(("parallel",)),
        name="mla_proj",
    )(p_c, tab, qn, kn, wqt, wk, wvt, pq)


_NEG = -1e30
FLASH_ISSUE_AHEAD = 4
FLASH_KV_SUB = 256
FLASH_Q_SUB = 512


def _flash_units(t, diagonal):
    units = []
    for q0 in range(0, t, FLASH_Q_SUB):
        for kv0 in range(0, t, FLASH_KV_SUB):
            for h in range(MLA_HEADS):
                if not diagonal:
                    units.append((h, kv0, q0, q0 + FLASH_Q_SUB, False))
                    continue
                q_lo = max(q0, kv0)
                if q_lo >= q0 + FLASH_Q_SUB:
                    continue
                units.append((h, kv0, q_lo, q0 + FLASH_Q_SUB, kv0 + FLASH_KV_SUB - 1 > q_lo))
    return units


def _flash_kernel(qi_ref, ki_ref, qt_ref, k_ref, vt_ref, o_ref, m_ref, l_ref, acc_ref, *, t):
    pair = pl.program_id(1)
    qi = qi_ref[pair]
    ki = ki_ref[pair]

    @pl.when(ki == 0)
    def _():
        m_ref[...] = jnp.full_like(m_ref, _NEG)
        l_ref[...] = jnp.zeros_like(l_ref)
        acc_ref[...] = jnp.zeros_like(acc_ref)

    def step(diagonal):
        units = _flash_units(t, diagonal)

        def scores(unit):
            h, kv0, q_lo, q_hi, _ = unit
            hs = slice(h * MLA_HEAD_PAD, (h + 1) * MLA_HEAD_PAD)
            return _dot(k_ref[0, kv0:kv0 + FLASH_KV_SUB, hs], qt_ref[0, hs, q_lo:q_hi])

        ahead = [scores(u) for u in units[:FLASH_ISSUE_AHEAD]]
        for i, (h, kv0, q_lo, q_hi, masked) in enumerate(units):
            vs = slice(h * MLA_V, (h + 1) * MLA_V)
            qs = slice(q_lo, q_hi)
            st = ahead.pop(0)
            if i + FLASH_ISSUE_AHEAD < len(units):
                ahead.append(scores(units[i + FLASH_ISSUE_AHEAD]))
            if masked:
                kv_i = lax.broadcasted_iota(jnp.int32, st.shape, 0) + kv0
                q_i = lax.broadcasted_iota(jnp.int32, st.shape, 1) + q_lo
                st = jnp.where(kv_i <= q_i, st, _NEG)
            m_old = m_ref[h:h + 1, qs]
            m_new = jnp.maximum(m_old, jnp.max(st, axis=0, keepdims=True))
            alpha = jnp.exp2(m_old - m_new)
            p = jnp.exp2(st - m_new).astype(BF16)
            pv = _dot(vt_ref[0, h * MLA_VT_ROWS:(h + 1) * MLA_VT_ROWS, kv0:kv0 + FLASH_KV_SUB], p)
            l_ref[h:h + 1, qs] = alpha * l_ref[h:h + 1, qs] + pv[MLA_V:MLA_V + 1, :]
            m_ref[h:h + 1, qs] = m_new
            acc_ref[vs, qs] = acc_ref[vs, qs] * alpha + pv[0:MLA_V, :]

    @pl.when(ki < qi)
    def _():
        step(False)

    @pl.when(ki == qi)
    def _():
        step(True)
        for h in range(MLA_HEADS):
            vs = slice(h * MLA_V, (h + 1) * MLA_V)
            o_ref[0, vs, :] = (acc_ref[vs, :] / l_ref[h:h + 1, :]).astype(o_ref.dtype)


def _flash(qt, k, vt, t):
    bsz, s, hp = k.shape
    tq = tk = t
    nq = s // t
    pairs = [(i, j) for i in range(nq) for j in range(i + 1)]
    qi_tab = jnp.asarray(np.asarray([p[0] for p in pairs], np.int32))
    ki_tab = jnp.asarray(np.asarray([p[1] for p in pairs], np.int32))
    grid_spec = pltpu.PrefetchScalarGridSpec(
        num_scalar_prefetch=2,
        grid=(bsz, len(pairs)),
        in_specs=[pl.BlockSpec((1, hp, tq), lambda b, p, qi, ki: (b, 0, qi[p])),
                  pl.BlockSpec((1, tk, hp), lambda b, p, qi, ki: (b, ki[p], 0)),
                  pl.BlockSpec((1, MLA_HEADS * MLA_VT_ROWS, tk), lambda b, p, qi, ki: (b, 0, ki[p]))],
        out_specs=pl.BlockSpec((1, MLA_DIM, tq), lambda b, p, qi, ki: (b, 0, qi[p])),
        scratch_shapes=[pltpu.VMEM((MLA_HEADS, tq), F32), pltpu.VMEM((MLA_HEADS, tq), F32),
                        pltpu.VMEM((MLA_DIM, tq), F32)],
    )
    return pl.pallas_call(
        functools.partial(_flash_kernel, t=t),
        grid_spec=grid_spec,
        out_shape=jax.ShapeDtypeStruct((bsz, MLA_DIM, s), BF16),
        compiler_params=_cparams(("parallel", "arbitrary")),
        name="mla_flash",
    )(qi_tab, ki_tab, qt, k, vt)


def _ffn_kernel(h_ref, ya_ref, yb_ref, yct_ref, woa_ref, wob_ref, woc_ref, fg_ref, wg_ref, wu_ref, wd_ref, *rest,
                final, tm):
    if final:
        fin_ref, o_ref, h1_ref, hn_ref = rest
    else:
        o_ref, h1_ref, hn_ref = rest
    h1 = (h_ref[...] + _dot(ya_ref[...], woa_ref[...]) + _dot(yb_ref[...], wob_ref[...])
          + _dot_tn(yct_ref[0], woc_ref[...]))
    h1_ref[...] = h1
    hn_ref[...] = _rms(h1, fg_ref[...]).astype(BF16)

    def gate_up(r):
        x = hn_ref[r * MM_SUB_ROWS:(r + 1) * MM_SUB_ROWS, :]
        return _dot(x, wg_ref[...]), _dot(x, wu_ref[...])

    nsub = tm // MM_SUB_ROWS
    nxt = gate_up(0)
    for r in range(nsub):
        rows = slice(r * MM_SUB_ROWS, (r + 1) * MM_SUB_ROWS)
        gate, up = nxt
        if r + 1 < nsub:
            nxt = gate_up(r + 1)
        act = (gate * jax.nn.sigmoid(gate) * up).astype(BF16)
        out = h1_ref[rows, :] + _dot(act, wd_ref[...])
        if final:
            out = _rms(out, fin_ref[...])
        o_ref[rows, :] = out


def _ffn(h, ya, yb, yct, wo, fg, wgu, wd, fin, layer, tm):
    t, d = h.shape
    dff = wd.shape[1]
    per = yct.shape[2] // tm
    row = lambda n: pl.BlockSpec((tm, n), lambda i: (i, 0))
    once = pl.Buffered(1)
    final = fin is not None
    in_specs = [row(d), row(RW_DIM), row(RET_DIM),
                pl.BlockSpec((1, MLA_DIM, tm), lambda i: (i // per, 0, i % per)),
                pl.BlockSpec((None, RW_DIM, d), lambda i: (layer, 0, 0), pipeline_mode=once),
                pl.BlockSpec((None, RET_DIM, d), lambda i: (layer, 1, 0), pipeline_mode=once),
                pl.BlockSpec((None, MLA_DIM, d), lambda i: (layer, 1, 0), pipeline_mode=once),
                _layer_spec(fg.shape[1:], layer, 1),
                pl.BlockSpec((None, d, dff), lambda i: (layer, 0, 0), pipeline_mode=once),
                pl.BlockSpec((None, d, dff), lambda i: (layer, 0, 1), pipeline_mode=once),
                pl.BlockSpec((None, dff, d), lambda i: (layer, 0, 0), pipeline_mode=once)]
    args = [h, ya, yb, yct, wo, wo, wo, fg, wgu, wgu, wd]
    if final:
        in_specs.append(_const_spec(fin, 1))
        args.append(fin)
    return pl.pallas_call(
        functools.partial(_ffn_kernel, final=final, tm=tm),
        grid=(t // tm,),
        in_specs=in_specs,
        out_specs=row(d),
        out_shape=jax.ShapeDtypeStruct((t, d), F32),
        scratch_shapes=[pltpu.VMEM((tm, d), F32), pltpu.VMEM((tm, d), BF16)],
        compiler_params=_cparams(("parallel",)),
        name="outproj_ffn",
    )(*args)


def _relayout(w, m, transpose=False):
    out = "lki" if transpose else "lik"
    return jnp.einsum("lij,jk->" + out, w.astype(BF16), jnp.asarray(m, BF16), preferred_element_type=BF16)


def _rot_half_into(m, src, dst, n, sign=1.0):
    for j in range(n // 2):
        m[src + n // 2 + j, dst + j] = -sign
        m[src + j, dst + n // 2 + j] = sign


def _prep_in_weights(w_in):
    perm = np.zeros((2 * RET_DIM, 2 * RET_DIM), np.float32)
    for part in range(2):
        for hh in range(RET_HEADS):
            for half in range(2):
                for j in range(RET_FREQS):
                    src = part * RET_DIM + hh * RET_HEAD_DIM + half * RET_FREQS + j
                    dst = part * RET_DIM + half * (RET_DIM // 2) + hh * RET_FREQS + j
                    perm[src, dst] = 1.0
    lat = MLA_Q_RANK + MLA_KV_RANK
    mc = np.zeros((MLA_IN, MLA_C_PAD), np.float32)
    mc[np.arange(lat), np.arange(lat)] = 1.0
    mc[lat + np.arange(MLA_ROPE), lat + MLA_NOPE + np.arange(MLA_ROPE)] = 1.0
    _rot_half_into(mc, lat, lat + MLA_NOPE + MLA_ROPE, MLA_ROPE)
    b0 = RW_IN
    c0 = RW_IN + RET_IN
    wa = w_in[..., :b0].astype(BF16)
    wb = jnp.concatenate([_relayout(w_in[..., b0:b0 + 2 * RET_DIM], perm), w_in[..., b0 + 2 * RET_DIM:c0].astype(BF16)],
                         axis=-1)
    wc = _relayout(w_in[..., c0:], mc)
    return wa, wb, wc


def _prep_mla_weights(w_q_up, w_kv_up):
    qd = MLA_NOPE + MLA_ROPE
    kvd = MLA_NOPE + MLA_V
    hp = MLA_HEADS * MLA_HEAD_PAD
    mq = np.zeros((MLA_HEADS * qd, hp), np.float32)
    mk = np.zeros((MLA_HEADS * kvd, hp), np.float32)
    mv = np.zeros((MLA_HEADS * kvd, MLA_DIM), np.float32)
    for h in range(MLA_HEADS):
        for c in range(qd):
            mq[h * qd + c, h * MLA_HEAD_PAD + c] = 1.0
        for c in range(MLA_NOPE):
            mk[h * kvd + c, h * MLA_HEAD_PAD + c] = 1.0
        for c in range(MLA_V):
            mv[h * kvd + MLA_NOPE + c, h * MLA_V + c] = 1.0
    return (_relayout(w_q_up, mq, transpose=True), _relayout(w_kv_up, mk),
            _relayout(w_kv_up, mv, transpose=True))


def _pad_rows(w, start, total):
    return jnp.pad(w, ((0, 0), (start, total - start - w.shape[1]), (0, 0)))


def _rope_table(positions):
    inv_ret = ROPE_BASE ** (-jnp.arange(0, RET_HEAD_DIM, 2, dtype=F32) / RET_HEAD_DIM)
    inv_mla = ROPE_BASE ** (-jnp.arange(0, MLA_ROPE, 2, dtype=F32) / MLA_ROPE)
    inv = jnp.concatenate([inv_ret, inv_ret, inv_mla, inv_mla, jnp.zeros((LANES - TAB_MLA_SIN - MLA_FREQS,), F32)])
    lane = np.arange(LANES)
    is_cos = (lane < TAB_RET_SIN) | ((lane >= TAB_MLA_COS) & (lane < TAB_MLA_SIN))
    ang = positions.astype(F32)[..., None] * inv
    return jnp.where(jnp.asarray(is_cos), jnp.cos(ang), jnp.sin(ang))


def kernel(x, positions, attn_norm, w_in, w_out, rw_mu, rw_w0, rw_w2, rw_a0, rw_a2, rw_g2, rw_k_k, rw_k_a, rw_r_k, rw_gn_w, rw_gn_b, rw_v0, rw_v1, rw_v2, mla_q_norm, mla_kv_norm, mla_w_q_up, mla_w_kv_up, ffn_norm, w_gate_up, w_down, final_norm):
    bsz, s, d = x.shape
    depth = w_in.shape[0]
    t = bsz * s
    tm = min(512, s)
    tm_proj = min(1024, s)
    hp = MLA_HEADS * MLA_HEAD_PAD

    tab = _rope_table(positions)
    wa, wb, wc = _prep_in_weights(w_in)
    wqt, wk, wvt = _prep_mla_weights(mla_w_q_up, mla_w_kv_up)
    wo, wgu, wd = w_out.astype(BF16), w_gate_up.astype(BF16), w_down.astype(BF16)
    v0 = jnp.concatenate([jnp.zeros((1, RW_DIM), F32), rw_v0], axis=0)
    vec = jnp.stack([rw_w0, rw_a0, rw_k_k, rw_k_a, rw_r_k.reshape(depth, RW_DIM), rw_gn_w, rw_gn_b, v0], axis=1)
    vec = jnp.pad(vec, ((0, 0), (0, _VEC_ROWS - vec.shape[1]), (0, 0)))
    w2p = _pad_rows(rw_w2, 0, RW_LOWRANK).astype(BF16)
    a2p = _pad_rows(rw_a2, RW_DECAY_RANK, RW_LOWRANK).astype(BF16)
    g2p = _pad_rows(rw_g2, RW_DECAY_RANK + RW_A_RANK, RW_LOWRANK).astype(BF16)
    v1p = jnp.pad(rw_v1, ((0, 0), (0, 0), (0, LANES - RW_V_RANK))).astype(BF16)
    v2p = _pad_rows(rw_v2, 0, LANES).astype(BF16)
    row3 = lambda a: a[:, None, :]
    mu, an, qn, kn, fn = row3(rw_mu), row3(attn_norm), row3(mla_q_norm), row3(mla_kv_norm), row3(ffn_norm)

    h = x.reshape(t, d)
    v_first = None
    for l in range(depth):
        p_a, p_b, p_c = _inproj(h, an, wa, wb, wc, l, tm_proj)
        res = _rwkv(p_a.reshape(bsz, s, RW_IN), v_first, mu, vec, w2p, a2p, g2p, v1p, v2p, l, bsz, min(512, s))
        if l == 0:
            y_a, v_first = res
        else:
            y_a = res
        y_b = _retention(p_b.reshape(bsz, s, RET_IN), tab, bsz)
        qt, k, vt = _mla_proj(p_c, tab, qn, kn, wqt, wk, wvt, l, tm_proj)
        y_ct = _flash(qt, k.reshape(bsz, s, hp), vt, min(1024, s))
        fin = final_norm[None, :] if l == depth - 1 else None
        h = _ffn(h, y_a.reshape(t, RW_DIM), y_b.reshape(t, RET_DIM), y_ct, wo, fn, wgu, wd, fin, l, tm)
    return h.reshape(bsz, s, d)
```

```python
import functools
import math

import jax
import jax.numpy as jnp
import numpy as np
from jax import lax
from jax.experimental import pallas as pl
from jax.experimental.pallas import tpu as pltpu

F32 = jnp.float32
BF16 = jnp.bfloat16

NORM_EPS = 1e-6
ROPE_BASE = 10000.0
LANES = 128

RW_HEADS = 4
RW_HEAD_DIM = 64
RW_DIM = RW_HEADS * RW_HEAD_DIM
RW_DECAY_RANK = 32
RW_A_RANK = 32
RW_V_RANK = 32
RW_GATE_RANK = 64
RW_GN_EPS = 64e-5
RW_IN = 3 * RW_DIM + RW_DECAY_RANK + RW_A_RANK + RW_GATE_RANK
RW_LOWRANK = RW_DECAY_RANK + RW_A_RANK + RW_GATE_RANK
RW_CHUNK = 64

RET_HEADS = 4
RET_HEAD_DIM = 64
RET_DIM = RET_HEADS * RET_HEAD_DIM
RET_CHUNK = 128
RET_BLOCK = 512
RET_IN = 4 * RET_DIM
RET_FREQS = RET_HEAD_DIM // 2

MLA_HEADS = 8
MLA_NOPE = 64
MLA_ROPE = 32
MLA_V = 64
MLA_Q_RANK = 384
MLA_KV_RANK = 256
MLA_DIM = MLA_HEADS * MLA_V
MLA_IN = MLA_Q_RANK + MLA_KV_RANK + MLA_ROPE
MLA_HEAD_PAD = 128
MLA_C_PAD = 768
MLA_FREQS = MLA_ROPE // 2
MLA_VT_ROWS = MLA_V + 16

TAB_RET_COS = 0
TAB_RET_SIN = RET_FREQS
TAB_MLA_COS = 2 * RET_FREQS
TAB_MLA_SIN = 2 * RET_FREQS + MLA_FREQS

V7X_VMEM_LIMIT_BYTES = 56 * 1024 * 1024


MM_SUB_ROWS = 256


def _cparams(semantics):
    return pltpu.CompilerParams(dimension_semantics=semantics, vmem_limit_bytes=V7X_VMEM_LIMIT_BYTES)


def _dot(a, b):
    return jnp.dot(a, b, preferred_element_type=F32)


def _dot_nt(a, b):
    return lax.dot_general(a, b, (((1,), (1,)), ((), ())), preferred_element_type=F32)


def _dot_tn(a, b):
    return lax.dot_general(a, b, (((0,), (0,)), ((), ())), preferred_element_type=F32)


def _split_dot(x, m_bf16, passes):
    acc = None
    rem = x
    for _ in range(passes):
        piece = rem.astype(BF16)
        part = _dot(piece, m_bf16)
        acc = part if acc is None else acc + part
        rem = rem - piece.astype(F32)
    return acc


def _split_dot_left(m_bf16, x):
    acc = None
    rem = x
    for _ in range(3):
        piece = rem.astype(BF16)
        part = _dot(m_bf16, piece)
        acc = part if acc is None else acc + part
        rem = rem - piece.astype(F32)
    return acc


def _lane_groups(x, width, picks):
    lane = lax.broadcasted_iota(jnp.int32, x.shape, 1)
    rolled = {0: x}
    out = jnp.zeros_like(x)
    for g, src in enumerate(picks):
        if src is None:
            continue
        shift = (g * width - src) % LANES
        if shift not in rolled:
            rolled[shift] = pltpu.roll(x, shift, 1)
        out = jnp.where((lane >= g * width) & (lane < (g + 1) * width), rolled[shift], out)
    return out


def _rms(x, g):
    return x * lax.rsqrt(jnp.mean(x * x, axis=-1, keepdims=True) + NORM_EPS) * g


def _layer_spec(shape, layer, ngrid):
    zeros = (0,) * len(shape)
    if ngrid == 1:
        return pl.BlockSpec((None,) + tuple(shape), lambda i: (layer,) + zeros)
    return pl.BlockSpec((None,) + tuple(shape), lambda i, j: (layer,) + zeros)


def _const_spec(a, ngrid):
    zeros = (0,) * a.ndim
    if ngrid == 1:
        return pl.BlockSpec(a.shape, lambda i: zeros)
    return pl.BlockSpec(a.shape, lambda i, j: zeros)


def _inproj_kernel(x_ref, g_ref, wa_ref, wb_ref, wc_ref, pa_ref, pb_ref, pc_ref):
    for r in range(x_ref.shape[0] // MM_SUB_ROWS):
        rows = slice(r * MM_SUB_ROWS, (r + 1) * MM_SUB_ROWS)
        hn = _rms(x_ref[rows, :], g_ref[...]).astype(BF16)
        pa_ref[rows, :] = _dot(hn, wa_ref[...])
        pb_ref[rows, :] = _dot(hn, wb_ref[...])
        pc_ref[rows, :] = _dot(hn, wc_ref[...])


def _inproj(x, g, wa, wb, wc, layer, tm):
    t, d = x.shape
    row = lambda n: pl.BlockSpec((tm, n), lambda i: (i, 0))
    ws = (wa, wb, wc)
    return pl.pallas_call(
        _inproj_kernel,
        grid=(t // tm,),
        in_specs=[row(d), _layer_spec(g.shape[1:], layer, 1)] + [_layer_spec(w.shape[1:], layer, 1) for w in ws],
        out_specs=[row(w.shape[2]) for w in ws],
        out_shape=[jax.ShapeDtypeStruct((t, w.shape[2]), F32) for w in ws],
        compiler_params=_cparams(("parallel",)),
        name="inproj",
    )(x, g, wa, wb, wc)


_VEC_ROWS = 16
(_V_W0, _V_A0, _V_KK, _V_KA, _V_RK, _V_GNW, _V_GNB, _V_V0) = range(8)


def _rwkv_kernel(*refs, has_vres, nb, tb):
    if has_vres:
        (p_ref, vf_ref, mu_ref, vec_ref, w2_ref, a2_ref, g2_ref, v1_ref, v2_ref, bd_ref, tri_ref,
         o_ref, carry_ref, s_ref, r_s, k_s, v_s, lw_s, kk_s, ka_s, g_s, bon_s, y_s) = refs
    else:
        (p_ref, mu_ref, vec_ref, w2_ref, a2_ref, g2_ref, bd_ref, tri_ref,
         o_ref, vfo_ref, carry_ref, s_ref, r_s, k_s, v_s, lw_s, kk_s, ka_s, g_s, bon_s, y_s) = refs

    L = RW_CHUNK
    C = RW_DIM
    tstep = pl.program_id(1)

    @pl.when(tstep == 0)
    def _():
        carry_ref[...] = jnp.zeros_like(carry_ref)
        s_ref[...] = jnp.zeros_like(s_ref)

    vec = vec_ref[...]
    row = lambda i: vec[i:i + 1, :]
    bd = bd_ref[...]
    mu = mu_ref[...]

    def bdsum(x):
        return _split_dot(x, bd, 2)

    def prep(b, _):
        p = p_ref[b]
        shifted = pltpu.roll(p, 1, 0)
        first = lax.broadcasted_iota(jnp.int32, p.shape, 0) == 0
        p_prev = jnp.where(first, carry_ref[b], shifted)
        carry_ref[b] = p[tb - 1:tb, :]
        ps = p + (p_prev - p) * mu
        r = ps[:, 0:C]
        k = ps[:, C:2 * C]
        v = ps[:, 2 * C:3 * C]
        lr = ps[:, 3 * C:3 * C + RW_LOWRANK]
        z = row(_V_W0) + _dot(jnp.tanh(lr).astype(BF16), w2_ref[...])
        lw = -math.exp(-0.5) * jax.nn.sigmoid(z)
        a = jax.nn.sigmoid(row(_V_A0) + _dot(lr.astype(BF16), a2_ref[...]))
        g = _dot(jax.nn.sigmoid(lr).astype(BF16), g2_ref[...])
        if has_vres:
            lat = _dot(v.astype(BF16), v1_ref[...]).astype(BF16)
            v = v + (vf_ref[b] - v) * jax.nn.sigmoid(row(_V_V0) + _dot(lat, v2_ref[...]))
        else:
            vfo_ref[b] = v
        kk = k * row(_V_KK)
        kk = kk * lax.rsqrt(jnp.maximum(bdsum(kk * kk), 1e-24))
        k2 = k * (1.0 + (a - 1.0) * row(_V_KA))
        r_s[b] = r
        k_s[b] = k2
        v_s[b] = v
        lw_s[b] = lw
        kk_s[b] = kk
        ka_s[b] = kk * a
        g_s[b] = g
        bon_s[b] = bdsum(r * k2 * row(_V_RK)) * v
        return 0

    for b in range(nb):
        prep(b, 0)

    hl = RW_HEADS * L
    r_i = lax.broadcasted_iota(jnp.int32, (hl, C), 0)
    c_i = lax.broadcasted_iota(jnp.int32, (hl, C), 1)
    headmask = (r_i // L) == (c_i // RW_HEAD_DIM)
    t_i = lax.broadcasted_iota(jnp.int32, (L, hl), 0)
    s_i = lax.broadcasted_iota(jnp.int32, (L, hl), 1) % L
    strict = t_i > s_i
    incl = t_i >= s_i
    tri = tri_ref[...]

    def masked4(x):
        return jnp.where(headmask, jnp.concatenate([x] * RW_HEADS, axis=0), 0.0).astype(BF16)

    def chunk(j, _):
        sl = pl.ds(pl.multiple_of(j * L, L), L)
        bs = range(nb)
        r = [r_s[b, sl, :] for b in bs]
        k2 = [k_s[b, sl, :] for b in bs]
        v = [v_s[b, sl, :] for b in bs]
        lw = [lw_s[b, sl, :] for b in bs]
        kk = [kk_s[b, sl, :] for b in bs]
        ka = [ka_s[b, sl, :] for b in bs]
        c = [_split_dot_left(tri, lw[b]) for b in bs]
        c_last = [c[b][L - 1:L, :] for b in bs]
        sig = [0.5 * c_last[b] for b in bs]
        e_neg = [jnp.exp(sig[b] - c[b]) for b in bs]
        kt = [k2[b] * e_neg[b] for b in bs]
        bt = [ka[b] * e_neg[b] for b in bs]
        x = [jnp.concatenate([-(kk[b] * jnp.exp(c[b] - lw[b] - sig[b])), r[b] * jnp.exp(c[b] - sig[b])],
                             axis=0).astype(BF16) for b in bs]
        s_old = [s_ref[b] for b in bs]
        kbs_w = [jnp.concatenate([masked4(kt[b]), masked4(bt[b]), (s_old[b] * jnp.exp(sig[b])).astype(BF16)], axis=0)
                 for b in bs]
        abx = [_dot_nt(x[b], kbs_w[b]) for b in bs]
        ab = [abx[b][:, 0:2 * hl] for b in bs]
        xs = [abx[b][:, 2 * hl:2 * hl + C] for b in bs]
        a_k = [jnp.concatenate([jnp.where(strict, ab[b][0:L, 0:hl], 0.0), jnp.where(incl, ab[b][L:2 * L, 0:hl], 0.0)],
                               axis=0).astype(BF16) for b in bs]
        n_w = [jnp.where(strict, ab[b][0:L, hl:2 * hl], 0.0) for b in bs]
        a_rb = [jnp.where(incl, ab[b][L:2 * L, hl:2 * hl], 0.0).astype(BF16) for b in bs]
        av = [_dot(a_k[b], masked4(v[b])) for b in bs]
        u = [av[b][0:L] + xs[b][0:L] for b in bs]
        for i in range(6):
            n_b = [n_w[b].astype(BF16) for b in bs]
            if i < 5:
                nu = [_dot(n_b[b], jnp.concatenate([masked4(u[b]), masked4(n_w[b])], axis=1)) for b in bs]
                u = [u[b] + nu[b][:, 0:C] for b in bs]
                n_w = [nu[b][:, C:2 * C] for b in bs]
            else:
                u = [u[b] + _dot(n_b[b], masked4(u[b])) for b in bs]
        y = [av[b][L:2 * L] + xs[b][L:2 * L] + _dot(a_rb[b], masked4(u[b])) for b in bs]
        for b in bs:
            y_s[b, sl, :] = y[b]
        scale = [jnp.exp(c_last[b] - sig[b]) for b in bs]
        kbs = [jnp.concatenate([kt[b] * scale[b], bt[b] * scale[b]], axis=0).astype(BF16) for b in bs]
        vu = [jnp.concatenate([v[b], u[b]], axis=0).astype(BF16) for b in bs]
        upd = [_dot_tn(vu[b], kbs[b]) for b in bs]
        for b in bs:
            s_ref[b] = s_old[b] * jnp.exp(c_last[b]) + jnp.where(headmask, upd[b], 0.0)
        return 0

    lax.fori_loop(0, tb // L, chunk, 0)

    bs = range(nb)
    mean = [bdsum(y_s[b]) * (1.0 / RW_HEAD_DIM) for b in bs]
    d = [y_s[b] - mean[b] for b in bs]
    var = [bdsum(d[b] * d[b]) * (1.0 / RW_HEAD_DIM) for b in bs]
    for b in bs:
        yn = d[b] * lax.rsqrt(var[b] + RW_GN_EPS) * row(_V_GNW) + row(_V_GNB)
        o_ref[b] = ((yn + bon_s[b]) * g_s[b]).astype(o_ref.dtype)


def _block_ones(n, blk):
    i = np.arange(n)
    return (i[:, None] // blk == i[None, :] // blk).astype(np.float32)


def _rwkv(p_a, v_first, mu, vec, w2p, a2p, g2p, v1p, v2p, layer, nb, tb):
    bsz, s, _ = p_a.shape
    has_vres = v_first is not None
    bd = jnp.asarray(_block_ones(RW_DIM, RW_HEAD_DIM), BF16)
    tri = jnp.asarray(np.tril(np.ones((RW_CHUNK, RW_CHUNK), np.float32)), BF16)
    blk = lambda n: pl.BlockSpec((nb, tb, n), lambda i, j: (i, j, 0))
    lay = lambda a, l: _layer_spec(a.shape[1:], l, 2)
    common = [lay(mu, layer), lay(vec, layer), lay(w2p, layer), lay(a2p, layer), lay(g2p, layer)]
    consts = [_const_spec(bd, 2), _const_spec(tri, 2)]
    if has_vres:
        args = (p_a, v_first, mu, vec, w2p, a2p, g2p, v1p, v2p, bd, tri)
        in_specs = [blk(RW_IN), blk(RW_DIM)] + common + [lay(v1p, layer - 1), lay(v2p, layer - 1)] + consts
        out_specs = blk(RW_DIM)
        out_shape = jax.ShapeDtypeStruct((bsz, s, RW_DIM), BF16)
    else:
        args = (p_a, mu, vec, w2p, a2p, g2p, bd, tri)
        in_specs = [blk(RW_IN)] + common + consts
        out_specs = [blk(RW_DIM), blk(RW_DIM)]
        out_shape = [jax.ShapeDtypeStruct((bsz, s, RW_DIM), BF16), jax.ShapeDtypeStruct((bsz, s, RW_DIM), F32)]
    big = pltpu.VMEM((nb, tb, RW_DIM), F32)
    scratch = [pltpu.VMEM((nb, 1, RW_IN), F32), pltpu.VMEM((nb, RW_DIM, RW_DIM), F32)] + [big] * 9
    return pl.pallas_call(
        functools.partial(_rwkv_kernel, has_vres=has_vres, nb=nb, tb=tb),
        grid=(bsz // nb, s // tb),
        in_specs=in_specs,
        out_specs=out_specs,
        out_shape=out_shape,
        scratch_shapes=scratch,
        compiler_params=_cparams(("parallel", "arbitrary")),
        name="rwkv7",
    )(*args)


def _ret_tables():
    h, d, c = RET_HEADS, RET_HEAD_DIM, RET_CHUNK
    log_gamma = np.log(1.0 - 2.0 ** (-5.0 - np.arange(h, dtype=np.float64)))
    idx = np.arange(c, dtype=np.float64)
    diff = idx[:, None] - idx[None, :]
    dmask = np.where(diff >= 0, np.exp(log_gamma[:, None, None] * np.maximum(diff, 0.0)), 0.0)
    lane = np.arange(RET_DIM)
    head_qk = (lane % (RET_DIM // 2)) // (d // 2)
    head_v = lane // d
    qdec = np.exp(log_gamma[head_qk][None, :] * (idx[:, None] + 1.0))
    kdec = np.exp(log_gamma[head_qk][None, :] * (c - 1.0 - idx[:, None])) * d ** -0.5
    hm_qk = (np.arange(h)[:, None, None] == head_qk[None, None, :]) * np.ones((1, c, 1))
    hm_v = (np.arange(h)[:, None, None] == head_v[None, None, :]) * np.ones((1, c, 1))
    block = head_qk[:, None] == head_v[None, :]
    rdec = np.where(block, np.exp(log_gamma[head_qk] * c)[:, None], 0.0)
    f = lambda a: jnp.asarray(a, F32)
    return (f(dmask.reshape(h * c, c) * d ** -0.5), f(qdec), f(kdec), f(hm_qk.reshape(h * c, RET_DIM)),
            f(hm_v.reshape(h * c, RET_DIM)), f(block), f(rdec))


def _ret_kernel(p_ref, tab_ref, dmask_ref, qdec_ref, kdec_ref, hmqk_ref, hmv_ref, block_ref, rdec_ref,
                bd_ref, o_ref, r_ref, *, nb):
    c, dm = RET_CHUNK, RET_DIM
    half = dm // 2

    @pl.when(pl.program_id(0) == 0)
    def _():
        r_ref[...] = jnp.zeros_like(r_ref)

    bs = range(nb)
    units = [(b, ci) for ci in range(p_ref.shape[1] // c) for b in bs]
    rows = lambda ci: slice(ci * c, (ci + 1) * c)
    cs = [(_lane_groups(tab_ref[b, rows(ci), :], RET_FREQS, [TAB_RET_COS] * RET_HEADS),
           _lane_groups(tab_ref[b, rows(ci), :], RET_FREQS, [TAB_RET_SIN] * RET_HEADS)) for b, ci in units]
    p = [p_ref[b, rows(ci), :] for b, ci in units]
    us = range(len(units))

    def rope(x, t):
        cos, sin = t
        x1, x2 = x[:, :half], x[:, half:]
        return jnp.concatenate([x1 * cos - x2 * sin, x2 * cos + x1 * sin], axis=1)

    q = [rope(p[u][:, 0:dm], cs[u]) for u in us]
    k = [rope(p[u][:, dm:2 * dm], cs[u]) for u in us]
    vb = [p[u][:, 2 * dm:3 * dm].astype(BF16) for u in us]
    q_bd = [(jnp.concatenate([q[u]] * RET_HEADS, axis=0) * hmqk_ref[...]).astype(BF16) for u in us]
    scores = [(_dot_nt(q_bd[u], k[u].astype(BF16)) * dmask_ref[...]).astype(BF16) for u in us]
    o = [_dot(scores[u], vb[u]) * hmv_ref[...] for u in us]
    upd = [_dot_tn((k[u] * kdec_ref[...]).astype(BF16), vb[u]) for u in us]
    r_in = [None] * len(units)
    for b in bs:
        r = r_ref[b]
        for u, (ub, _) in enumerate(units):
            if ub == b:
                r_in[u] = r
                r = r * rdec_ref[...] + upd[u] * block_ref[...]
        r_ref[b] = r
    cross = [_dot((q[u] * qdec_ref[...]).astype(BF16), r_in[u].astype(BF16)) for u in us]
    y = [o[u][0:c] + o[u][c:2 * c] + o[u][2 * c:3 * c] + o[u][3 * c:4 * c] + cross[u] for u in us]
    ms = [_split_dot(y[u] * y[u], bd_ref[...], 2) * (1.0 / RET_HEAD_DIM) for u in us]
    for u, (b, ci) in enumerate(units):
        g = p[u][:, 3 * dm:4 * dm]
        o_ref[b, rows(ci), :] = (g * jax.nn.sigmoid(g) * y[u] * lax.rsqrt(ms[u] + NORM_EPS)).astype(o_ref.dtype)


def _retention(p_b, tab, nb):
    bsz, s, _ = p_b.shape
    tb = min(RET_BLOCK, s)
    consts = _ret_tables() + (jnp.asarray(_block_ones(RET_DIM, RET_HEAD_DIM), BF16),)
    blk = lambda n: pl.BlockSpec((nb, tb, n), lambda j, i: (i, j, 0))
    full = lambda a: pl.BlockSpec(a.shape, lambda j, i: (0,) * a.ndim)
    assert bsz == nb, "one batch group per time block keeps the state scratch simple"
    return pl.pallas_call(
        functools.partial(_ret_kernel, nb=nb),
        grid=(s // tb, bsz // nb),
        in_specs=[blk(RET_IN), blk(LANES)] + [full(a) for a in consts],
        out_specs=blk(RET_DIM),
        out_shape=jax.ShapeDtypeStruct((bsz, s, RET_DIM), BF16),
        scratch_shapes=[pltpu.VMEM((nb, RET_DIM, RET_DIM), F32)],
        compiler_params=_cparams(("arbitrary", "arbitrary")),
        name="retention",
    )(p_b, tab, *consts)


def _mla_tables():
    pq = np.zeros((2 * MLA_FREQS, LANES), np.float32)
    for j in range(MLA_FREQS):
        pq[j, TAB_MLA_COS + j] = 1.0
        pq[MLA_FREQS + j, TAB_MLA_SIN + j] = 1.0
    return jnp.asarray(pq, BF16)


def _split_dot_nt(m_bf16, x, passes):
    acc = None
    rem = x
    for _ in range(passes):
        piece = rem.astype(BF16)
        part = _dot_nt(m_bf16, piece)
        acc = part if acc is None else acc + part
        rem = rem - piece.astype(F32)
    return acc


def _mla_proj_kernel(pc_ref, tab_ref, qn_ref, kn_ref, wqt_ref, wk_ref, wvt_ref, pq_ref,
                     qt_ref, k_ref, vt_ref):
    pc = pc_ref[...]
    tab = tab_ref[0]
    ck = _lane_groups(tab, MLA_FREQS, [None] * 4 + [TAB_MLA_COS] * 2 + [TAB_MLA_SIN] * 2)
    cs = _split_dot_nt(pq_ref[...], tab, 3)
    cos, sin = cs[0:MLA_FREQS], cs[MLA_FREQS:2 * MLA_FREQS]
    nq = _rms(pc[:, 0:MLA_Q_RANK], qn_ref[...]).astype(BF16)
    nkv = _rms(pc[:, MLA_Q_RANK:MLA_Q_RANK + MLA_KV_RANK], kn_ref[...]).astype(BF16)
    kr = pc[:, MLA_Q_RANK + MLA_KV_RANK:MLA_C_PAD] * ck
    kr = kr + pltpu.roll(kr, LANES - MLA_ROPE, 1)
    lane = lax.broadcasted_iota(jnp.int32, kr.shape, 1)
    kr = jnp.where((lane >= MLA_NOPE) & (lane < MLA_NOPE + MLA_ROPE), kr, 0.0)
    qa = _dot_nt(wqt_ref[...], nq) * ((MLA_NOPE + MLA_ROPE) ** -0.5 * math.log2(math.e))
    zeros = jnp.zeros((MLA_HEAD_PAD - MLA_NOPE - MLA_ROPE, qa.shape[1]), qt_ref.dtype)
    for h in range(MLA_HEADS):
        b0 = h * MLA_HEAD_PAD
        b1, b2, b3 = b0 + MLA_NOPE, b0 + MLA_NOPE + MLA_FREQS, b0 + MLA_NOPE + MLA_ROPE
        x1, x2 = qa[b1:b2], qa[b2:b3]
        qt_ref[0, b0:b1, :] = qa[b0:b1].astype(qt_ref.dtype)
        qt_ref[0, b1:b2, :] = (x1 * cos - x2 * sin).astype(qt_ref.dtype)
        qt_ref[0, b2:b3, :] = (x2 * cos + x1 * sin).astype(qt_ref.dtype)
        qt_ref[0, b3:b0 + MLA_HEAD_PAD, :] = zeros
    k_nope = _dot(nkv, wk_ref[...])
    for h in range(MLA_HEADS):
        hs = slice(h * MLA_HEAD_PAD, (h + 1) * MLA_HEAD_PAD)
        k_ref[:, hs] = (k_nope[:, hs] + kr).astype(k_ref.dtype)
    vt = _dot_nt(wvt_ref[...], nkv).astype(vt_ref.dtype)
    ones = jnp.ones((MLA_VT_ROWS - MLA_V, vt.shape[1]), vt_ref.dtype)
    for h in range(MLA_HEADS):
        vt_ref[0, h * MLA_VT_ROWS:h * MLA_VT_ROWS + MLA_V, :] = vt[h * MLA_V:(h + 1) * MLA_V, :]
        vt_ref[0, h * MLA_VT_ROWS + MLA_V:(h + 1) * MLA_VT_ROWS, :] = ones


def _mla_proj(p_c, tab, qn, kn, wqt, wk, wvt, layer, tm):
    t = p_c.shape[0]
    bsz, s, _ = tab.shape
    per = s // tm
    row = lambda n: pl.BlockSpec((tm, n), lambda i: (i, 0))
    cols = lambda n: pl.BlockSpec((1, n, tm), lambda i: (i // per, 0, i % per))
    pq = _mla_tables()
    hp = MLA_HEADS * MLA_HEAD_PAD
    lay = lambda a: _layer_spec(a.shape[1:], layer, 1)
    return pl.pallas_call(
        _mla_proj_kernel,
        grid=(t // tm,),
        in_specs=[row(MLA_C_PAD), pl.BlockSpec((1, tm, LANES), lambda i: (i // per, i % per, 0)),
                  lay(qn), lay(kn), lay(wqt), lay(wk), lay(wvt),
                  _const_spec(pq, 1)],
        out_specs=[cols(hp), row(hp), cols(MLA_HEADS * MLA_VT_ROWS)],
        out_shape=[jax.ShapeDtypeStruct((bsz, hp, s), BF16), jax.ShapeDtypeStruct((t, hp), BF16),
                   jax.ShapeDtypeStruct((bsz, MLA_HEADS * MLA_VT_ROWS, s), BF16)],
        compiler_params=_cparams(("parallel",)),
        name="mla_proj",
    )(p_c, tab, qn, kn, wqt, wk, wvt, pq)


_NEG = -1e30
FLASH_ISSUE_AHEAD = 4
FLASH_KV_SUB = 256
FLASH_Q_SUB = 512


def _flash_units(t, diagonal):
    units = []
    for q0 in range(0, t, FLASH_Q_SUB):
        for kv0 in range(0, t, FLASH_KV_SUB):
            for h in range(MLA_HEADS):
                if not diagonal:
                    units.append((h, kv0, q0, q0 + FLASH_Q_SUB, False))
                    continue
                q_lo = max(q0, kv0)
                if q_lo >= q0 + FLASH_Q_SUB:
                    continue
                units.append((h, kv0, q_lo, q0 + FLASH_Q_SUB, kv0 + FLASH_KV_SUB - 1 > q_lo))
    return units


def _flash_kernel(qi_ref, ki_ref, qt_ref, k_ref, vt_ref, o_ref, m_ref, l_ref, acc_ref, *, t):
    pair = pl.program_id(1)
    qi = qi_ref[pair]
    ki = ki_ref[pair]

    @pl.when(ki == 0)
    def _():
        m_ref[...] = jnp.full_like(m_ref, _NEG)
        l_ref[...] = jnp.zeros_like(l_ref)
        acc_ref[...] = jnp.zeros_like(acc_ref)

    def step(diagonal):
        units = _flash_units(t, diagonal)

        def scores(unit):
            h, kv0, q_lo, q_hi, _ = unit
            hs = slice(h * MLA_HEAD_PAD, (h + 1) * MLA_HEAD_PAD)
            return _dot(k_ref[0, kv0:kv0 + FLASH_KV_SUB, hs], qt_ref[0, hs, q_lo:q_hi])

        ahead = [scores(u) for u in units[:FLASH_ISSUE_AHEAD]]
        for i, (h, kv0, q_lo, q_hi, masked) in enumerate(units):
            vs = slice(h * MLA_V, (h + 1) * MLA_V)
            qs = slice(q_lo, q_hi)
            st = ahead.pop(0)
            if i + FLASH_ISSUE_AHEAD < len(units):
                ahead.append(scores(units[i + FLASH_ISSUE_AHEAD]))
            if masked:
                kv_i = lax.broadcasted_iota(jnp.int32, st.shape, 0) + kv0
                q_i = lax.broadcasted_iota(jnp.int32, st.shape, 1) + q_lo
                st = jnp.where(kv_i <= q_i, st, _NEG)
            m_old = m_ref[h:h + 1, qs]
            m_new = jnp.maximum(m_old, jnp.max(st, axis=0, keepdims=True))
            alpha = jnp.exp2(m_old - m_new)
            p = jnp.exp2(st - m_new).astype(BF16)
            pv = _dot(vt_ref[0, h * MLA_VT_ROWS:(h + 1) * MLA_VT_ROWS, kv0:kv0 + FLASH_KV_SUB], p)
            l_ref[h:h + 1, qs] = alpha * l_ref[h:h + 1, qs] + pv[MLA_V:MLA_V + 1, :]
            m_ref[h:h + 1, qs] = m_new
            acc_ref[vs, qs] = acc_ref[vs, qs] * alpha + pv[0:MLA_V, :]

    @pl.when(ki < qi)
    def _():
        step(False)

    @pl.when(ki == qi)
    def _():
        step(True)
        for h in range(MLA_HEADS):
            vs = slice(h * MLA_V, (h + 1) * MLA_V)
            o_ref[0, vs, :] = (acc_ref[vs, :] / l_ref[h:h + 1, :]).astype(o_ref.dtype)


def _flash(qt, k, vt, t):
    bsz, s, hp = k.shape
    tq = tk = t
    nq = s // t
    pairs = [(i, j) for i in range(nq) for j in range(i + 1)]
    qi_tab = jnp.asarray(np.asarray([p[0] for p in pairs], np.int32))
    ki_tab = jnp.asarray(np.asarray([p[1] for p in pairs], np.int32))
    grid_spec = pltpu.PrefetchScalarGridSpec(
        num_scalar_prefetch=2,
        grid=(bsz, len(pairs)),
        in_specs=[pl.BlockSpec((1, hp, tq), lambda b, p, qi, ki: (b, 0, qi[p])),
                  pl.BlockSpec((1, tk, hp), lambda b, p, qi, ki: (b, ki[p], 0)),
                  pl.BlockSpec((1, MLA_HEADS * MLA_VT_ROWS, tk), lambda b, p, qi, ki: (b, 0, ki[p]))],
        out_specs=pl.BlockSpec((1, MLA_DIM, tq), lambda b, p, qi, ki: (b, 0, qi[p])),
        scratch_shapes=[pltpu.VMEM((MLA_HEADS, tq), F32), pltpu.VMEM((MLA_HEADS, tq), F32),
                        pltpu.VMEM((MLA_DIM, tq), F32)],
    )
    return pl.pallas_call(
        functools.partial(_flash_kernel, t=t),
        grid_spec=grid_spec,
        out_shape=jax.ShapeDtypeStruct((bsz, MLA_DIM, s), BF16),
        compiler_params=_cparams(("parallel", "arbitrary")),
        name="mla_flash",
    )(qi_tab, ki_tab, qt, k, vt)


def _ffn_kernel(h_ref, ya_ref, yb_ref, yct_ref, woa_ref, wob_ref, woc_ref, fg_ref, wg_ref, wu_ref, wd_ref, *rest,
                final, tm):
    if final:
        fin_ref, o_ref, h1_ref, hn_ref = rest
    else:
        o_ref, h1_ref, hn_ref = rest
    h1 = (h_ref[...] + _dot(ya_ref[...], woa_ref[...]) + _dot(yb_ref[...], wob_ref[...])
          + _dot_tn(yct_ref[0], woc_ref[...]))
    h1_ref[...] = h1
    hn_ref[...] = _rms(h1, fg_ref[...]).astype(BF16)

    def gate_up(r):
        x = hn_ref[r * MM_SUB_ROWS:(r + 1) * MM_SUB_ROWS, :]
        return _dot(x, wg_ref[...]), _dot(x, wu_ref[...])

    nsub = tm // MM_SUB_ROWS
    nxt = gate_up(0)
    for r in range(nsub):
        rows = slice(r * MM_SUB_ROWS, (r + 1) * MM_SUB_ROWS)
        gate, up = nxt
        if r + 1 < nsub:
            nxt = gate_up(r + 1)
        act = (gate * jax.nn.sigmoid(gate) * up).astype(BF16)
        out = h1_ref[rows, :] + _dot(act, wd_ref[...])
        if final:
            out = _rms(out, fin_ref[...])
        o_ref[rows, :] = out


def _ffn(h, ya, yb, yct, wo, fg, wgu, wd, fin, layer, tm):
    t, d = h.shape
    dff = wd.shape[1]
    per = yct.shape[2] // tm
    row = lambda n: pl.BlockSpec((tm, n), lambda i: (i, 0))
    once = pl.Buffered(1)
    final = fin is not None
    in_specs = [row(d), row(RW_DIM), row(RET_DIM),
                pl.BlockSpec((1, MLA_DIM, tm), lambda i: (i // per, 0, i % per)),
                pl.BlockSpec((None, RW_DIM, d), lambda i: (layer, 0, 0), pipeline_mode=once),
                pl.BlockSpec((None, RET_DIM, d), lambda i: (layer, 1, 0), pipeline_mode=once),
                pl.BlockSpec((None, MLA_DIM, d), lambda i: (layer, 1, 0), pipeline_mode=once),
                _layer_spec(fg.shape[1:], layer, 1),
                pl.BlockSpec((None, d, dff), lambda i: (layer, 0, 0), pipeline_mode=once),
                pl.BlockSpec((None, d, dff), lambda i: (layer, 0, 1), pipeline_mode=once),
                pl.BlockSpec((None, dff, d), lambda i: (layer, 0, 0), pipeline_mode=once)]
    args = [h, ya, yb, yct, wo, wo, wo, fg, wgu, wgu, wd]
    if final:
        in_specs.append(_const_spec(fin, 1))
        args.append(fin)
    return pl.pallas_call(
        functools.partial(_ffn_kernel, final=final, tm=tm),
        grid=(t // tm,),
        in_specs=in_specs,
        out_specs=row(d),
        out_shape=jax.ShapeDtypeStruct((t, d), F32),
        scratch_shapes=[pltpu.VMEM((tm, d), F32), pltpu.VMEM((tm, d), BF16)],
        compiler_params=_cparams(("parallel",)),
        name="outproj_ffn",
    )(*args)


def _relayout(w, m, transpose=False):
    out = "lki" if transpose else "lik"
    return jnp.einsum("lij,jk->" + out, w.astype(BF16), jnp.asarray(m, BF16), preferred_element_type=BF16)


def _rot_half_into(m, src, dst, n, sign=1.0):
    for j in range(n // 2):
        m[src + n // 2 + j, dst + j] = -sign
        m[src + j, dst + n // 2 + j] = sign


def _prep_in_weights(w_in):
    perm = np.zeros((2 * RET_DIM, 2 * RET_DIM), np.float32)
    for part in range(2):
        for hh in range(RET_HEADS):
            for half in range(2):
                for j in range(RET_FREQS):
                    src = part * RET_DIM + hh * RET_HEAD_DIM + half * RET_FREQS + j
                    dst = part * RET_DIM + half * (RET_DIM // 2) + hh * RET_FREQS + j
                    perm[src, dst] = 1.0
    lat = MLA_Q_RANK + MLA_KV_RANK
    mc = np.zeros((MLA_IN, MLA_C_PAD), np.float32)
    mc[np.arange(lat), np.arange(lat)] = 1.0
    mc[lat + np.arange(MLA_ROPE), lat + MLA_NOPE + np.arange(MLA_ROPE)] = 1.0
    _rot_half_into(mc, lat, lat + MLA_NOPE + MLA_ROPE, MLA_ROPE)
    b0 = RW_IN
    c0 = RW_IN + RET_IN
    wa = w_in[..., :b0].astype(BF16)
    wb = jnp.concatenate([_relayout(w_in[..., b0:b0 + 2 * RET_DIM], perm), w_in[..., b0 + 2 * RET_DIM:c0].astype(BF16)],
                         axis=-1)
    wc = _relayout(w_in[..., c0:], mc)
    return wa, wb, wc


def _prep_mla_weights(w_q_up, w_kv_up):
    qd = MLA_NOPE + MLA_ROPE
    kvd = MLA_NOPE + MLA_V
    hp = MLA_HEADS * MLA_HEAD_PAD
    mq = np.zeros((MLA_HEADS * qd, hp), np.float32)
    mk = np.zeros((MLA_HEADS * kvd, hp), np.float32)
    mv = np.zeros((MLA_HEADS * kvd, MLA_DIM), np.float32)
    for h in range(MLA_HEADS):
        for c in range(qd):
            mq[h * qd + c, h * MLA_HEAD_PAD + c] = 1.0
        for c in range(MLA_NOPE):
            mk[h * kvd + c, h * MLA_HEAD_PAD + c] = 1.0
        for c in range(MLA_V):
            mv[h * kvd + MLA_NOPE + c, h * MLA_V + c] = 1.0
    return (_relayout(w_q_up, mq, transpose=True), _relayout(w_kv_up, mk),
            _relayout(w_kv_up, mv, transpose=True))


def _pad_rows(w, start, total):
    return jnp.pad(w, ((0, 0), (start, total - start - w.shape[1]), (0, 0)))


def _rope_table(positions):
    inv_ret = ROPE_BASE ** (-jnp.arange(0, RET_HEAD_DIM, 2, dtype=F32) / RET_HEAD_DIM)
    inv_mla = ROPE_BASE ** (-jnp.arange(0, MLA_ROPE, 2, dtype=F32) / MLA_ROPE)
    inv = jnp.concatenate([inv_ret, inv_ret, inv_mla, inv_mla, jnp.zeros((LANES - TAB_MLA_SIN - MLA_FREQS,), F32)])
    lane = np.arange(LANES)
    is_cos = (lane < TAB_RET_SIN) | ((lane >= TAB_MLA_COS) & (lane < TAB_MLA_SIN))
    ang = positions.astype(F32)[..., None] * inv
    return jnp.where(jnp.asarray(is_cos), jnp.cos(ang), jnp.sin(ang))


def kernel(x, positions, attn_norm, w_in, w_out, rw_mu, rw_w0, rw_w2, rw_a0, rw_a2, rw_g2, rw_k_k, rw_k_a, rw_r_k, rw_gn_w, rw_gn_b, rw_v0, rw_v1, rw_v2, mla_q_norm, mla_kv_norm, mla_w_q_up, mla_w_kv_up, ffn_norm, w_gate_up, w_down, final_norm):
    bsz, s, d = x.shape
    depth = w_in.shape[0]
    t = bsz * s
    tm = min(512, s)
    tm_proj = min(1024, s)
    hp = MLA_HEADS * MLA_HEAD_PAD

    tab = _rope_table(positions)
    wa, wb, wc = _prep_in_weights(w_in)
    wqt, wk, wvt = _prep_mla_weights(mla_w_q_up, mla_w_kv_up)
    wo, wgu, wd = w_out.astype(BF16), w_gate_up.astype(BF16), w_down.astype(BF16)
    v0 = jnp.concatenate([jnp.zeros((1, RW_DIM), F32), rw_v0], axis=0)
    vec = jnp.stack([rw_w0, rw_a0, rw_k_k, rw_k_a, rw_r_k.reshape(depth, RW_DIM), rw_gn_w, rw_gn_b, v0], axis=1)
    vec = jnp.pad(vec, ((0, 0), (0, _VEC_ROWS - vec.shape[1]), (0, 0)))
    w2p = _pad_rows(rw_w2, 0, RW_LOWRANK).astype(BF16)
    a2p = _pad_rows(rw_a2, RW_DECAY_RANK, RW_LOWRANK).astype(BF16)
    g2p = _pad_rows(rw_g2, RW_DECAY_RANK + RW_A_RANK, RW_LOWRANK).astype(BF16)
    v1p = jnp.pad(rw_v1, ((0, 0), (0, 0), (0, LANES - RW_V_RANK))).astype(BF16)
    v2p = _pad_rows(rw_v2, 0, LANES).astype(BF16)
    row3 = lambda a: a[:, None, :]
    mu, an, qn, kn, fn = row3(rw_mu), row3(attn_norm), row3(mla_q_norm), row3(mla_kv_norm), row3(ffn_norm)

    h = x.reshape(t, d)
    v_first = None
    for l in range(depth):
        p_a, p_b, p_c = _inproj(h, an, wa, wb, wc, l, tm_proj)
        res = _rwkv(p_a.reshape(bsz, s, RW_IN), v_first, mu, vec, w2p, a2p, g2p, v1p, v2p, l, bsz, min(512, s))
        if l == 0:
            y_a, v_first = res
        else:
            y_a = res
        y_b = _retention(p_b.reshape(bsz, s, RET_IN), tab, bsz)
        qt, k, vt = _mla_proj(p_c, tab, qn, kn, wqt, wk, wvt, l, tm_proj)
        y_ct = _flash(qt, k.reshape(bsz, s, hp), vt, min(1024, s))
        fin = final_norm[None, :] if l == depth - 1 else None
        h = _ffn(h, y_a.reshape(t, RW_DIM), y_b.reshape(t, RET_DIM), y_ct, wo, fn, wgu, wd, fin, l, tm)
    return h.reshape(bsz, s, d)
```
